```python
import math
import jax
import jax.numpy as jnp
from jax import lax
import numpy as np


D_MODEL = 1024
BATCH = 8
SEQ = 4096
DEPTH = 2

HEAD_DIM = 64
ROPE_THETA = 10000.0
NORM_EPS = 1e-6
QBLK = 128

DSW_GROUPS = ((128, 1), (512, 4), (2048, 16))
DSW_HEADS_PER_GROUP = 4
DSW_HEADS = DSW_HEADS_PER_GROUP * len(DSW_GROUPS)
DSW_BLK = 128

MLA_HEADS = (3 * D_MODEL) // (4 * HEAD_DIM)
MLA_Q_RANK = D_MODEL // 4
MLA_KV_RANK = D_MODEL // 8
MLA_NOPE = HEAD_DIM
MLA_ROPE = HEAD_DIM // 2
MLA_V = HEAD_DIM

SSM_INNER = D_MODEL
SSM_HEADDIM = 64
SSM_HEADS = SSM_INNER // SSM_HEADDIM
SSM_STATE = 128
SSM_GROUPS = 2
SSM_CONV = 4
SSM_CHUNK = 128
SSM_CONV_DIM = SSM_INNER + 2 * SSM_GROUPS * SSM_STATE

SB_HEADS = 8
SB_WIDTH = SB_HEADS * HEAD_DIM

FFN_DIM = 2816
FFN_CONV = 3

IN0 = 3 * DSW_HEADS * HEAD_DIM + MLA_Q_RANK + MLA_KV_RANK + MLA_ROPE
OUT0 = DSW_HEADS_PER_GROUP * HEAD_DIM + MLA_HEADS * MLA_V
IN1 = SSM_INNER + SSM_CONV_DIM + SSM_HEADS + 3 * SB_WIDTH
OUT1 = SSM_INNER + SB_WIDTH

kernel_name = 'hybrid_dilated_mla_ssd_stickbreak'


def rmsnorm(x, g):
    xf = x.astype(jnp.float32)
    y = xf * lax.rsqrt(jnp.mean(xf * xf, axis=-1, keepdims=True) + NORM_EPS)
    return (y * g.astype(jnp.float32)).astype(x.dtype)


def group_rmsnorm(x, g, groups):
    shp = x.shape
    xf = x.astype(jnp.float32).reshape(shp[:-1] + (groups, shp[-1] // groups))
    y = xf * lax.rsqrt(jnp.mean(xf * xf, axis=-1, keepdims=True) + NORM_EPS)
    return y.reshape(shp) * g.astype(jnp.float32)


def rope(x, positions):
    d = x.shape[-1]
    half = d // 2
    inv_freq = 1.0 / (ROPE_THETA ** (jnp.arange(half, dtype=jnp.float32) * (2.0 / d)))
    ang = positions.astype(jnp.float32)[..., None] * inv_freq
    cos = jnp.cos(ang)[:, :, None, :]
    sin = jnp.sin(ang)[:, :, None, :]
    xf = x.astype(jnp.float32)
    x1, x2 = xf[..., :half], xf[..., half:]
    return jnp.concatenate([x1 * cos - x2 * sin, x2 * cos + x1 * sin], axis=-1).astype(x.dtype)


def causal_dwconv(x, w, b):
    k = w.shape[0]
    s = x.shape[1]
    xp = jnp.pad(x, ((0, 0), (k - 1, 0), (0, 0)))
    y = b
    for j in range(k):
        y = y + xp[:, j:j + s, :] * w[j]
    return y


def dilated_window_attention(q, k, v, window, dilation):
    bsz, s, h, dh = q.shape
    span = window // dilation
    lsub = s // dilation
    nb = -(-lsub // DSW_BLK)
    lp = nb * DSW_BLK

    def to_blocks(t):
        t = t.reshape(bsz, lsub, dilation, h, dh)
        t = jnp.pad(t, ((0, 0), (0, lp - lsub), (0, 0), (0, 0), (0, 0)))
        return t.reshape(bsz, nb, DSW_BLK, dilation, h, dh)

    def with_prev(t):
        prev = jnp.pad(t, ((0, 0), (1, 0), (0, 0), (0, 0), (0, 0), (0, 0)))[:, :-1]
        return jnp.concatenate([prev, t], axis=2)

    qb, kb, vb = to_blocks(q), to_blocks(k), to_blocks(v)
    kk, vv = with_prev(kb), with_prev(vb)
    scores = jnp.einsum('bnqrhd,bnkrhd->bnqrhk', qb, kk).astype(jnp.float32) * (dh ** -0.5)
    qi = jnp.arange(DSW_BLK)[:, None]
    kj = jnp.arange(2 * DSW_BLK)[None, :]
    rel = qi + DSW_BLK - kj
    key_sub = jnp.arange(nb)[:, None, None] * DSW_BLK + kj[None] - DSW_BLK
    valid = (rel >= 0) & (rel <= span) & (key_sub >= 0)
    scores = jnp.where(valid[None, :, :, None, None, :], scores, -jnp.inf)
    m = jnp.max(scores, axis=-1, keepdims=True)
    p = jnp.exp(scores - m)
    l = jnp.sum(p, axis=-1, keepdims=True)
    out = jnp.einsum('bnqrhk,bnkrhd->bnqrhd', (p / l).astype(v.dtype), vv)
    lse = (m + jnp.log(l))[..., 0]
    out = out.reshape(bsz, lp, dilation, h, dh)[:, :lsub].reshape(bsz, s, h, dh)
    lse = lse.reshape(bsz, lp, dilation, h)[:, :lsub].reshape(bsz, s, h)
    return out, lse


def causal_softmax_attention(q, k, v, scale):
    bsz, s, h, dq = q.shape
    nb = s // QBLK
    qb = q.reshape(bsz, nb, QBLK, h, dq).swapaxes(0, 1)
    key_idx = jnp.arange(s)

    def one_block(args):
        q_blk, i = args
        scores = jnp.einsum('bqhd,bkhd->bhqk', q_blk, k).astype(jnp.float32) * scale
        q_idx = i * QBLK + jnp.arange(QBLK)
        causal = key_idx[None, :] <= q_idx[:, None]
        probs = jax.nn.softmax(jnp.where(causal, scores, -jnp.inf), axis=-1)
        return jnp.einsum('bhqk,bkhd->bqhd', probs.astype(v.dtype), v)

    out = lax.map(one_block, (qb, jnp.arange(nb)))
    return out.swapaxes(0, 1).reshape(bsz, s, h, v.shape[-1])


def stick_breaking_attention(q, k, v):
    bsz, s, h, dh = q.shape
    nb = s // QBLK
    qb = q.reshape(bsz, nb, QBLK, h, dh).swapaxes(0, 1)
    key_idx = jnp.arange(s)

    def one_block(args):
        q_blk, i = args
        z = jnp.einsum('bqhd,bkhd->bhqk', q_blk, k).astype(jnp.float32) * (dh ** -0.5)
        q_idx = i * QBLK + jnp.arange(QBLK)
        strict = key_idx[None, :] < q_idx[:, None]
        log_stay = jnp.where(strict, jax.nn.log_sigmoid(-z), 0.0)
        after = lax.cumsum(log_stay, axis=3, reverse=True) - log_stay
        log_w = jnp.where(strict, jax.nn.log_sigmoid(z) + after, -jnp.inf)
        return jnp.einsum('bhqk,bkhd->bqhd', jnp.exp(log_w).astype(v.dtype), v)

    out = lax.map(one_block, (qb, jnp.arange(nb)))
    return out.swapaxes(0, 1).reshape(bsz, s, h, dh)


def ssd_chunked(x, dt, a, b_mat, c_mat):
    bsz, s, nh, p = x.shape
    g, n = b_mat.shape[2], b_mat.shape[3]
    hg = nh // g
    cl = SSM_CHUNK
    nc = s // cl
    f32 = jnp.float32
    xdt = (x.astype(f32) * dt[..., None]).reshape(bsz, nc, cl, g, hg, p)
    da = (dt * a).reshape(bsz, nc, cl, g, hg)
    bc = b_mat.astype(f32).reshape(bsz, nc, cl, g, n)
    cc = c_mat.astype(f32).reshape(bsz, nc, cl, g, n)
    cs = jnp.cumsum(da, axis=2)
    causal = jnp.tril(jnp.ones((cl, cl), dtype=bool))
    seg = cs[:, :, :, None] - cs[:, :, None, :]
    decay = jnp.exp(jnp.where(causal[None, None, :, :, None, None], seg, -jnp.inf))
    cb = jnp.einsum('bclgn,bcsgn->bclsg', cc, bc)
    y_diag = jnp.einsum('bclsgh,bcsghp->bclghp', cb[..., None] * decay, xdt)
    decay_end = jnp.exp(cs[:, :, -1:] - cs)
    states = jnp.einsum('bclgn,bclgh,bclghp->bcghpn', bc, decay_end, xdt)
    chunk_decay = jnp.exp(cs[:, :, -1])

    def step(hstate, inp):
        st, dec = inp
        return hstate * dec[..., None, None] + st, hstate

    h0 = jnp.zeros((bsz, g, hg, p, n), f32)
    _, prev = lax.scan(step, h0, (jnp.moveaxis(states, 1, 0), jnp.moveaxis(chunk_decay, 1, 0)))
    prev = jnp.moveaxis(prev, 0, 1)
    y_off = jnp.einsum('bclgn,bcghpn,bclgh->bclghp', cc, prev, jnp.exp(cs))
    return (y_diag + y_off).reshape(bsz, s, nh, p)


def even_mixer(h, positions, w_in, q_norm, w_uq, kv_norm, w_ukv, w_out):
    bsz, s, _ = h.shape
    proj = h @ w_in
    nd = DSW_HEADS * HEAD_DIM
    q = proj[..., :nd].reshape(bsz, s, DSW_HEADS, HEAD_DIM)
    k = proj[..., nd:2 * nd].reshape(bsz, s, DSW_HEADS, HEAD_DIM)
    v = proj[..., 2 * nd:3 * nd].reshape(bsz, s, DSW_HEADS, HEAD_DIM)
    o = 3 * nd
    c_q = proj[..., o:o + MLA_Q_RANK]
    o += MLA_Q_RANK
    c_kv = proj[..., o:o + MLA_KV_RANK]
    o += MLA_KV_RANK
    k_pe = proj[..., o:o + MLA_ROPE]

    q = rope(q, positions)
    k = rope(k, positions)
    outs, lses = [], []
    for gi, (window, dilation) in enumerate(DSW_GROUPS):
        sl = slice(gi * DSW_HEADS_PER_GROUP, (gi + 1) * DSW_HEADS_PER_GROUP)
        og, lg = dilated_window_attention(q[:, :, sl], k[:, :, sl], v[:, :, sl], window, dilation)
        outs.append(og)
        lses.append(lg)
    wts = jax.nn.softmax(jnp.stack(lses, axis=0), axis=0)
    y_a = jnp.sum(wts[..., None] * jnp.stack(outs, axis=0).astype(jnp.float32), axis=0).astype(h.dtype)

    qm = (rmsnorm(c_q, q_norm) @ w_uq).reshape(bsz, s, MLA_HEADS, MLA_NOPE + MLA_ROPE)
    q_nope, q_pe = qm[..., :MLA_NOPE], rope(qm[..., MLA_NOPE:], positions)
    kv = (rmsnorm(c_kv, kv_norm) @ w_ukv).reshape(bsz, s, MLA_HEADS, MLA_NOPE + MLA_V)
    k_nope, v_m = kv[..., :MLA_NOPE], kv[..., MLA_NOPE:]
    k_pe = rope(k_pe[:, :, None, :], positions)
    q_full = jnp.concatenate([q_nope, q_pe], axis=-1)
    k_full = jnp.concatenate([k_nope, jnp.broadcast_to(k_pe, (bsz, s, MLA_HEADS, MLA_ROPE))], axis=-1)
    y_b = causal_softmax_attention(q_full, k_full, v_m, (MLA_NOPE + MLA_ROPE) ** -0.5)

    y = jnp.concatenate([y_a.reshape(bsz, s, -1), y_b.reshape(bsz, s, -1)], axis=-1)
    return y @ w_out


def odd_mixer(h, w_in, conv_w, conv_b, dt_bias, a_log, d_skip, ssm_norm, w_out):
    bsz, s, _ = h.shape
    proj = h @ w_in
    z = proj[..., :SSM_INNER]
    o = SSM_INNER
    xbc = proj[..., o:o + SSM_CONV_DIM]
    o += SSM_CONV_DIM
    dt_raw = proj[..., o:o + SSM_HEADS]
    o += SSM_HEADS
    qkv = proj[..., o:]

    xbc = jax.nn.silu(causal_dwconv(xbc, conv_w, conv_b))
    gn = SSM_GROUPS * SSM_STATE
    xs = xbc[..., :SSM_INNER].reshape(bsz, s, SSM_HEADS, SSM_HEADDIM)
    bm = xbc[..., SSM_INNER:SSM_INNER + gn].reshape(bsz, s, SSM_GROUPS, SSM_STATE)
    cm = xbc[..., SSM_INNER + gn:].reshape(bsz, s, SSM_GROUPS, SSM_STATE)
    dt = jax.nn.softplus(dt_raw.astype(jnp.float32) + dt_bias.astype(jnp.float32))
    a = -jnp.exp(a_log.astype(jnp.float32))
    y = ssd_chunked(xs, dt, a, bm, cm)
    y = y + xs.astype(jnp.float32) * d_skip.astype(jnp.float32)[:, None]
    y = y.reshape(bsz, s, SSM_INNER) * jax.nn.silu(z.astype(jnp.float32))
    y_c = group_rmsnorm(y, ssm_norm, SSM_GROUPS).astype(h.dtype)

    q = qkv[..., :SB_WIDTH].reshape(bsz, s, SB_HEADS, HEAD_DIM)
    k = qkv[..., SB_WIDTH:2 * SB_WIDTH].reshape(bsz, s, SB_HEADS, HEAD_DIM)
    v = qkv[..., 2 * SB_WIDTH:].reshape(bsz, s, SB_HEADS, HEAD_DIM)
    y_d = stick_breaking_attention(q, k, v).reshape(bsz, s, SB_WIDTH)

    return jnp.concatenate([y_c, y_d], axis=-1) @ w_out


def conv_ffn(h, w_gate, w_up, conv_w, conv_b, w_down):
    gate = causal_dwconv(h @ w_gate, conv_w, conv_b)
    return (jax.nn.silu(gate) * (h @ w_up)) @ w_down


def setup_inputs(seed: int = 0) -> dict:
    key = jax.random.key(seed)
    ks = iter(jax.random.split(key, 32))
    f32 = jnp.float32

    def dense(fi, fo):
        return jax.random.normal(next(ks), (fi, fo), f32) * fi ** -0.5

    def gain(n):
        return 1.0 + 0.02 * jax.random.normal(next(ks), (n,), f32)

    def bias(n):
        return 0.02 * jax.random.normal(next(ks), (n,), f32)

    def dwconv(kw, c):
        return jax.random.normal(next(ks), (kw, c), f32) * kw ** -0.5

    def dt_bias_init(n):
        dt = jnp.exp(jax.random.uniform(next(ks), (n,), f32, math.log(1e-3), math.log(1e-1)))
        return dt + jnp.log(-jnp.expm1(-dt))

    def a_log_init(n):
        return jnp.log(jax.random.uniform(next(ks), (n,), f32, 1.0, 16.0))

    x = jax.random.normal(next(ks), (BATCH, SEQ, D_MODEL), f32)
    offset = jax.random.randint(next(ks), (BATCH, 1), 0, 1024, dtype=jnp.int32)
    positions = jnp.arange(SEQ, dtype=jnp.int32)[None, :] + offset
    return {
        'x': x,
        'positions': positions,
        'l0_norm_mix': gain(D_MODEL),
        'l0_w_in': dense(D_MODEL, IN0),
        'l0_mla_q_norm': gain(MLA_Q_RANK),
        'l0_mla_w_uq': dense(MLA_Q_RANK, MLA_HEADS * (MLA_NOPE + MLA_ROPE)),
        'l0_mla_kv_norm': gain(MLA_KV_RANK),
        'l0_mla_w_ukv': dense(MLA_KV_RANK, MLA_HEADS * (MLA_NOPE + MLA_V)),
        'l0_w_out': dense(OUT0, D_MODEL),
        'l0_norm_ffn': gain(D_MODEL),
        'l0_ffn_w_gate': dense(D_MODEL, FFN_DIM),
        'l0_ffn_w_up': dense(D_MODEL, FFN_DIM),
        'l0_ffn_conv_w': dwconv(FFN_CONV, FFN_DIM),
        'l0_ffn_conv_b': bias(FFN_DIM),
        'l0_ffn_w_down': dense(FFN_DIM, D_MODEL),
        'l1_norm_mix': gain(D_MODEL),
        'l1_w_in': dense(D_MODEL, IN1),
        'l1_ssm_conv_w': dwconv(SSM_CONV, SSM_CONV_DIM),
        'l1_ssm_conv_b': bias(SSM_CONV_DIM),
        'l1_ssm_dt_bias': dt_bias_init(SSM_HEADS),
        'l1_ssm_a_log': a_log_init(SSM_HEADS),
        'l1_ssm_d': gain(SSM_HEADS),
        'l1_ssm_norm': gain(SSM_INNER),
        'l1_w_out': dense(OUT1, D_MODEL),
        'l1_norm_ffn': gain(D_MODEL),
        'l1_ffn_w_gate': dense(D_MODEL, FFN_DIM),
        'l1_ffn_w_up': dense(D_MODEL, FFN_DIM),
        'l1_ffn_conv_w': dwconv(FFN_CONV, FFN_DIM),
        'l1_ffn_conv_b': bias(FFN_DIM),
        'l1_ffn_w_down': dense(FFN_DIM, D_MODEL),
        'final_norm': gain(D_MODEL),
    }


def reference(x, positions,
              l0_norm_mix, l0_w_in, l0_mla_q_norm, l0_mla_w_uq, l0_mla_kv_norm, l0_mla_w_ukv, l0_w_out,
              l0_norm_ffn, l0_ffn_w_gate, l0_ffn_w_up, l0_ffn_conv_w, l0_ffn_conv_b, l0_ffn_w_down,
              l1_norm_mix, l1_w_in, l1_ssm_conv_w, l1_ssm_conv_b, l1_ssm_dt_bias, l1_ssm_a_log, l1_ssm_d,
              l1_ssm_norm, l1_w_out,
              l1_norm_ffn, l1_ffn_w_gate, l1_ffn_w_up, l1_ffn_conv_w, l1_ffn_conv_b, l1_ffn_w_down,
              final_norm):
    mix_norms = [l0_norm_mix, l1_norm_mix]
    mixers = [
        (l0_w_in, l0_mla_q_norm, l0_mla_w_uq, l0_mla_kv_norm, l0_mla_w_ukv, l0_w_out),
        (l1_w_in, l1_ssm_conv_w, l1_ssm_conv_b, l1_ssm_dt_bias, l1_ssm_a_log, l1_ssm_d, l1_ssm_norm, l1_w_out),
    ]
    ffn_norms = [l0_norm_ffn, l1_norm_ffn]
    ffns = [
        (l0_ffn_w_gate, l0_ffn_w_up, l0_ffn_conv_w, l0_ffn_conv_b, l0_ffn_w_down),
        (l1_ffn_w_gate, l1_ffn_w_up, l1_ffn_conv_w, l1_ffn_conv_b, l1_ffn_w_down),
    ]
    for i in range(DEPTH):
        h = rmsnorm(x, mix_norms[i])
        if i % 2 == 0:
            x = x + even_mixer(h, positions, *mixers[i])
        else:
            x = x + odd_mixer(h, *mixers[i])
        x = x + conv_ffn(rmsnorm(x, ffn_norms[i]), *ffns[i])
    return rmsnorm(x, final_norm)
```

```python
import functools

import jax
import jax.numpy as jnp
from jax import lax
from jax.experimental import pallas as pl
from jax.experimental.pallas import tpu as pltpu

F32 = jnp.float32
BF16 = jnp.bfloat16

LANES = 128
SUBLANES = 8
VMEM_LIMIT_BYTES = 56 * 1024 * 1024

D_MODEL = 1024
HEAD_DIM = 64
ROPE_THETA = 10000.0
NORM_EPS = 1e-6

DSW_GROUPS = ((128, 1), (512, 4), (2048, 16))
DSW_HEADS_PER_GROUP = 4
DSW_HEADS = DSW_HEADS_PER_GROUP * len(DSW_GROUPS)
DSW_BLK = 128
DSW_WIDTH = DSW_HEADS * HEAD_DIM
DSW_GROUP_WIDTH = DSW_HEADS_PER_GROUP * HEAD_DIM

MLA_HEADS = 12
MLA_Q_RANK = 256
MLA_KV_RANK = 128
MLA_NOPE = 64
MLA_ROPE = 32
MLA_V = 64
MLA_SLOT = 128

SSM_INNER = 1024
SSM_HEADDIM = 64
SSM_HEADS = 16
SSM_STATE = 128
SSM_GROUPS = 2
SSM_CONV = 4
SSM_CHUNK = 128
SSM_GROUP_WIDTH = SSM_INNER // SSM_GROUPS
SSM_BC_WIDTH = 2 * SSM_GROUPS * SSM_STATE
SSM_DT_PAD = LANES

SB_HEADS = 8
SB_WIDTH = SB_HEADS * HEAD_DIM
SB_LOG_UNDERFLOW = 104.0

FFN_DIM = 2816
FFN_CONV = 3

NEG_BIG = -1e30


def _params(*sem):
    return pltpu.CompilerParams(dimension_semantics=sem, vmem_limit_bytes=VMEM_LIMIT_BYTES)


def _rms(x, g):
    return x * lax.rsqrt(jnp.mean(x * x, axis=-1, keepdims=True) + NORM_EPS) * g


def _silu(x):
    return x * (1.0 / (1.0 + jnp.exp(-x)))


def _softplus(x):
    return jnp.maximum(x, 0.0) + jnp.log1p(jnp.exp(-jnp.abs(x)))


def _swap_halves(x, half):
    lane = lax.broadcasted_iota(jnp.int32, x.shape, 1)
    up = pltpu.roll(x, LANES - half, 1)
    down = pltpu.roll(x, half, 1)
    return jnp.where((lane & half) == 0, up, down)


def _rope_tile(x, cos, sin, half):
    return x * cos + _swap_halves(x, half) * sin


def _split3(x):
    hi = x.astype(BF16)
    r1 = x - hi.astype(F32)
    mid = r1.astype(BF16)
    lo = (r1 - mid.astype(F32)).astype(BF16)
    return hi, mid, lo


def _dot(a, b):
    return jnp.dot(a, b, preferred_element_type=F32)


def _dot_nt(a, b):
    return lax.dot_general(a, b, (((1,), (1,)), ((), ())), preferred_element_type=F32)


def _keep_head(x2, lane, second):
    mine = (lane >= HEAD_DIM) if second else (lane < HEAD_DIM)
    return jnp.where(mine, x2.astype(F32), 0.0).astype(BF16)


def _rope_table_kernel(pos_ref, freq_ref, sign_ref, cos_ref, sin_ref):
    ang = pos_ref[...] * freq_ref[...]
    cos_ref[...] = jnp.cos(ang)
    sin_ref[...] = jnp.sin(ang) * sign_ref[...]


def _rope_tables(pos_lanes, freq, sign, tm):
    t = pos_lanes.shape[0]
    row = pl.BlockSpec((tm, LANES), lambda i: (i, 0))
    const = pl.BlockSpec((1, LANES), lambda i: (0, 0))
    return pl.pallas_call(
        _rope_table_kernel,
        grid=(t // tm,),
        in_specs=[row, const, const],
        out_specs=[row, row],
        out_shape=[jax.ShapeDtypeStruct((t, LANES), F32)] * 2,
        compiler_params=_params("parallel"),
        name="rope_tables",
    )(pos_lanes, freq, sign)


def _norm_proj_kernel(*refs, rope_tiles, rope_half, kpe_tiles, n_tiles, tn):
    x_ref, g_ref, w_ref = refs[:3]
    pos = 3
    cos_ref = sin_ref = kpe_ref = kpe_s = None
    if rope_tiles or kpe_tiles:
        cos_ref, sin_ref = refs[pos], refs[pos + 1]
        pos += 2
    if kpe_tiles:
        kpe_ref = refs[pos]
        pos += 1
    o_ref, hn_ref = refs[pos], refs[pos + 1]
    if kpe_tiles:
        kpe_s = refs[pos + 2]
    j = pl.program_id(1)

    @pl.when(j == 0)
    def _():
        hn_ref[...] = _rms(x_ref[...].astype(F32), g_ref[...]).astype(BF16)
        if kpe_tiles:
            kpe_s[...] = _rope_tile(kpe_ref[...], cos_ref[...], sin_ref[...], rope_half)

    acc = _dot(hn_ref[...], w_ref[...])

    def plain():
        o_ref[...] = acc.astype(o_ref.dtype)

    def special():
        for c in range(tn // LANES):
            sl = slice(c * LANES, (c + 1) * LANES)
            tile = acc[:, sl]
            if rope_tiles:
                tile = _rope_tile(tile, cos_ref[...], sin_ref[...], rope_half)
            if kpe_tiles:
                tile = tile + kpe_s[...]
            o_ref[:, sl] = tile.astype(o_ref.dtype)

    n_special = max(rope_tiles, kpe_tiles)
    if n_special == 0:
        plain()
    elif n_special >= n_tiles:
        special()
    else:
        pl.when(j < n_special)(special)
        pl.when(j >= n_special)(plain)


def _norm_proj(x, x_col, k_width, g, w, out_dtype, *, tm, tn, cos=None, sin=None, rope_tiles=0,
               rope_half=0, kpe=None, kpe_col=0, kpe_tiles=0, name):
    t = x.shape[0]
    n = w.shape[1]
    n_tiles = n // tn
    in_specs = [
        pl.BlockSpec((tm, k_width), lambda i, j: (i, x_col)),
        pl.BlockSpec((1, k_width), lambda i, j: (0, 0)),
        pl.BlockSpec((k_width, tn), lambda i, j: (0, j)),
    ]
    args = [x, g, w]
    scratch = [pltpu.VMEM((tm, k_width), BF16)]
    if rope_tiles or kpe_tiles:
        in_specs += [pl.BlockSpec((tm, LANES), lambda i, j: (i, 0))] * 2
        args += [cos, sin]
    if kpe_tiles:
        in_specs.append(pl.BlockSpec((tm, LANES), lambda i, j: (i, kpe_col)))
        args.append(kpe)
        scratch.append(pltpu.VMEM((tm, LANES), F32))
    kern = functools.partial(_norm_proj_kernel, rope_tiles=rope_tiles, rope_half=rope_half,
                             kpe_tiles=kpe_tiles, n_tiles=n_tiles, tn=tn)
    return pl.pallas_call(
        kern,
        grid=(t // tm, n_tiles),
        in_specs=in_specs,
        out_specs=pl.BlockSpec((tm, tn), lambda i, j: (i, j)),
        out_shape=jax.ShapeDtypeStruct((t, n), out_dtype),
        scratch_shapes=scratch,
        compiler_params=_params("parallel", "arbitrary"),
        name=name,
    )(*args)


def _dilated_kernel(q_ref, kp_ref, kc_ref, vp_ref, vc_ref, o_ref, lse_ref):
    n = pl.program_id(2)
    blk = DSW_BLK
    qi = lax.broadcasted_iota(jnp.int32, (blk, blk), 0)
    kj = lax.broadcasted_iota(jnp.int32, (blk, blk), 1)
    prev_ok = kj >= qi + jnp.where(n > 0, 0, blk)
    cur_ok = kj <= qi
    lane = lax.broadcasted_iota(jnp.int32, (blk, LANES), 1)
    first = lane < HEAD_DIM
    scale = HEAD_DIM ** -0.5
    for pair in range(DSW_GROUP_WIDTH // LANES):
        sl = slice(pair * LANES, (pair + 1) * LANES)
        q2, kp, kc, vp, vc = q_ref[:, sl], kp_ref[:, sl], kc_ref[:, sl], vp_ref[:, sl], vc_ref[:, sl]
        outs, lses = [], []
        for hh in range(2):
            qh = _keep_head(q2, lane, hh)
            sp = jnp.where(prev_ok, _dot_nt(qh, kp) * scale, NEG_BIG)
            sc = jnp.where(cur_ok, _dot_nt(qh, kc) * scale, NEG_BIG)
            m = jnp.maximum(jnp.max(sp, axis=-1, keepdims=True), jnp.max(sc, axis=-1, keepdims=True))
            pp = jnp.exp(sp - m)
            pc = jnp.exp(sc - m)
            l = jnp.sum(pp, axis=-1, keepdims=True) + jnp.sum(pc, axis=-1, keepdims=True)
            inv = 1.0 / l
            outs.append(_dot((pp * inv).astype(BF16), vp) + _dot((pc * inv).astype(BF16), vc))
            lses.append(jnp.broadcast_to(m + jnp.log(l), (blk, LANES)))
        o_ref[:, sl] = jnp.where(first, outs[0], outs[1])
        lse_ref[:, sl] = jnp.where(first, lses[0], lses[1])


def _dilated_group(qkv, bsz, seq, group, dilation):
    lsub = seq // dilation
    nb = lsub // DSW_BLK
    width = qkv.shape[1]
    cols = width // DSW_GROUP_WIDTH
    kcol = DSW_WIDTH // DSW_GROUP_WIDTH
    view = qkv.reshape(bsz, lsub, dilation * width)

    def spec(col0, prev):
        if prev:
            return pl.BlockSpec((None, DSW_BLK, DSW_GROUP_WIDTH),
                                lambda b, r, n: (b, jnp.maximum(n - 1, 0), r * cols + col0 + group))
        return pl.BlockSpec((None, DSW_BLK, DSW_GROUP_WIDTH), lambda b, r, n: (b, n, r * cols + col0 + group))

    out_spec = pl.BlockSpec((None, DSW_BLK, DSW_GROUP_WIDTH), lambda b, r, n: (b, n, r))
    out_sds = jax.ShapeDtypeStruct((bsz, lsub, dilation * DSW_GROUP_WIDTH), F32)
    o, lse = pl.pallas_call(
        _dilated_kernel,
        grid=(bsz, dilation, nb),
        in_specs=[spec(0, False), spec(kcol, True), spec(kcol, False), spec(2 * kcol, True), spec(2 * kcol, False)],
        out_specs=[out_spec, out_spec],
        out_shape=[out_sds, out_sds],
        compiler_params=_params("parallel", "parallel", "arbitrary"),
        name=f"dilated_attn_g{group}",
    )(view, view, view, view, view)
    return o.reshape(bsz * seq, DSW_GROUP_WIDTH), lse.reshape(bsz * seq, DSW_GROUP_WIDTH)


def _dsw_merge_kernel(o0, o1, o2, l0, l1, l2, y_ref):
    a, b, c = l0[...], l1[...], l2[...]
    m = jnp.maximum(jnp.maximum(a, b), c)
    ea, eb, ec = jnp.exp(a - m), jnp.exp(b - m), jnp.exp(c - m)
    inv = 1.0 / (ea + eb + ec)
    y_ref[...] = (ea * inv * o0[...] + eb * inv * o1[...] + ec * inv * o2[...]).astype(y_ref.dtype)


def _dsw_merge(outs, lses, tm):
    t = outs[0].shape[0]
    spec = pl.BlockSpec((tm, DSW_GROUP_WIDTH), lambda i: (i, 0))
    return pl.pallas_call(
        _dsw_merge_kernel,
        grid=(t // tm,),
        in_specs=[spec] * 6,
        out_specs=spec,
        out_shape=jax.ShapeDtypeStruct((t, DSW_GROUP_WIDTH), BF16),
        compiler_params=_params("parallel"),
        name="dilated_merge",
    )(*outs, *lses)


def _mla_attn_kernel(q_ref, k_ref, v_ref, o_ref, *, tq, scale):
    i = pl.program_id(2)
    lane = lax.broadcasted_iota(jnp.int32, (tq, LANES), 1)
    row = lax.broadcasted_iota(jnp.int32, (tq, tq), 0)
    col = lax.broadcasted_iota(jnp.int32, (tq, tq), 1)
    causal = col <= row
    accs = []
    for hh in range(2):
        sl = slice(hh * MLA_SLOT, (hh + 1) * MLA_SLOT)
        q = q_ref[:, sl]

        def block(j, carry, masked, q=q, sl=sl):
            m, l, acc = carry
            start = pl.multiple_of(j * tq, tq)
            s = _dot_nt(q, k_ref[pl.ds(start, tq), sl]) * scale
            if masked:
                s = jnp.where(causal, s, NEG_BIG)
            m_new = jnp.maximum(m, jnp.max(s, axis=-1, keepdims=True))
            alpha = jnp.exp(m - m_new)
            p = jnp.exp(s - m_new)
            l = alpha * l + jnp.sum(p, axis=-1, keepdims=True)
            acc = alpha * acc + _dot(p.astype(BF16), v_ref[pl.ds(start, tq), :])
            return m_new, l, acc

        init = (jnp.full((tq, 1), NEG_BIG, F32), jnp.zeros((tq, 1), F32), jnp.zeros((tq, LANES), F32))
        carry = lax.fori_loop(0, i, functools.partial(block, masked=False), init)
        m, l, acc = block(i, carry, True)
        accs.append(acc * (1.0 / l))
    o_ref[...] = jnp.where(lane < MLA_V, accs[0], accs[1]).astype(o_ref.dtype)


def _mla_attention(q, kv, bsz, seq, tq):
    nq = seq // tq
    pairs = MLA_HEADS // 2
    v_col0 = MLA_HEADS * MLA_SLOT // (2 * MLA_V)
    kern = functools.partial(_mla_attn_kernel, tq=tq, scale=(MLA_NOPE + MLA_ROPE) ** -0.5)
    return pl.pallas_call(
        kern,
        grid=(bsz, pairs, nq),
        in_specs=[
            pl.BlockSpec((tq, 2 * MLA_SLOT), lambda b, p, i: (b * nq + i, p)),
            pl.BlockSpec((seq, 2 * MLA_SLOT), lambda b, p, i: (b, p)),
            pl.BlockSpec((seq, 2 * MLA_V), lambda b, p, i: (b, v_col0 + p)),
        ],
        out_specs=pl.BlockSpec((tq, 2 * MLA_V), lambda b, p, i: (b * nq + i, p)),
        out_shape=jax.ShapeDtypeStruct((bsz * seq, MLA_HEADS * MLA_V), BF16),
        compiler_params=_params("parallel", "parallel", "arbitrary"),
        name="mla_attention",
    )(q, kv, kv)


def _sb_attn_kernel(q_ref, k_ref, v_ref, o_ref, *, tq, tk):
    i = pl.program_id(2)
    q0 = i * tq
    nblk = (i + 1) * (tq // tk)
    scale = HEAD_DIM ** -0.5
    lane = lax.broadcasted_iota(jnp.int32, (tq, LANES), 1)
    qpos = q0 + lax.broadcasted_iota(jnp.int32, (tq, tk), 0)
    kofs = lax.broadcasted_iota(jnp.int32, (tq, tk), 1)
    later = jnp.where(lax.broadcasted_iota(jnp.int32, (tk, tk), 0) > lax.broadcasted_iota(jnp.int32, (tk, tk), 1),
                      1.0, 0.0).astype(BF16)
    q2 = q_ref[...]
    accs = []
    for hh in range(2):
        qh = _keep_head(q2, lane, hh)

        def cond(c):
            jj, run, _ = c
            return (jj < nblk) & (jnp.max(run) >= -SB_LOG_UNDERFLOW)

        def body(c, qh=qh):
            jj, run, acc = c
            start = pl.multiple_of((nblk - 1 - jj) * tk, tk)
            z = _dot_nt(qh, k_ref[pl.ds(start, tk), :]) * scale
            strict = (start + kofs) < qpos
            sp = jnp.log1p(jnp.exp(-jnp.abs(z)))
            log_beta = jnp.minimum(z, 0.0) - sp
            log_stay = jnp.where(strict, -jnp.maximum(z, 0.0) - sp, 0.0)
            hi = log_stay.astype(BF16)
            lo = (log_stay - hi.astype(F32)).astype(BF16)
            after = _dot(hi, later) + _dot(lo, later) + run
            w = jnp.where(strict, jnp.exp(log_beta + after), 0.0)
            acc = acc + _dot(w.astype(BF16), v_ref[pl.ds(start, tk), :])
            run = run + jnp.sum(log_stay, axis=-1, keepdims=True)
            return jj + 1, run, acc

        init = (jnp.int32(0), jnp.zeros((tq, 1), F32), jnp.zeros((tq, LANES), F32))
        accs.append(lax.while_loop(cond, body, init)[2])
    o_ref[...] = jnp.where(lane < HEAD_DIM, accs[0], accs[1]).astype(o_ref.dtype)


def _sb_attention(qkv, bsz, seq, tq, tk):
    nq = seq // tq
    pairs = SB_WIDTH // LANES
    kern = functools.partial(_sb_attn_kernel, tq=tq, tk=tk)
    return pl.pallas_call(
        kern,
        grid=(bsz, pairs, nq),
        in_specs=[
            pl.BlockSpec((tq, LANES), lambda b, p, i: (b * nq + i, p)),
            pl.BlockSpec((seq, LANES), lambda b, p, i: (b, pairs + p)),
            pl.BlockSpec((seq, LANES), lambda b, p, i: (b, 2 * pairs + p)),
        ],
        out_specs=pl.BlockSpec((tq, LANES), lambda b, p, i: (b * nq + i, p)),
        out_shape=jax.ShapeDtypeStruct((bsz * seq, SB_WIDTH), BF16),
        compiler_params=_params("parallel", "parallel", "arbitrary"),
        name="stickbreak_attention",
    )(qkv, qkv, qkv)


def _ssd_kernel(z_ref, xs_ref, bc_ref, dt_ref, cwx_ref, cbx_ref, cwb_ref, cbb_ref, dtb_ref, alog_ref, dskip_ref,
                gn_ref, y_ref, xbuf, bbuf, state):
    c = pl.program_id(1)
    cl = SSM_CHUNK
    halo = SUBLANES

    @pl.when(c == 0)
    def _():
        xbuf[0:halo, :] = jnp.zeros((halo, SSM_INNER), F32)
        bbuf[0:halo, :] = jnp.zeros((halo, SSM_BC_WIDTH), F32)
        state[...] = jnp.zeros_like(state)

    def conv_silu(buf, raw_ref, w_ref, b_ref):
        buf[halo:halo + cl, :] = raw_ref[...]
        y = b_ref[...] + w_ref[SSM_CONV - 1:SSM_CONV, :] * raw_ref[...]
        for j in range(SSM_CONV - 1):
            back = SSM_CONV - 1 - j
            y = y + w_ref[j:j + 1, :] * buf[halo - back:halo - back + cl, :]
        buf[0:halo, :] = buf[cl:cl + halo, :]
        return _silu(y)

    xs = conv_silu(xbuf, xs_ref, cwx_ref, cbx_ref)
    bc = conv_silu(bbuf, bc_ref, cwb_ref, cbb_ref)

    dt = _softplus(dt_ref[...] + dtb_ref[...])
    da = dt * (-jnp.exp(alog_ref[...]))
    row = lax.broadcasted_iota(jnp.int32, (cl, cl), 0)
    col = lax.broadcasted_iota(jnp.int32, (cl, cl), 1)
    causal = col <= row
    tri = jnp.where(causal, 1.0, 0.0).astype(BF16)
    d_hi, d_mid, d_lo = _split3(da)
    cs = _dot(tri, d_hi) + _dot(tri, d_mid) + _dot(tri, d_lo)
    cs_t = cs.T
    ecs = jnp.exp(cs)
    dec_end = jnp.exp(cs[cl - 1:cl, :] - cs)

    head_of_lane = jnp.right_shift(lax.broadcasted_iota(jnp.int32, (LANES, SSM_INNER), 1), 6)
    expand = jnp.where(lax.broadcasted_iota(jnp.int32, (LANES, SSM_INNER), 0) == head_of_lane, 1.0, 0.0).astype(BF16)

    def per_head_lanes(v):
        a, b, c3 = _split3(v)
        return _dot(a, expand) + _dot(b, expand) + _dot(c3, expand)

    dt_e = per_head_lanes(dt)
    ecs_e = per_head_lanes(ecs)
    dec_end_e = per_head_lanes(dec_end)

    xdt = xs * dt_e
    xdt_b = xdt.astype(BF16)
    xw_b = (xdt * dec_end_e).astype(BF16)
    lane = lax.broadcasted_iota(jnp.int32, (cl, LANES), 1)
    first = lane < SSM_HEADDIM
    gs = SSM_STATE
    heads_per_group = SSM_HEADS // SSM_GROUPS
    for g in range(SSM_GROUPS):
        bg = bc[:, g * gs:(g + 1) * gs]
        cg_b = bc[:, (SSM_GROUPS + g) * gs:(SSM_GROUPS + g + 1) * gs].astype(BF16)
        cb = _dot_nt(cg_b, bg.astype(BF16))
        cols = slice(g * SSM_GROUP_WIDTH, (g + 1) * SSM_GROUP_WIDTH)
        prev = state[:, cols]
        y_off = _dot(cg_b, prev.astype(BF16)) * ecs_e[:, cols]
        state[:, cols] = prev * ecs_e[cl - 1:cl, cols] + _dot(bg.T.astype(BF16), xw_b[:, cols])
        for pr in range(heads_per_group // 2):
            pcols = slice(g * SSM_GROUP_WIDTH + pr * LANES, g * SSM_GROUP_WIDTH + (pr + 1) * LANES)
            x_pair = xdt_b[:, pcols]
            ys = []
            for hh in range(2):
                h = g * heads_per_group + 2 * pr + hh
                seg = jnp.where(causal, cs[:, h:h + 1] - cs_t[h:h + 1, :], NEG_BIG)
                ys.append(_dot((cb * jnp.exp(seg)).astype(BF16), x_pair))
            y_diag = jnp.where(first, ys[0], ys[1])
            y_pair = y_diag + y_off[:, pr * LANES:(pr + 1) * LANES] + xs[:, pcols] * dskip_ref[:, pcols]
            y_ref[:, pcols] = y_pair * _silu(z_ref[:, pcols])
    for g in range(SSM_GROUPS):
        cols = slice(g * SSM_GROUP_WIDTH, (g + 1) * SSM_GROUP_WIDTH)
        y_ref[:, cols] = _rms(y_ref[:, cols], gn_ref[:, cols])


def _ssd(proj, bsz, seq, cwx, cbx, cwb, cbb, dt_bias, a_log, d_skip, gnorm):
    cl = SSM_CHUNK
    nc = seq // cl

    def rows(width, colblk):
        return pl.BlockSpec((cl, width), lambda b, c: (b * nc + c, colblk))

    def const(r, width):
        return pl.BlockSpec((r, width), lambda b, c: (0, 0))

    return pl.pallas_call(
        _ssd_kernel,
        grid=(bsz, nc),
        in_specs=[
            rows(SSM_INNER, 0),
            rows(SSM_INNER, 1),
            rows(SSM_BC_WIDTH, 2 * SSM_INNER // SSM_BC_WIDTH),
            rows(LANES, (2 * SSM_INNER + SSM_BC_WIDTH) // LANES),
            const(SSM_CONV, SSM_INNER), const(1, SSM_INNER), const(SSM_CONV, SSM_BC_WIDTH), const(1, SSM_BC_WIDTH),
            const(1, LANES), const(1, LANES), const(1, SSM_INNER), const(1, SSM_INNER),
        ],
        out_specs=pl.BlockSpec((cl, SSM_INNER), lambda b, c: (b * nc + c, 0)),
        out_shape=jax.ShapeDtypeStruct((bsz * seq, SSM_INNER), F32),
        scratch_shapes=[
            pltpu.VMEM((cl + SUBLANES, SSM_INNER), F32),
            pltpu.VMEM((cl + SUBLANES, SSM_BC_WIDTH), F32),
            pltpu.VMEM((SSM_STATE, SSM_INNER), F32),
        ],
        compiler_params=_params("parallel", "arbitrary"),
        name="ssd_scan",
    )(proj, proj, proj, proj, cwx, cbx, cwb, cbb, dt_bias, a_log, d_skip, gnorm)


def _out_proj_kernel(x_ref, a_ref, b_ref, wa_ref, wb_ref, o_ref):
    y = _dot(a_ref[...].astype(BF16), wa_ref[...]) + _dot(b_ref[...].astype(BF16), wb_ref[...])
    o_ref[...] = x_ref[...] + y


def _out_proj(x, a, b, wa, wb, tm):
    t, d = x.shape
    ka, kb = a.shape[1], b.shape[1]
    return pl.pallas_call(
        _out_proj_kernel,
        grid=(t // tm,),
        in_specs=[
            pl.BlockSpec((tm, d), lambda i: (i, 0)),
            pl.BlockSpec((tm, ka), lambda i: (i, 0)),
            pl.BlockSpec((tm, kb), lambda i: (i, 0)),
            pl.BlockSpec((ka, d), lambda i: (0, 0)),
            pl.BlockSpec((kb, d), lambda i: (0, 0)),
        ],
        out_specs=pl.BlockSpec((tm, d), lambda i: (i, 0)),
        out_shape=jax.ShapeDtypeStruct((t, d), F32),
        compiler_params=_params("parallel"),
        name="out_proj",
    )(x, a, b, wa, wb)


def _ffn_kernel(*refs, tm, tiles_per_seq, final_norm):
    x_ref, g_ref, wg_ref, wu_ref, cw_ref, cb_ref, wd_ref = refs[:7]
    pos = 7
    fn_ref = None
    if final_norm:
        fn_ref = refs[pos]
        pos += 1
    o_ref, hn_ref, acc_ref, gbuf, carry = refs[pos:pos + 5]
    i, j = pl.program_id(0), pl.program_id(1)
    halo = SUBLANES

    @pl.when(j == 0)
    def _():
        hn_ref[...] = _rms(x_ref[...], g_ref[...]).astype(BF16)
        acc_ref[...] = jnp.zeros_like(acc_ref)

    hn = hn_ref[...]
    gate = _dot(hn, wg_ref[...])
    up = _dot(hn, wu_ref[...])
    gbuf[halo:halo + tm, :] = gate

    @pl.when(i % tiles_per_seq == 0)
    def _():
        gbuf[0:halo, :] = jnp.zeros((halo, gbuf.shape[1]), F32)

    @pl.when(i % tiles_per_seq != 0)
    def _():
        gbuf[0:halo, :] = carry[j]

    conv = cb_ref[...] + cw_ref[FFN_CONV - 1:FFN_CONV, :] * gate
    for t in range(FFN_CONV - 1):
        back = FFN_CONV - 1 - t
        conv = conv + cw_ref[t:t + 1, :] * gbuf[halo - back:halo - back + tm, :]
    carry[j] = gbuf[tm:tm + halo, :]
    act = (_silu(conv) * up).astype(BF16)
    acc_ref[...] += _dot(act, wd_ref[...])

    @pl.when(j == pl.num_programs(1) - 1)
    def _():
        y = x_ref[...] + acc_ref[...]
        if final_norm:
            y = _rms(y, fn_ref[...])
        o_ref[...] = y


def _conv_ffn(x, g, wg, wu, cw, cb, wd, seq, *, tm, tf, final_gain=None):
    t, d = x.shape
    f = wg.shape[1]
    nf = f // tf
    final_norm = final_gain is not None
    in_specs = [
        pl.BlockSpec((tm, d), lambda i, j: (i, 0)),
        pl.BlockSpec((1, d), lambda i, j: (0, 0)),
        pl.BlockSpec((d, tf), lambda i, j: (0, j)),
        pl.BlockSpec((d, tf), lambda i, j: (0, j)),
        pl.BlockSpec((FFN_CONV, tf), lambda i, j: (0, j)),
        pl.BlockSpec((1, tf), lambda i, j: (0, j)),
        pl.BlockSpec((tf, d), lambda i, j: (j, 0)),
    ]
    args = [x, g, wg, wu, cw, cb, wd]
    if final_norm:
        in_specs.append(pl.BlockSpec((1, d), lambda i, j: (0, 0)))
        args.append(final_gain)
    kern = functools.partial(_ffn_kernel, tm=tm, tiles_per_seq=seq // tm, final_norm=final_norm)
    return pl.pallas_call(
        kern,
        grid=(t // tm, nf),
        in_specs=in_specs,
        out_specs=pl.BlockSpec((tm, d), lambda i, j: (i, 0)),
        out_shape=jax.ShapeDtypeStruct((t, d), F32),
        scratch_shapes=[
            pltpu.VMEM((tm, d), BF16),
            pltpu.VMEM((tm, d), F32),
            pltpu.VMEM((tm + SUBLANES, tf), F32),
            pltpu.VMEM((nf, SUBLANES, tf), F32),
        ],
        compiler_params=_params("arbitrary", "arbitrary"),
        name="conv_ffn",
    )(*args)


def _row(v):
    return v.reshape(1, -1).astype(F32)


def _pad_cols(w, n):
    return jnp.pad(w, ((0, 0), (0, n - w.shape[1])))


def _rope_lane_constants():
    lane = jnp.arange(LANES)
    half_a = HEAD_DIM // 2
    inv_a = 1.0 / (ROPE_THETA ** (jnp.arange(half_a, dtype=F32) * (2.0 / HEAD_DIM)))
    freq_a = inv_a[lane % half_a]
    sign_a = jnp.where((lane % HEAD_DIM) < half_a, -1.0, 1.0).astype(F32)
    half_b = MLA_ROPE // 2
    inv_b = 1.0 / (ROPE_THETA ** (jnp.arange(half_b, dtype=F32) * (2.0 / MLA_ROPE)))
    in_rope = (lane >= MLA_NOPE) & (lane < MLA_NOPE + MLA_ROPE)
    freq_b = jnp.where(in_rope, inv_b[(lane - MLA_NOPE) % half_b], 0.0)
    sign_b = jnp.where(in_rope, jnp.where((lane - MLA_NOPE) < half_b, -1.0, 1.0), 0.0).astype(F32)
    return _row(freq_a), _row(sign_a), _row(freq_b), _row(sign_b)


def _mla_slot_weights(w_uq, w_ukv):
    rq = w_uq.shape[0]
    wq = w_uq.reshape(rq, MLA_HEADS, MLA_NOPE + MLA_ROPE)
    wq = jnp.pad(wq, ((0, 0), (0, 0), (0, MLA_SLOT - MLA_NOPE - MLA_ROPE))).reshape(rq, MLA_HEADS * MLA_SLOT)
    rk = w_ukv.shape[0]
    wkv = w_ukv.reshape(rk, MLA_HEADS, MLA_NOPE + MLA_V)
    wk = jnp.pad(wkv[:, :, :MLA_NOPE], ((0, 0), (0, 0), (0, MLA_SLOT - MLA_NOPE))).reshape(rk, MLA_HEADS * MLA_SLOT)
    wv = wkv[:, :, MLA_NOPE:].reshape(rk, MLA_HEADS * MLA_V)
    return wq.astype(BF16), jnp.concatenate([wk, wv], axis=1).astype(BF16)


def kernel(x, positions,
           l0_norm_mix, l0_w_in, l0_mla_q_norm, l0_mla_w_uq, l0_mla_kv_norm, l0_mla_w_ukv, l0_w_out,
           l0_norm_ffn, l0_ffn_w_gate, l0_ffn_w_up, l0_ffn_conv_w, l0_ffn_conv_b, l0_ffn_w_down,
           l1_norm_mix, l1_w_in, l1_ssm_conv_w, l1_ssm_conv_b, l1_ssm_dt_bias, l1_ssm_a_log, l1_ssm_d,
           l1_ssm_norm, l1_w_out,
           l1_norm_ffn, l1_ffn_w_gate, l1_ffn_w_up, l1_ffn_conv_w, l1_ffn_conv_b, l1_ffn_w_down,
           final_norm):
    bsz, seq, d = x.shape
    t = bsz * seq
    xf = x.reshape(t, d)
    tm = 512

    pos_lanes = jnp.broadcast_to(positions.reshape(t, 1).astype(F32), (t, LANES))
    freq_a, sign_a, freq_b, sign_b = _rope_lane_constants()
    cos_a, sin_a = _rope_tables(pos_lanes, freq_a, sign_a, tm)
    cos_b, sin_b = _rope_tables(pos_lanes, freq_b, sign_b, tm)

    nd = DSW_WIDTH
    w_qkv = l0_w_in[:, :3 * nd].astype(BF16)
    w_cq = l0_w_in[:, 3 * nd:3 * nd + MLA_Q_RANK]
    w_ckv = l0_w_in[:, 3 * nd + MLA_Q_RANK:3 * nd + MLA_Q_RANK + MLA_KV_RANK]
    w_kpe = l0_w_in[:, 3 * nd + MLA_Q_RANK + MLA_KV_RANK:]
    w_kpe_slot = jnp.pad(w_kpe, ((0, 0), (MLA_NOPE, MLA_SLOT - MLA_NOPE - MLA_ROPE)))
    w_lat = jnp.concatenate([w_cq, w_ckv, w_kpe_slot], axis=1).astype(BF16)
    g0 = _row(l0_norm_mix)

    qkv = _norm_proj(xf, 0, d, g0, w_qkv, BF16, tm=tm, tn=DSW_GROUP_WIDTH, cos=cos_a, sin=sin_a,
                     rope_tiles=2 * nd // DSW_GROUP_WIDTH, rope_half=HEAD_DIM // 2, name="l0_qkv_proj")
    lat = _norm_proj(xf, 0, d, g0, w_lat, F32, tm=tm, tn=2 * LANES, name="l0_latent_proj")

    outs, lses = [], []
    for gi, (_, dilation) in enumerate(DSW_GROUPS):
        o, lse = _dilated_group(qkv, bsz, seq, gi, dilation)
        outs.append(o)
        lses.append(lse)
    y_a = _dsw_merge(outs, lses, tm)

    wq_slot, wkv_slot = _mla_slot_weights(l0_mla_w_uq, l0_mla_w_ukv)
    q_mla = _norm_proj(lat, 0, MLA_Q_RANK, _row(l0_mla_q_norm), wq_slot, BF16, tm=tm, tn=2 * MLA_SLOT,
                       cos=cos_b, sin=sin_b, rope_tiles=MLA_HEADS // 2, rope_half=MLA_ROPE // 2, name="mla_q_proj")
    kv_mla = _norm_proj(lat, MLA_Q_RANK // MLA_KV_RANK, MLA_KV_RANK, _row(l0_mla_kv_norm), wkv_slot, BF16,
                        tm=tm, tn=2 * MLA_SLOT, cos=cos_b, sin=sin_b, rope_half=MLA_ROPE // 2,
                        kpe=lat, kpe_col=(MLA_Q_RANK + MLA_KV_RANK) // LANES, kpe_tiles=MLA_HEADS // 2,
                        name="mla_kv_proj")
    y_b = _mla_attention(q_mla, kv_mla, bsz, seq, tq=512)

    w_out0 = l0_w_out.astype(BF16)
    x1 = _out_proj(xf, y_a, y_b, w_out0[:DSW_GROUP_WIDTH], w_out0[DSW_GROUP_WIDTH:], tm)

    x2 = _conv_ffn(x1, _row(l0_norm_ffn), l0_ffn_w_gate.astype(BF16), l0_ffn_w_up.astype(BF16),
                   l0_ffn_conv_w, _row(l0_ffn_conv_b), l0_ffn_w_down.astype(BF16), seq, tm=tm, tf=FFN_DIM // 2)

    o_dt = 2 * SSM_INNER + SSM_BC_WIDTH
    w_ssm = _pad_cols(l1_w_in[:, :o_dt + SSM_HEADS], o_dt + SSM_DT_PAD).astype(BF16)
    w_sb = l1_w_in[:, o_dt + SSM_HEADS:].astype(BF16)
    g1 = _row(l1_norm_mix)
    proj = _norm_proj(x2, 0, d, g1, w_ssm, F32, tm=tm, tn=3 * LANES, name="l1_ssm_proj")
    qkv_sb = _norm_proj(x2, 0, d, g1, w_sb, BF16, tm=tm, tn=SB_WIDTH, name="l1_sb_proj")

    cw = l1_ssm_conv_w
    cb = _row(l1_ssm_conv_b)
    y_c = _ssd(proj, bsz, seq, cw[:, :SSM_INNER], cb[:, :SSM_INNER], cw[:, SSM_INNER:], cb[:, SSM_INNER:],
               _pad_cols(_row(l1_ssm_dt_bias), LANES), _pad_cols(_row(l1_ssm_a_log), LANES),
               _row(jnp.repeat(l1_ssm_d, SSM_HEADDIM)), _row(l1_ssm_norm))
    y_d = _sb_attention(qkv_sb, bsz, seq, tq=256, tk=128)

    w_out1 = l1_w_out.astype(BF16)
    x3 = _out_proj(x2, y_c, y_d, w_out1[:SSM_INNER], w_out1[SSM_INNER:], tm)

    out = _conv_ffn(x3, _row(l1_norm_ffn), l1_ffn_w_gate.astype(BF16), l1_ffn_w_up.astype(BF16),
                    l1_ffn_conv_w, _row(l1_ffn_conv_b), l1_ffn_w_down.astype(BF16), seq, tm=tm, tf=FFN_DIM // 2,
                    final_gain=_row(final_norm))
    return out.reshape(bsz, seq, d)
```

```python
import functools
import math

import jax
import jax.numpy as jnp
from jax import lax
from jax.experimental import pallas as pl
from jax.experimental.pallas import tpu as pltpu

F32 = jnp.float32
BF16 = jnp.bfloat16

LANES = 128
SUBLANES = 8
MXU_WIDTH = 256
VMEM_LIMIT_BYTES = 56 * 1024 * 1024

D_MODEL = 1024
HEAD_DIM = 64
ROPE_THETA = 10000.0
NORM_EPS = 1e-6

DSW_GROUPS = ((128, 1), (512, 4), (2048, 16))
DSW_HEADS_PER_GROUP = 4
DSW_HEADS = DSW_HEADS_PER_GROUP * len(DSW_GROUPS)
DSW_BLK = 128
DSW_WIDTH = DSW_HEADS * HEAD_DIM
DSW_GROUP_WIDTH = DSW_HEADS_PER_GROUP * HEAD_DIM

MLA_HEADS = 12
MLA_Q_RANK = 256
MLA_KV_RANK = 128
MLA_NOPE = 64
MLA_ROPE = 32
MLA_V = 64
MLA_SLOT = 128
MLA_WIDTH = MLA_HEADS * MLA_SLOT
MLA_LAT_WIDTH = MLA_Q_RANK + MLA_KV_RANK + MLA_SLOT
MLA_Q_SCALE = (MLA_NOPE + MLA_ROPE) ** -0.5 * math.log2(math.e)

SSM_INNER = 1024
SSM_HEADDIM = 64
SSM_HEADS = 16
SSM_STATE = 128
SSM_GROUPS = 2
SSM_CONV = 4
SSM_CHUNK = 128
SSM_GROUP_WIDTH = SSM_INNER // SSM_GROUPS
SSM_BC_WIDTH = 2 * SSM_GROUPS * SSM_STATE
SSM_DT_PAD = LANES
SSM_PROJ_WIDTH = 2 * SSM_INNER + SSM_BC_WIDTH + SSM_DT_PAD

SB_HEADS = 8
SB_WIDTH = SB_HEADS * HEAD_DIM
SB_LOG_UNDERFLOW = 104.0

FFN_DIM = 2816
FFN_CONV = 3

NEG_BIG = -1e30


def _params(*sem):
    return pltpu.CompilerParams(dimension_semantics=sem, vmem_limit_bytes=VMEM_LIMIT_BYTES)


def _rms(x, g):
    return x * lax.rsqrt(jnp.mean(x * x, axis=-1, keepdims=True) + NORM_EPS) * g


def _silu(x):
    return x * (1.0 / (1.0 + jnp.exp(-x)))


def _softplus(x):
    return jnp.maximum(x, 0.0) + jnp.log1p(jnp.exp(-jnp.abs(x)))


def _swap_halves(x, half):
    lane = lax.broadcasted_iota(jnp.int32, x.shape, 1)
    up = pltpu.roll(x, LANES - half, 1)
    down = pltpu.roll(x, half, 1)
    return jnp.where((lane & half) == 0, up, down)


def _rope_tile(x, cos, sin, half):
    return x * cos + _swap_halves(x, half) * sin


def _split2(x):
    hi = x.astype(BF16)
    return hi, (x - hi.astype(F32)).astype(BF16)


def _split3(x):
    hi = x.astype(BF16)
    r1 = x - hi.astype(F32)
    mid = r1.astype(BF16)
    lo = (r1 - mid.astype(F32)).astype(BF16)
    return hi, mid, lo


def _dot(a, b):
    return jnp.dot(a, b, preferred_element_type=F32)


def _dot_nt(a, b):
    return lax.dot_general(a, b, (((1,), (1,)), ((), ())), preferred_element_type=F32)


def _keep_head(x2, lane, second):
    mine = (lane >= HEAD_DIM) if second else (lane < HEAD_DIM)
    return jnp.where(mine, x2.astype(F32), 0.0).astype(BF16)


def _resident(shape):
    return pl.BlockSpec(shape, lambda *_: (0,) * len(shape))


def _rope_table_kernel(pos_ref, freq_ref, sign_ref, cos_ref, sin_ref):
    ang = pos_ref[...] * freq_ref[...]
    cos_ref[...] = jnp.cos(ang)
    sin_ref[...] = jnp.sin(ang) * sign_ref[...]


def _rope_tables(pos_lanes, freq, sign, tm):
    t = pos_lanes.shape[0]
    row = pl.BlockSpec((tm, LANES), lambda i: (i, 0))
    const = pl.BlockSpec((1, LANES), lambda i: (0, 0))
    return pl.pallas_call(
        _rope_table_kernel,
        grid=(t // tm,),
        in_specs=[row, const, const],
        out_specs=[row, row],
        out_shape=[jax.ShapeDtypeStruct((t, LANES), F32)] * 2,
        compiler_params=_params("parallel"),
        name="rope_tables",
    )(pos_lanes, freq, sign)


def _l0_in_kernel(x_ref, g_ref, wqkv_ref, wlat_ref, gq_ref, wq_ref, gkv_ref, wkv_ref,
                  cosa_ref, sina_ref, cosb_ref, sinb_ref, qkv_ref, qm_ref, kvm_ref):
    hn = _rms(x_ref[...], g_ref[...]).astype(BF16)
    cos_a, sin_a = cosa_ref[...], sina_ref[...]
    cos_b, sin_b = cosb_ref[...], sinb_ref[...]
    half_a, half_b = HEAD_DIM // 2, MLA_ROPE // 2
    step = MXU_WIDTH
    halves = step // LANES
    rope_cols = 2 * DSW_WIDTH
    for c0 in range(0, 3 * DSW_WIDTH, step):
        acc = _dot(hn, wqkv_ref[:, c0:c0 + step])
        if c0 < rope_cols:
            for h in range(halves):
                sl = slice(h * LANES, (h + 1) * LANES)
                qkv_ref[:, c0 + h * LANES:c0 + (h + 1) * LANES] = _rope_tile(acc[:, sl], cos_a, sin_a, half_a).astype(BF16)
        else:
            qkv_ref[:, c0:c0 + step] = acc.astype(BF16)

    lat = _dot(hn, wlat_ref[...])
    cq = _rms(lat[:, :MLA_Q_RANK], gq_ref[...]).astype(BF16)
    ckv = _rms(lat[:, MLA_Q_RANK:MLA_Q_RANK + MLA_KV_RANK], gkv_ref[...]).astype(BF16)
    kpe = _rope_tile(lat[:, MLA_Q_RANK + MLA_KV_RANK:], cos_b, sin_b, half_b)
    for c0 in range(0, MLA_WIDTH, step):
        acc = _dot(cq, wq_ref[:, c0:c0 + step])
        for h in range(halves):
            sl = slice(h * LANES, (h + 1) * LANES)
            tile = _rope_tile(acc[:, sl], cos_b, sin_b, half_b) * MLA_Q_SCALE
            qm_ref[:, c0 + h * LANES:c0 + (h + 1) * LANES] = tile.astype(BF16)
    lane = lax.broadcasted_iota(jnp.int32, kpe.shape, 1)
    one_hot = jnp.where(lane == MLA_V, 1.0, 0.0)
    for c0 in range(0, 2 * MLA_WIDTH, step):
        acc = _dot(ckv, wkv_ref[:, c0:c0 + step])
        extra = kpe if c0 < MLA_WIDTH else one_hot
        for h in range(halves):
            sl = slice(h * LANES, (h + 1) * LANES)
            kvm_ref[:, c0 + h * LANES:c0 + (h + 1) * LANES] = (acc[:, sl] + extra).astype(BF16)


def _l0_in_proj(x, g, w_qkv, w_lat, gq, wq, gkv, wkv, cos_a, sin_a, cos_b, sin_b, tm):
    t, d = x.shape
    row = lambda w: pl.BlockSpec((tm, w), lambda i: (i, 0))
    return pl.pallas_call(
        _l0_in_kernel,
        grid=(t // tm,),
        in_specs=[row(d), _resident((1, d)), _resident(w_qkv.shape), _resident(w_lat.shape),
                  _resident(gq.shape), _resident(wq.shape), _resident(gkv.shape), _resident(wkv.shape),
                  row(LANES), row(LANES), row(LANES), row(LANES)],
        out_specs=[row(3 * DSW_WIDTH), row(MLA_WIDTH), row(2 * MLA_WIDTH)],
        out_shape=[jax.ShapeDtypeStruct((t, 3 * DSW_WIDTH), BF16),
                   jax.ShapeDtypeStruct((t, MLA_WIDTH), BF16),
                   jax.ShapeDtypeStruct((t, 2 * MLA_WIDTH), BF16)],
        compiler_params=_params("parallel"),
        name="l0_in_proj",
    )(x, g, w_qkv, w_lat, gq, wq, gkv, wkv, cos_a, sin_a, cos_b, sin_b)


def _l1_in_kernel(x_ref, g_ref, wssm_ref, wsb_ref, proj_ref, qkv_ref):
    hn = _rms(x_ref[...], g_ref[...]).astype(BF16)
    step = 3 * LANES
    for c0 in range(0, SSM_PROJ_WIDTH, step):
        proj_ref[:, c0:c0 + step] = _dot(hn, wssm_ref[:, c0:c0 + step])
    step = MXU_WIDTH
    for c0 in range(0, 3 * SB_WIDTH, step):
        qkv_ref[:, c0:c0 + step] = _dot(hn, wsb_ref[:, c0:c0 + step]).astype(BF16)


def _l1_in_proj(x, g, w_ssm, w_sb, tm):
    t, d = x.shape
    row = lambda w: pl.BlockSpec((tm, w), lambda i: (i, 0))
    return pl.pallas_call(
        _l1_in_kernel,
        grid=(t // tm,),
        in_specs=[row(d), _resident((1, d)), _resident(w_ssm.shape), _resident(w_sb.shape)],
        out_specs=[row(SSM_PROJ_WIDTH), row(3 * SB_WIDTH)],
        out_shape=[jax.ShapeDtypeStruct((t, SSM_PROJ_WIDTH), F32),
                   jax.ShapeDtypeStruct((t, 3 * SB_WIDTH), BF16)],
        compiler_params=_params("parallel"),
        name="l1_in_proj",
    )(x, g, w_ssm, w_sb)


def _dilated_kernel(q_ref, kp_ref, kc_ref, vp_ref, vc_ref, o_ref, lse_ref):
    n = pl.program_id(2)
    blk = DSW_BLK
    qi = lax.broadcasted_iota(jnp.int32, (blk, blk), 0)
    kj = lax.broadcasted_iota(jnp.int32, (blk, blk), 1)
    prev_ok = kj >= qi + jnp.where(n > 0, 0, blk)
    cur_ok = kj <= qi
    lane = lax.broadcasted_iota(jnp.int32, (blk, LANES), 1)
    first = lane < HEAD_DIM
    scale = HEAD_DIM ** -0.5
    for pair in range(DSW_GROUP_WIDTH // LANES):
        sl = slice(pair * LANES, (pair + 1) * LANES)
        q2, kp, kc, vp, vc = q_ref[:, sl], kp_ref[:, sl], kc_ref[:, sl], vp_ref[:, sl], vc_ref[:, sl]
        outs, lses = [], []
        for hh in range(2):
            qh = _keep_head(q2, lane, hh)
            sp = jnp.where(prev_ok, _dot_nt(qh, kp) * scale, NEG_BIG)
            sc = jnp.where(cur_ok, _dot_nt(qh, kc) * scale, NEG_BIG)
            m = jnp.maximum(jnp.max(sp, axis=-1, keepdims=True), jnp.max(sc, axis=-1, keepdims=True))
            pp = jnp.exp(sp - m)
            pc = jnp.exp(sc - m)
            l = jnp.sum(pp, axis=-1, keepdims=True) + jnp.sum(pc, axis=-1, keepdims=True)
            inv = 1.0 / l
            outs.append(_dot((pp * inv).astype(BF16), vp) + _dot((pc * inv).astype(BF16), vc))
            lses.append(jnp.broadcast_to(m + jnp.log(l), (blk, LANES)))
        o_ref[:, sl] = jnp.where(first, outs[0], outs[1])
        lse_ref[:, sl] = jnp.where(first, lses[0], lses[1])


def _dilated_group(qkv, bsz, seq, group, dilation):
    lsub = seq // dilation
    nb = lsub // DSW_BLK
    width = qkv.shape[1]
    cols = width // DSW_GROUP_WIDTH
    kcol = DSW_WIDTH // DSW_GROUP_WIDTH
    view = qkv.reshape(bsz, lsub, dilation * width)

    def spec(col0, prev):
        if prev:
            return pl.BlockSpec((None, DSW_BLK, DSW_GROUP_WIDTH),
                                lambda b, r, n: (b, jnp.maximum(n - 1, 0), r * cols + col0 + group))
        return pl.BlockSpec((None, DSW_BLK, DSW_GROUP_WIDTH), lambda b, r, n: (b, n, r * cols + col0 + group))

    out_spec = pl.BlockSpec((None, DSW_BLK, DSW_GROUP_WIDTH), lambda b, r, n: (b, n, r))
    out_sds = jax.ShapeDtypeStruct((bsz, lsub, dilation * DSW_GROUP_WIDTH), F32)
    o, lse = pl.pallas_call(
        _dilated_kernel,
        grid=(bsz, dilation, nb),
        in_specs=[spec(0, False), spec(kcol, True), spec(kcol, False), spec(2 * kcol, True), spec(2 * kcol, False)],
        out_specs=[out_spec, out_spec],
        out_shape=[out_sds, out_sds],
        compiler_params=_params("parallel", "parallel", "arbitrary"),
        name=f"dilated_attn_g{group}",
    )(view, view, view, view, view)
    return o.reshape(bsz * seq, DSW_GROUP_WIDTH), lse.reshape(bsz * seq, DSW_GROUP_WIDTH)


def _dsw_merge_kernel(o0, o1, o2, l0, l1, l2, y_ref):
    a, b, c = l0[...], l1[...], l2[...]
    m = jnp.maximum(jnp.maximum(a, b), c)
    ea, eb, ec = jnp.exp(a - m), jnp.exp(b - m), jnp.exp(c - m)
    inv = 1.0 / (ea + eb + ec)
    y_ref[...] = (ea * inv * o0[...] + eb * inv * o1[...] + ec * inv * o2[...]).astype(y_ref.dtype)


def _dsw_merge(outs, lses, tm):
    t = outs[0].shape[0]
    spec = pl.BlockSpec((tm, DSW_GROUP_WIDTH), lambda i: (i, 0))
    return pl.pallas_call(
        _dsw_merge_kernel,
        grid=(t // tm,),
        in_specs=[spec] * 6,
        out_specs=spec,
        out_shape=jax.ShapeDtypeStruct((t, DSW_GROUP_WIDTH), BF16),
        compiler_params=_params("parallel"),
        name="dilated_merge",
    )(*outs, *lses)


def _mla_attn_kernel(q_ref, k_ref, v_ref, o_ref, *, tq):
    i = pl.program_id(2)
    lane = lax.broadcasted_iota(jnp.int32, (tq, LANES), 1)
    causal = lax.broadcasted_iota(jnp.int32, (tq, tq), 1) <= lax.broadcasted_iota(jnp.int32, (tq, tq), 0)
    slots = [slice(hh * MLA_SLOT, (hh + 1) * MLA_SLOT) for hh in range(2)]
    qs = [q_ref[:, sl] for sl in slots]

    def block(j, carry, masked):
        start = pl.multiple_of(j * tq, tq)
        new = []
        for hh in range(2):
            m, acc = carry[hh]
            s = _dot_nt(qs[hh], k_ref[pl.ds(start, tq), slots[hh]])
            if masked:
                s = jnp.where(causal, s, NEG_BIG)
            m_new = jnp.maximum(m, jnp.max(s, axis=-1, keepdims=True))
            p = jnp.exp2(s - m_new)
            acc = jnp.exp2(m - m_new) * acc + _dot(p.astype(BF16), v_ref[pl.ds(start, tq), slots[hh]])
            new.append((m_new, acc))
        return tuple(new)

    def block_pair(jp, carry):
        return block(2 * jp + 1, block(2 * jp, carry, False), False)

    init = tuple((jnp.full((tq, 1), NEG_BIG, F32), jnp.zeros((tq, LANES), F32)) for _ in range(2))
    n_pairs = lax.shift_right_logical(i, 1)
    carry = lax.fori_loop(0, n_pairs, block_pair, init)
    carry = lax.fori_loop(2 * n_pairs, i, functools.partial(block, masked=False), carry)
    (_, acc0), (_, acc1) = block(i, carry, True)
    o0 = acc0 * (1.0 / acc0[:, MLA_V:MLA_V + 1])
    o1 = acc1 * (1.0 / acc1[:, MLA_V:MLA_V + 1])
    o_ref[...] = jnp.where(lane < MLA_V, o0, pltpu.roll(o1, MLA_V, 1)).astype(o_ref.dtype)


def _mla_attention(q, kv, bsz, seq, tq):
    nq = seq // tq
    pairs = MLA_HEADS // 2
    kern = functools.partial(_mla_attn_kernel, tq=tq)
    return pl.pallas_call(
        kern,
        grid=(bsz, pairs, nq),
        in_specs=[
            pl.BlockSpec((tq, 2 * MLA_SLOT), lambda b, p, i: (b * nq + i, p)),
            pl.BlockSpec((seq, 2 * MLA_SLOT), lambda b, p, i: (b, p)),
            pl.BlockSpec((seq, 2 * MLA_SLOT), lambda b, p, i: (b, pairs + p)),
        ],
        out_specs=pl.BlockSpec((tq, 2 * MLA_V), lambda b, p, i: (b * nq + i, p)),
        out_shape=jax.ShapeDtypeStruct((bsz * seq, MLA_HEADS * MLA_V), BF16),
        compiler_params=_params("parallel", "parallel", "arbitrary"),
        name="mla_attention",
    )(q, kv, kv)


def _sb_attn_kernel(q_ref, k_ref, v_ref, o_ref, *, tq):
    i = pl.program_id(2)
    scale = HEAD_DIM ** -0.5
    lane = lax.broadcasted_iota(jnp.int32, (tq, LANES), 1)
    row = lax.broadcasted_iota(jnp.int32, (tq, tq), 0)
    col = lax.broadcasted_iota(jnp.int32, (tq, tq), 1)
    strict = col < row
    later = jnp.where(row > col, 1.0, 0.0).astype(BF16)
    q2 = q_ref[...]
    qs = [_keep_head(q2, lane, hh) for hh in range(2)]

    def block(blk, runs, accs, masked):
        start = pl.multiple_of(blk * tq, tq)
        kb, vb = k_ref[pl.ds(start, tq), :], v_ref[pl.ds(start, tq), :]
        new_runs, new_accs = [], []
        for hh in range(2):
            z = _dot_nt(qs[hh], kb) * scale
            sp = jnp.log(1.0 + jnp.exp(-jnp.abs(z)))
            log_beta = jnp.minimum(z, 0.0) - sp
            log_stay = -jnp.maximum(z, 0.0) - sp
            if masked:
                log_stay = jnp.where(strict, log_stay, 0.0)
            hi, lo = _split2(log_stay)
            after = _dot(hi, later) + _dot(lo, later) + runs[hh]
            w = jnp.exp(log_beta + after)
            if masked:
                w = jnp.where(strict, w, 0.0)
            new_accs.append(accs[hh] + _dot(w.astype(BF16), vb))
            new_runs.append(runs[hh] + jnp.sum(log_stay, axis=-1, keepdims=True))
        return tuple(new_runs), tuple(new_accs)

    run0 = (jnp.zeros((tq, 1), F32), jnp.zeros((tq, 1), F32))
    acc0 = (jnp.zeros((tq, LANES), F32), jnp.zeros((tq, LANES), F32))

    def diagonal_only():
        return block(i, run0, acc0, True)

    def diagonal_and_previous():
        runs, accs = block(i, run0, acc0, True)
        return block(i - 1, runs, accs, False)

    runs, accs = lax.cond(i > 0, diagonal_and_previous, diagonal_only)

    def cond(c):
        left, runs, _ = c
        alive = jnp.maximum(jnp.max(runs[0]), jnp.max(runs[1])) >= -SB_LOG_UNDERFLOW
        return (left > 0) & alive

    def body(c):
        left, runs, accs = c
        runs, accs = block(left - 1, runs, accs, False)
        return left - 1, runs, accs

    _, _, accs = lax.while_loop(cond, body, (jnp.maximum(i - 1, 0), runs, accs))
    o_ref[...] = jnp.where(lane < HEAD_DIM, accs[0], accs[1]).astype(o_ref.dtype)


def _sb_attention(qkv, bsz, seq, tq):
    nq = seq // tq
    pairs = SB_WIDTH // LANES
    kern = functools.partial(_sb_attn_kernel, tq=tq)
    return pl.pallas_call(
        kern,
        grid=(bsz, pairs, nq),
        in_specs=[
            pl.BlockSpec((tq, LANES), lambda b, p, i: (b * nq + i, p)),
            pl.BlockSpec((seq, LANES), lambda b, p, i: (b, pairs + p)),
            pl.BlockSpec((seq, LANES), lambda b, p, i: (b, 2 * pairs + p)),
        ],
        out_specs=pl.BlockSpec((tq, LANES), lambda b, p, i: (b * nq + i, p)),
        out_shape=jax.ShapeDtypeStruct((bsz * seq, SB_WIDTH), BF16),
        compiler_params=_params("parallel", "parallel", "arbitrary"),
        name="stickbreak_attention",
    )(qkv, qkv, qkv)


def _ssd_kernel(z_ref, xs_ref, bc_ref, dt_ref, cwx_ref, cbx_ref, cwb_ref, cbb_ref, dtb_ref, alog_ref, dskip_ref,
                gn_ref, y_ref, xbuf, bbuf, state):
    c = pl.program_id(1)
    cl = SSM_CHUNK
    halo = SUBLANES

    @pl.when(c == 0)
    def _():
        xbuf[0:halo, :] = jnp.zeros((halo, SSM_INNER), F32)
        bbuf[0:halo, :] = jnp.zeros((halo, SSM_BC_WIDTH), F32)
        state[...] = jnp.zeros_like(state)

    def conv_silu(buf, raw_ref, w_ref, b_ref):
        buf[halo:halo + cl, :] = raw_ref[...]
        y = b_ref[...] + w_ref[SSM_CONV - 1:SSM_CONV, :] * raw_ref[...]
        for j in range(SSM_CONV - 1):
            back = SSM_CONV - 1 - j
            y = y + w_ref[j:j + 1, :] * buf[halo - back:halo - back + cl, :]
        buf[0:halo, :] = buf[cl:cl + halo, :]
        return _silu(y)

    xs = conv_silu(xbuf, xs_ref, cwx_ref, cbx_ref)
    bc = conv_silu(bbuf, bc_ref, cwb_ref, cbb_ref)

    dt = _softplus(dt_ref[...] + dtb_ref[...])
    da = dt * (-jnp.exp(alog_ref[...]))
    row = lax.broadcasted_iota(jnp.int32, (cl, cl), 0)
    col = lax.broadcasted_iota(jnp.int32, (cl, cl), 1)
    causal = col <= row
    tri = jnp.where(causal, 1.0, 0.0).astype(BF16)
    d_hi, d_mid, d_lo = _split3(da)
    cs = _dot(tri, d_hi) + _dot(tri, d_mid) + _dot(tri, d_lo)
    cs_t = cs.T
    ecs = jnp.exp(cs)
    dec_end = jnp.exp(cs[cl - 1:cl, :] - cs)

    head_of_lane = jnp.right_shift(lax.broadcasted_iota(jnp.int32, (LANES, SSM_INNER), 1), 6)
    expand = jnp.where(lax.broadcasted_iota(jnp.int32, (LANES, SSM_INNER), 0) == head_of_lane, 1.0, 0.0).astype(BF16)

    def per_head_lanes(v):
        a, b, c3 = _split3(v)
        return _dot(a, expand) + _dot(b, expand) + _dot(c3, expand)

    dt_e = per_head_lanes(dt)
    ecs_e = per_head_lanes(ecs)
    dec_end_e = per_head_lanes(dec_end)

    xdt = xs * dt_e
    xdt_b = xdt.astype(BF16)
    xw_b = (xdt * dec_end_e).astype(BF16)
    lane = lax.broadcasted_iota(jnp.int32, (cl, LANES), 1)
    first = lane < SSM_HEADDIM
    gs = SSM_STATE
    heads_per_group = SSM_HEADS // SSM_GROUPS
    for g in range(SSM_GROUPS):
        bg = bc[:, g * gs:(g + 1) * gs]
        cg_b = bc[:, (SSM_GROUPS + g) * gs:(SSM_GROUPS + g + 1) * gs].astype(BF16)
        cb = _dot_nt(cg_b, bg.astype(BF16))
        cols = slice(g * SSM_GROUP_WIDTH, (g + 1) * SSM_GROUP_WIDTH)
        prev = state[:, cols]
        y_off = _dot(cg_b, prev.astype(BF16)) * ecs_e[:, cols]
        state[:, cols] = prev * ecs_e[cl - 1:cl, cols] + _dot(bg.T.astype(BF16), xw_b[:, cols])
        for pr in range(heads_per_group // 2):
            pcols = slice(g * SSM_GROUP_WIDTH + pr * LANES, g * SSM_GROUP_WIDTH + (pr + 1) * LANES)
            x_pair = xdt_b[:, pcols]
            ys = []
            for hh in range(2):
                h = g * heads_per_group + 2 * pr + hh
                seg = jnp.where(causal, cs[:, h:h + 1] - cs_t[h:h + 1, :], NEG_BIG)
                ys.append(_dot((cb * jnp.exp(seg)).astype(BF16), x_pair))
            y_diag = jnp.where(first, ys[0], ys[1])
            y_pair = y_diag + y_off[:, pr * LANES:(pr + 1) * LANES] + xs[:, pcols] * dskip_ref[:, pcols]
            y_ref[:, pcols] = y_pair * _silu(z_ref[:, pcols])
    for g in range(SSM_GROUPS):
        cols = slice(g * SSM_GROUP_WIDTH, (g + 1) * SSM_GROUP_WIDTH)
        y_ref[:, cols] = _rms(y_ref[:, cols], gn_ref[:, cols])


def _ssd(proj, bsz, seq, cwx, cbx, cwb, cbb, dt_bias, a_log, d_skip, gnorm):
    cl = SSM_CHUNK
    nc = seq // cl

    def rows(width, colblk):
        return pl.BlockSpec((cl, width), lambda b, c: (b * nc + c, colblk))

    def const(r, width):
        return pl.BlockSpec((r, width), lambda b, c: (0, 0))

    return pl.pallas_call(
        _ssd_kernel,
        grid=(bsz, nc),
        in_specs=[
            rows(SSM_INNER, 0),
            rows(SSM_INNER, 1),
            rows(SSM_BC_WIDTH, 2 * SSM_INNER // SSM_BC_WIDTH),
            rows(LANES, (2 * SSM_INNER + SSM_BC_WIDTH) // LANES),
            const(SSM_CONV, SSM_INNER), const(1, SSM_INNER), const(SSM_CONV, SSM_BC_WIDTH), const(1, SSM_BC_WIDTH),
            const(1, LANES), const(1, LANES), const(1, SSM_INNER), const(1, SSM_INNER),
        ],
        out_specs=pl.BlockSpec((cl, SSM_INNER), lambda b, c: (b * nc + c, 0)),
        out_shape=jax.ShapeDtypeStruct((bsz * seq, SSM_INNER), F32),
        scratch_shapes=[
            pltpu.VMEM((cl + SUBLANES, SSM_INNER), F32),
            pltpu.VMEM((cl + SUBLANES, SSM_BC_WIDTH), F32),
            pltpu.VMEM((SSM_STATE, SSM_INNER), F32),
        ],
        compiler_params=_params("parallel", "arbitrary"),
        name="ssd_scan",
    )(proj, proj, proj, proj, cwx, cbx, cwb, cbb, dt_bias, a_log, d_skip, gnorm)


def _out_proj_kernel(x_ref, a_ref, b_ref, wa_ref, wb_ref, o_ref):
    y = _dot(a_ref[...].astype(BF16), wa_ref[...]) + _dot(b_ref[...].astype(BF16), wb_ref[...])
    o_ref[...] = x_ref[...] + y


def _out_proj(x, a, b, wa, wb, tm):
    t, d = x.shape
    ka, kb = a.shape[1], b.shape[1]
    return pl.pallas_call(
        _out_proj_kernel,
        grid=(t // tm,),
        in_specs=[
            pl.BlockSpec((tm, d), lambda i: (i, 0)),
            pl.BlockSpec((tm, ka), lambda i: (i, 0)),
            pl.BlockSpec((tm, kb), lambda i: (i, 0)),
            _resident((ka, d)),
            _resident((kb, d)),
        ],
        out_specs=pl.BlockSpec((tm, d), lambda i: (i, 0)),
        out_shape=jax.ShapeDtypeStruct((t, d), F32),
        compiler_params=_params("parallel"),
        name="out_proj",
    )(x, a, b, wa, wb)


def _ffn_kernel(*refs, tm, tiles_per_seq, final_norm):
    x_ref, g_ref, wg_ref, wu_ref, cw_ref, cb_ref, wd_ref = refs[:7]
    pos = 7
    fn_ref = None
    if final_norm:
        fn_ref = refs[pos]
        pos += 1
    o_ref, hn_ref, acc_ref, gbuf, carry = refs[pos:pos + 5]
    i, j = pl.program_id(0), pl.program_id(1)
    halo = SUBLANES

    @pl.when(j == 0)
    def _():
        hn_ref[...] = _rms(x_ref[...], g_ref[...]).astype(BF16)
        acc_ref[...] = jnp.zeros_like(acc_ref)

    hn = hn_ref[...]
    gate = _dot(hn, wg_ref[...])
    up = _dot(hn, wu_ref[...])
    gbuf[halo:halo + tm, :] = gate

    @pl.when(i % tiles_per_seq == 0)
    def _():
        gbuf[0:halo, :] = jnp.zeros((halo, gbuf.shape[1]), F32)

    @pl.when(i % tiles_per_seq != 0)
    def _():
        gbuf[0:halo, :] = carry[j]

    conv = cb_ref[...] + cw_ref[FFN_CONV - 1:FFN_CONV, :] * gate
    for t in range(FFN_CONV - 1):
        back = FFN_CONV - 1 - t
        conv = conv + cw_ref[t:t + 1, :] * gbuf[halo - back:halo - back + tm, :]
    carry[j] = gbuf[tm:tm + halo, :]
    act = (_silu(conv) * up).astype(BF16)
    acc_ref[...] += _dot(act, wd_ref[...])

    @pl.when(j == pl.num_programs(1) - 1)
    def _():
        y = x_ref[...] + acc_ref[...]
        if final_norm:
            y = _rms(y, fn_ref[...])
        o_ref[...] = y


def _conv_ffn(x, g, wg, wu, cw, cb, wd, seq, *, tm, tf, final_gain=None):
    t, d = x.shape
    f = wg.shape[1]
    nf = f // tf
    final_norm = final_gain is not None
    in_specs = [
        pl.BlockSpec((tm, d), lambda i, j: (i, 0)),
        pl.BlockSpec((1, d), lambda i, j: (0, 0)),
        pl.BlockSpec((d, tf), lambda i, j: (0, j)),
        pl.BlockSpec((d, tf), lambda i, j: (0, j)),
        pl.BlockSpec((FFN_CONV, tf), lambda i, j: (0, j)),
        pl.BlockSpec((1, tf), lambda i, j: (0, j)),
        pl.BlockSpec((tf, d), lambda i, j: (j, 0)),
    ]
    args = [x, g, wg, wu, cw, cb, wd]
    if final_norm:
        in_specs.append(pl.BlockSpec((1, d), lambda i, j: (0, 0)))
        args.append(final_gain)
    kern = functools.partial(_ffn_kernel, tm=tm, tiles_per_seq=seq // tm, final_norm=final_norm)
    return pl.pallas_call(
        kern,
        grid=(t // tm, nf),
        in_specs=in_specs,
        out_specs=pl.BlockSpec((tm, d), lambda i, j: (i, 0)),
        out_shape=jax.ShapeDtypeStruct((t, d), F32),
        scratch_shapes=[
            pltpu.VMEM((tm, d), BF16),
            pltpu.VMEM((tm, d), F32),
            pltpu.VMEM((tm + SUBLANES, tf), F32),
            pltpu.VMEM((nf, SUBLANES, tf), F32),
        ],
        compiler_params=_params("arbitrary", "arbitrary"),
        name="conv_ffn",
    )(*args)


def _row(v):
    return v.reshape(1, -1).astype(F32)


def _pad_cols(w, n):
    return jnp.pad(w, ((0, 0), (0, n - w.shape[1])))


def _rope_lane_constants():
    lane = jnp.arange(LANES)
    half_a = HEAD_DIM // 2
    inv_a = 1.0 / (ROPE_THETA ** (jnp.arange(half_a, dtype=F32) * (2.0 / HEAD_DIM)))
    freq_a = inv_a[lane % half_a]
    sign_a = jnp.where((lane % HEAD_DIM) < half_a, -1.0, 1.0).astype(F32)
    half_b = MLA_ROPE // 2
    inv_b = 1.0 / (ROPE_THETA ** (jnp.arange(half_b, dtype=F32) * (2.0 / MLA_ROPE)))
    in_rope = (lane >= MLA_NOPE) & (lane < MLA_NOPE + MLA_ROPE)
    freq_b = jnp.where(in_rope, inv_b[(lane - MLA_NOPE) % half_b], 0.0)
    sign_b = jnp.where(in_rope, jnp.where((lane - MLA_NOPE) < half_b, -1.0, 1.0), 0.0).astype(F32)
    return _row(freq_a), _row(sign_a), _row(freq_b), _row(sign_b)


def _mla_slot_weights(w_uq, w_ukv):
    rq = w_uq.shape[0]
    wq = w_uq.reshape(rq, MLA_HEADS, MLA_NOPE + MLA_ROPE)
    wq = jnp.pad(wq, ((0, 0), (0, 0), (0, MLA_SLOT - MLA_NOPE - MLA_ROPE))).reshape(rq, MLA_WIDTH)
    rk = w_ukv.shape[0]
    wkv = w_ukv.reshape(rk, MLA_HEADS, MLA_NOPE + MLA_V)
    wk = jnp.pad(wkv[:, :, :MLA_NOPE], ((0, 0), (0, 0), (0, MLA_SLOT - MLA_NOPE))).reshape(rk, MLA_WIDTH)
    wv = jnp.pad(wkv[:, :, MLA_NOPE:], ((0, 0), (0, 0), (0, MLA_SLOT - MLA_V))).reshape(rk, MLA_WIDTH)
    return wq.astype(BF16), jnp.concatenate([wk, wv], axis=1).astype(BF16)


def kernel(x, positions,
           l0_norm_mix, l0_w_in, l0_mla_q_norm, l0_mla_w_uq, l0_mla_kv_norm, l0_mla_w_ukv, l0_w_out,
           l0_norm_ffn, l0_ffn_w_gate, l0_ffn_w_up, l0_ffn_conv_w, l0_ffn_conv_b, l0_ffn_w_down,
           l1_norm_mix, l1_w_in, l1_ssm_conv_w, l1_ssm_conv_b, l1_ssm_dt_bias, l1_ssm_a_log, l1_ssm_d,
           l1_ssm_norm, l1_w_out,
           l1_norm_ffn, l1_ffn_w_gate, l1_ffn_w_up, l1_ffn_conv_w, l1_ffn_conv_b, l1_ffn_w_down,
           final_norm):
    bsz, seq, d = x.shape
    t = bsz * seq
    xf = x.reshape(t, d)
    tm = 512

    pos_lanes = jnp.broadcast_to(positions.reshape(t, 1).astype(F32), (t, LANES))
    freq_a, sign_a, freq_b, sign_b = _rope_lane_constants()
    cos_a, sin_a = _rope_tables(pos_lanes, freq_a, sign_a, tm)
    cos_b, sin_b = _rope_tables(pos_lanes, freq_b, sign_b, tm)

    nd = DSW_WIDTH
    w_qkv = l0_w_in[:, :3 * nd].astype(BF16)
    w_cq = l0_w_in[:, 3 * nd:3 * nd + MLA_Q_RANK]
    w_ckv = l0_w_in[:, 3 * nd + MLA_Q_RANK:3 * nd + MLA_Q_RANK + MLA_KV_RANK]
    w_kpe = l0_w_in[:, 3 * nd + MLA_Q_RANK + MLA_KV_RANK:]
    w_kpe_slot = jnp.pad(w_kpe, ((0, 0), (MLA_NOPE, MLA_SLOT - MLA_NOPE - MLA_ROPE)))
    w_lat = jnp.concatenate([w_cq, w_ckv, w_kpe_slot], axis=1).astype(BF16)
    wq_slot, wkv_slot = _mla_slot_weights(l0_mla_w_uq, l0_mla_w_ukv)
    qkv, q_mla, kv_mla = _l0_in_proj(xf, _row(l0_norm_mix), w_qkv, w_lat, _row(l0_mla_q_norm), wq_slot,
                                     _row(l0_mla_kv_norm), wkv_slot, cos_a, sin_a, cos_b, sin_b, tm)

    outs, lses = [], []
    for gi, (_, dilation) in enumerate(DSW_GROUPS):
        o, lse = _dilated_group(qkv, bsz, seq, gi, dilation)
        outs.append(o)
        lses.append(lse)
    y_a = _dsw_merge(outs, lses, tm)
    y_b = _mla_attention(q_mla, kv_mla, bsz, seq, tq=512)

    w_out0 = l0_w_out.astype(BF16)
    x1 = _out_proj(xf, y_a, y_b, w_out0[:DSW_GROUP_WIDTH], w_out0[DSW_GROUP_WIDTH:], tm)

    x2 = _conv_ffn(x1, _row(l0_norm_ffn), l0_ffn_w_gate.astype(BF16), l0_ffn_w_up.astype(BF16),
                   l0_ffn_conv_w, _row(l0_ffn_conv_b), l0_ffn_w_down.astype(BF16), seq, tm=tm, tf=FFN_DIM // 2)

    o_dt = 2 * SSM_INNER + SSM_BC_WIDTH
    w_ssm = _pad_cols(l1_w_in[:, :o_dt + SSM_HEADS], SSM_PROJ_WIDTH).astype(BF16)
    w_sb = l1_w_in[:, o_dt + SSM_HEADS:].astype(BF16)
    proj, qkv_sb = _l1_in_proj(x2, _row(l1_norm_mix), w_ssm, w_sb, tm)

    cw = l1_ssm_conv_w
    cb = _row(l1_ssm_conv_b)
    y_c = _ssd(proj, bsz, seq, cw[:, :SSM_INNER], cb[:, :SSM_INNER], cw[:, SSM_INNER:], cb[:, SSM_INNER:],
               _pad_cols(_row(l1_ssm_dt_bias), LANES), _pad_cols(_row(l1_ssm_a_log), LANES),
               _row(jnp.repeat(l1_ssm_d, SSM_HEADDIM)), _row(l1_ssm_norm))
    y_d = _sb_attention(qkv_sb, bsz, seq, tq=256)

    w_out1 = l1_w_out.astype(BF16)
    x3 = _out_proj(x2, y_c, y_d, w_out1[:SSM_INNER], w_out1[SSM_INNER:], tm)

    out = _conv_ffn(x3, _row(l1_norm_ffn), l1_ffn_w_gate.astype(BF16), l1_ffn_w_up.astype(BF16),
                    l1_ffn_conv_w, _row(l1_ffn_conv_b), l1_ffn_w_down.astype(BF16), seq, tm=tm, tf=FFN_DIM // 2,
                    final_gain=_row(final_norm))
    return out.reshape(bsz, seq, d)
```

```python
import functools
import math

import jax
import jax.numpy as jnp
from jax import lax
from jax.experimental import pallas as pl
from jax.experimental.pallas import tpu as pltpu

F32 = jnp.float32
BF16 = jnp.bfloat16

LANES = 128
SUBLANES = 8
MXU_WIDTH = 256
VMEM_LIMIT_BYTES = 56 * 1024 * 1024

D_MODEL = 1024
HEAD_DIM = 64
ROPE_THETA = 10000.0
NORM_EPS = 1e-6

DSW_GROUPS = ((128, 1), (512, 4), (2048, 16))
DSW_HEADS_PER_GROUP = 4
DSW_HEADS = DSW_HEADS_PER_GROUP * len(DSW_GROUPS)
DSW_BLK = 128
DSW_WIDTH = DSW_HEADS * HEAD_DIM
DSW_GROUP_WIDTH = DSW_HEADS_PER_GROUP * HEAD_DIM
DSW_UNITS_PER_BODY = 4

MLA_HEADS = 12
MLA_Q_RANK = 256
MLA_KV_RANK = 128
MLA_NOPE = 64
MLA_ROPE = 32
MLA_V = 64
MLA_SLOT = 128
MLA_WIDTH = MLA_HEADS * MLA_SLOT
MLA_LAT_WIDTH = MLA_Q_RANK + MLA_KV_RANK + MLA_SLOT
MLA_Q_SCALE = (MLA_NOPE + MLA_ROPE) ** -0.5 * math.log2(math.e)

SSM_INNER = 1024
SSM_HEADDIM = 64
SSM_HEADS = 16
SSM_STATE = 128
SSM_GROUPS = 2
SSM_CONV = 4
SSM_CHUNK = 128
SSM_GROUP_WIDTH = SSM_INNER // SSM_GROUPS
SSM_BC_WIDTH = 2 * SSM_GROUPS * SSM_STATE
SSM_DT_PAD = LANES
SSM_PROJ_WIDTH = 2 * SSM_INNER + SSM_BC_WIDTH + SSM_DT_PAD

SB_HEADS = 8
SB_WIDTH = SB_HEADS * HEAD_DIM
SB_LOG_UNDERFLOW = 104.0

FFN_DIM = 2816
FFN_CONV = 3

NEG_BIG = -1e30


def _params(*sem):
    return pltpu.CompilerParams(dimension_semantics=sem, vmem_limit_bytes=VMEM_LIMIT_BYTES)


def _rms(x, g):
    return x * lax.rsqrt(jnp.mean(x * x, axis=-1, keepdims=True) + NORM_EPS) * g


def _silu(x):
    return x * (1.0 / (1.0 + jnp.exp(-x)))


def _log1p(e):
    u = 1.0 + e
    return jnp.where(u == 1.0, e, jnp.log(u) * (e / (u - 1.0)))


def _softplus(x):
    return jnp.maximum(x, 0.0) + _log1p(jnp.exp(-jnp.abs(x)))


def _swap_halves(x, half):
    lane = lax.broadcasted_iota(jnp.int32, x.shape, 1)
    up = pltpu.roll(x, LANES - half, 1)
    down = pltpu.roll(x, half, 1)
    return jnp.where((lane & half) == 0, up, down)


def _rope_tile(x, cos, sin, half):
    return x * cos + _swap_halves(x, half) * sin


def _split2(x):
    hi = x.astype(BF16)
    return hi, (x - hi.astype(F32)).astype(BF16)


def _split3(x):
    hi = x.astype(BF16)
    r1 = x - hi.astype(F32)
    mid = r1.astype(BF16)
    lo = (r1 - mid.astype(F32)).astype(BF16)
    return hi, mid, lo


def _dot(a, b):
    return jnp.dot(a, b, preferred_element_type=F32)


def _dot_nt(a, b):
    return lax.dot_general(a, b, (((1,), (1,)), ((), ())), preferred_element_type=F32)


def _keep_head(x2, lane, second):
    mine = (lane >= HEAD_DIM) if second else (lane < HEAD_DIM)
    return jnp.where(mine, x2.astype(F32), 0.0).astype(BF16)


def _resident(shape):
    return pl.BlockSpec(shape, lambda *_: (0,) * len(shape), pipeline_mode=pl.Buffered(1))


def _rope_table_kernel(pos_ref, freq_ref, sign_ref, cos_ref, sin_ref):
    ang = pos_ref[...] * freq_ref[...]
    cos_ref[...] = jnp.cos(ang)
    sin_ref[...] = jnp.sin(ang) * sign_ref[...]


def _rope_tables(pos_lanes, freq, sign, tm):
    t = pos_lanes.shape[0]
    row = pl.BlockSpec((tm, LANES), lambda i: (i, 0))
    const = pl.BlockSpec((1, LANES), lambda i: (0, 0))
    return pl.pallas_call(
        _rope_table_kernel,
        grid=(t // tm,),
        in_specs=[row, const, const],
        out_specs=[row, row],
        out_shape=[jax.ShapeDtypeStruct((t, LANES), F32)] * 2,
        compiler_params=_params("parallel"),
        name="rope_tables",
    )(pos_lanes, freq, sign)


def _l0_in_kernel(x_ref, g_ref, wqkv_ref, wlat_ref, gq_ref, wq_ref, gkv_ref, wkv_ref,
                  cosa_ref, sina_ref, cosb_ref, sinb_ref, g0_ref, g1_ref, g2_ref, qm_ref, kvm_ref, slabs):
    tm = x_ref.shape[0]
    hn = _rms(x_ref[...], g_ref[...]).astype(BF16)
    cos_a, sin_a = cosa_ref[...], sina_ref[...]
    cos_b, sin_b = cosb_ref[...], sinb_ref[...]
    half_a, half_b = HEAD_DIM // 2, MLA_ROPE // 2
    step = MXU_WIDTH
    halves = step // LANES
    group_refs = (g0_ref, g1_ref, g2_ref)
    slab = 0
    for gi, (_, dil) in enumerate(DSW_GROUPS):
        for part in range(3):
            c0 = (3 * gi + part) * DSW_GROUP_WIDTH
            acc = _dot(hn, wqkv_ref[:, c0:c0 + step])
            for h in range(halves):
                tile = acc[:, h * LANES:(h + 1) * LANES]
                if part < 2:
                    tile = _rope_tile(tile, cos_a, sin_a, half_a)
                if part == 0:
                    tile = tile * HEAD_DIM ** -0.5
                col = part * DSW_GROUP_WIDTH + h * LANES
                if dil == 1:
                    g0_ref[:, col:col + LANES] = tile.astype(BF16)
                else:
                    slabs[slab] = tile
                    for r in range(dil):
                        rows = slabs[slab, pl.ds(r, tm // dil, stride=dil), :]
                        group_refs[gi][r, :, col:col + LANES] = rows.astype(BF16)
                    slab += 1

    lat = _dot(hn, wlat_ref[...])
    cq = _rms(lat[:, :MLA_Q_RANK], gq_ref[...]).astype(BF16)
    ckv = _rms(lat[:, MLA_Q_RANK:MLA_Q_RANK + MLA_KV_RANK], gkv_ref[...]).astype(BF16)
    kpe = _rope_tile(lat[:, MLA_Q_RANK + MLA_KV_RANK:], cos_b, sin_b, half_b)
    for c0 in range(0, MLA_WIDTH, step):
        acc = _dot(cq, wq_ref[:, c0:c0 + step])
        for h in range(halves):
            sl = slice(h * LANES, (h + 1) * LANES)
            tile = _rope_tile(acc[:, sl], cos_b, sin_b, half_b) * MLA_Q_SCALE
            qm_ref[:, c0 + h * LANES:c0 + (h + 1) * LANES] = tile.astype(BF16)
    lane = lax.broadcasted_iota(jnp.int32, kpe.shape, 1)
    one_hot = jnp.where(lane == MLA_V, 1.0, 0.0)
    for c0 in range(0, 2 * MLA_WIDTH, step):
        acc = _dot(ckv, wkv_ref[:, c0:c0 + step])
        extra = kpe if c0 < MLA_WIDTH else one_hot
        for h in range(halves):
            sl = slice(h * LANES, (h + 1) * LANES)
            kvm_ref[:, c0 + h * LANES:c0 + (h + 1) * LANES] = (acc[:, sl] + extra).astype(BF16)


def _l0_in_proj(x, g, w_qkv, w_lat, gq, wq, gkv, wkv, cos_a, sin_a, cos_b, sin_b, bsz, seq, tm):
    t, d = x.shape
    tps = seq // tm
    gw = 3 * DSW_GROUP_WIDTH
    row = lambda w: pl.BlockSpec((tm, w), lambda i: (i, 0))

    def residue_major(dil):
        return pl.BlockSpec((None, dil, tm // dil, gw), lambda i: (i // tps, 0, i % tps, 0))

    dils = [dil for _, dil in DSW_GROUPS]
    n_slabs = sum(gw // LANES for dil in dils if dil > 1)
    return pl.pallas_call(
        _l0_in_kernel,
        grid=(t // tm,),
        in_specs=[row(d), _resident((1, d)), _resident(w_qkv.shape), _resident(w_lat.shape),
                  _resident(gq.shape), _resident(wq.shape), _resident(gkv.shape), _resident(wkv.shape),
                  row(LANES), row(LANES), row(LANES), row(LANES)],
        out_specs=[row(gw), residue_major(dils[1]), residue_major(dils[2]), row(MLA_WIDTH), row(2 * MLA_WIDTH)],
        out_shape=[jax.ShapeDtypeStruct((t, gw), BF16),
                   jax.ShapeDtypeStruct((bsz, dils[1], seq // dils[1], gw), BF16),
                   jax.ShapeDtypeStruct((bsz, dils[2], seq // dils[2], gw), BF16),
                   jax.ShapeDtypeStruct((t, MLA_WIDTH), BF16),
                   jax.ShapeDtypeStruct((t, 2 * MLA_WIDTH), BF16)],
        scratch_shapes=[pltpu.VMEM((n_slabs, tm, LANES), F32)],
        compiler_params=_params("parallel"),
        name="l0_in_proj",
    )(x, g, w_qkv, w_lat, gq, wq, gkv, wkv, cos_a, sin_a, cos_b, sin_b)


def _l1_in_kernel(x_ref, g_ref, wssm_ref, wsb_ref, proj_ref, qkv_ref):
    hn = _rms(x_ref[...], g_ref[...]).astype(BF16)
    step = 3 * LANES
    for c0 in range(0, SSM_PROJ_WIDTH, step):
        proj_ref[:, c0:c0 + step] = _dot(hn, wssm_ref[:, c0:c0 + step])
    step = MXU_WIDTH
    for c0 in range(0, 3 * SB_WIDTH, step):
        qkv_ref[:, c0:c0 + step] = _dot(hn, wsb_ref[:, c0:c0 + step]).astype(BF16)


def _l1_in_proj(x, g, w_ssm, w_sb, tm):
    t, d = x.shape
    row = lambda w: pl.BlockSpec((tm, w), lambda i: (i, 0))
    return pl.pallas_call(
        _l1_in_kernel,
        grid=(t // tm,),
        in_specs=[row(d), _resident((1, d)), _resident(w_ssm.shape), _resident(w_sb.shape)],
        out_specs=[row(SSM_PROJ_WIDTH), row(3 * SB_WIDTH)],
        out_shape=[jax.ShapeDtypeStruct((t, SSM_PROJ_WIDTH), F32),
                   jax.ShapeDtypeStruct((t, 3 * SB_WIDTH), BF16)],
        compiler_params=_params("parallel"),
        name="l1_in_proj",
    )(x, g, w_ssm, w_sb)


def _dilated_kernel(c0_ref, p0_ref, c1_ref, p1_ref, c2_ref, p2_ref, y_ref, o_acc, l_acc):
    blk = DSW_BLK
    gw = DSW_GROUP_WIDTH
    pairs = gw // LANES
    qcols, kcols, vcols = slice(0, gw), slice(gw, 2 * gw), slice(2 * gw, 3 * gw)
    first_key = jnp.where(pl.program_id(1) > 0, 0, blk)
    qi = lax.broadcasted_iota(jnp.int32, (blk, 2 * blk), 0)
    kj = lax.broadcasted_iota(jnp.int32, (blk, 2 * blk), 1)
    window = (kj >= qi) & (kj <= qi + blk)
    lane = lax.broadcasted_iota(jnp.int32, (blk, LANES), 1)
    first = lane < HEAD_DIM

    def unit(q, k2, v2, lowest_key):
        valid = window & (kj >= lowest_key)
        res = []
        for pair in range(pairs):
            sl = slice(pair * LANES, (pair + 1) * LANES)
            kk, vv = k2[:, sl], v2[:, sl]
            outs, lses = [], []
            for hh in range(2):
                s = jnp.where(valid, _dot_nt(_keep_head(q[:, sl], lane, hh), kk), NEG_BIG)
                m = jnp.max(s, axis=-1, keepdims=True)
                p = jnp.exp(s - m)
                l = jnp.sum(p, axis=-1, keepdims=True)
                outs.append(_dot((p * (1.0 / l)).astype(BF16), vv))
                lses.append(jnp.broadcast_to(m + jnp.log(l), (blk, LANES)))
            res.append((jnp.where(first, outs[0], outs[1]), jnp.where(first, lses[0], lses[1])))
        return res

    def put(rows, res):
        for pair, (o, lse) in enumerate(res):
            o_acc[pair, rows, :] = o
            l_acc[pair, rows, :] = lse

    def merge(rows, res):
        for pair, (o, lse) in enumerate(res):
            o_old, l_old = o_acc[pair, rows, :], l_acc[pair, rows, :]
            top = jnp.maximum(l_old, lse)
            l_new = top + jnp.log(jnp.exp(l_old - top) + jnp.exp(lse - top))
            o_acc[pair, rows, :] = o_old * jnp.exp(l_old - l_new) + o * jnp.exp(lse - l_new)
            l_acc[pair, rows, :] = l_new

    def stacked(prev, cur):
        return jnp.concatenate([prev, cur], axis=0)

    head = slice(0, blk)

    def head_unit(cur, prev):
        return unit(cur(head, qcols), stacked(prev(kcols), cur(head, kcols)),
                    stacked(prev(vcols), cur(head, vcols)), first_key)

    def later_unit(cur, r0):
        keys = pl.ds(r0 - blk, 2 * blk)
        return unit(cur(pl.ds(r0, blk), qcols), cur(keys, kcols), cur(keys, vcols), 0)

    units_per_body = DSW_UNITS_PER_BODY

    cur0 = lambda rows, cols: c0_ref[rows, cols]
    put(head, head_unit(cur0, lambda cols: p0_ref[:, cols]))
    later_blocks = c0_ref.shape[0] // blk - 1
    group0 = 3
    assert later_blocks % group0 == 0

    def dense_blocks(u, carry):
        for k in range(group0):
            r0 = pl.multiple_of((1 + u * group0 + k) * blk, blk)
            put(pl.ds(r0, blk), later_unit(cur0, r0))
        return carry

    lax.fori_loop(0, later_blocks // group0, dense_blocks, 0)

    for cur_ref, prev_ref in ((c1_ref, p1_ref), (c2_ref, p2_ref)):
        dil, blocks = cur_ref.shape[0], cur_ref.shape[1] // blk
        residues_per_body = max(1, units_per_body // blocks)
        assert dil % residues_per_body == 0

        def residues(u, carry, cur_ref=cur_ref, prev_ref=prev_ref, dil=dil, blocks=blocks,
                     residues_per_body=residues_per_body):
            for k in range(residues_per_body):
                r = u * residues_per_body + k
                cur = lambda rows, cols, r=r: cur_ref[r, rows, cols]
                merge(pl.ds(r, blk, stride=dil), head_unit(cur, lambda cols, r=r: prev_ref[r, :, cols]))
                for n in range(1, blocks):
                    merge(pl.ds(n * blk * dil + r, blk, stride=dil), later_unit(cur, n * blk))
            return carry

        lax.fori_loop(0, dil // residues_per_body, residues, 0)

    for pair in range(pairs):
        y_ref[:, pair * LANES:(pair + 1) * LANES] = o_acc[pair].astype(y_ref.dtype)


def _dilated_attention(g0, g1, g2, bsz, seq):
    blk = DSW_BLK
    span = blk * max(dil for _, dil in DSW_GROUPS)
    nspan = seq // span
    gw = 3 * DSW_GROUP_WIDTH
    d1, d2 = g1.shape[1], g2.shape[1]
    blocks_per_span0 = span // blk

    def residue_major(dil, rows, prev):
        per_span = span // dil // rows
        if prev:
            return pl.BlockSpec((None, dil, rows, gw), lambda b, s: (b, 0, jnp.maximum(s * per_span - 1, 0), 0))
        return pl.BlockSpec((None, dil, rows, gw), lambda b, s: (b, 0, s, 0))

    return pl.pallas_call(
        _dilated_kernel,
        grid=(bsz, nspan),
        in_specs=[
            pl.BlockSpec((span, gw), lambda b, s: (b * nspan + s, 0)),
            pl.BlockSpec((blk, gw), lambda b, s: (jnp.maximum((b * nspan + s) * blocks_per_span0 - 1, 0), 0)),
            residue_major(d1, span // d1, False), residue_major(d1, blk, True),
            residue_major(d2, span // d2, False), residue_major(d2, blk, True),
        ],
        out_specs=pl.BlockSpec((span, DSW_GROUP_WIDTH), lambda b, s: (b * nspan + s, 0)),
        out_shape=jax.ShapeDtypeStruct((bsz * seq, DSW_GROUP_WIDTH), BF16),
        scratch_shapes=[pltpu.VMEM((DSW_GROUP_WIDTH // LANES, span, LANES), F32)] * 2,
        compiler_params=_params("parallel", "arbitrary"),
        name="dilated_attention",
    )(g0, g0, g1, g1, g2, g2)


def _mla_attn_kernel(q_ref, k_ref, v_ref, o_ref, *, tq, tk):
    i = pl.program_id(2)
    diag_blocks = tq // tk
    assert diag_blocks % 2 == 0
    lane = lax.broadcasted_iota(jnp.int32, (tq, LANES), 1)
    row = lax.broadcasted_iota(jnp.int32, (tq, tk), 0)
    col = lax.broadcasted_iota(jnp.int32, (tq, tk), 1)
    slots = [slice(hh * MLA_SLOT, (hh + 1) * MLA_SLOT) for hh in range(2)]
    qs = [q_ref[:, sl] for sl in slots]

    def block(j, carry, visible=None):
        start = pl.multiple_of(j * tk, tk)
        new = []
        for hh in range(2):
            m, acc = carry[hh]
            s = _dot_nt(qs[hh], k_ref[pl.ds(start, tk), slots[hh]])
            if visible is not None:
                s = jnp.where(visible, s, NEG_BIG)
            m_new = jnp.maximum(m, jnp.max(s, axis=-1, keepdims=True))
            p = jnp.exp2(s - m_new)
            acc = jnp.exp2(m - m_new) * acc + _dot(p.astype(BF16), v_ref[pl.ds(start, tk), slots[hh]])
            new.append((m_new, acc))
        return tuple(new)

    def block_pair(jp, carry):
        return block(2 * jp + 1, block(2 * jp, carry))

    init = tuple((jnp.full((tq, 1), NEG_BIG, F32), jnp.zeros((tq, LANES), F32)) for _ in range(2))
    carry = lax.fori_loop(0, i * (diag_blocks // 2), block_pair, init)
    for b in range(diag_blocks):
        carry = block(i * diag_blocks + b, carry, col + b * tk <= row)
    (_, acc0), (_, acc1) = carry
    o0 = acc0 * (1.0 / acc0[:, MLA_V:MLA_V + 1])
    o1 = acc1 * (1.0 / acc1[:, MLA_V:MLA_V + 1])
    o_ref[...] = jnp.where(lane < MLA_V, o0, pltpu.roll(o1, MLA_V, 1)).astype(o_ref.dtype)


def _mla_attention(q, kv, bsz, seq, tq, tk):
    nq = seq // tq
    pairs = MLA_HEADS // 2
    kern = functools.partial(_mla_attn_kernel, tq=tq, tk=tk)
    return pl.pallas_call(
        kern,
        grid=(bsz, pairs, nq),
        in_specs=[
            pl.BlockSpec((tq, 2 * MLA_SLOT), lambda b, p, i: (b * nq + i, p)),
            pl.BlockSpec((seq, 2 * MLA_SLOT), lambda b, p, i: (b, p)),
            pl.BlockSpec((seq, 2 * MLA_SLOT), lambda b, p, i: (b, pairs + p)),
        ],
        out_specs=pl.BlockSpec((tq, 2 * MLA_V), lambda b, p, i: (b * nq + i, p)),
        out_shape=jax.ShapeDtypeStruct((bsz * seq, MLA_HEADS * MLA_V), BF16),
        compiler_params=_params("parallel", "parallel", "arbitrary"),
        name="mla_attention",
    )(q, kv, kv)


def _sb_attn_kernel(q_ref, k_ref, v_ref, o_ref, *, tq):
    i = pl.program_id(2)
    scale = HEAD_DIM ** -0.5
    lane = lax.broadcasted_iota(jnp.int32, (tq, LANES), 1)
    row = lax.broadcasted_iota(jnp.int32, (tq, tq), 0)
    col = lax.broadcasted_iota(jnp.int32, (tq, tq), 1)
    strict = col < row
    later = jnp.where(row > col, 1.0, 0.0).astype(BF16)
    q2 = q_ref[...]
    qs = [_keep_head(q2, lane, hh) for hh in range(2)]

    def block(blk, runs, accs, masked):
        start = pl.multiple_of(blk * tq, tq)
        kb, vb = k_ref[pl.ds(start, tq), :], v_ref[pl.ds(start, tq), :]
        new_runs, new_accs = [], []
        for hh in range(2):
            z = _dot_nt(qs[hh], kb) * scale
            sp = jnp.log(1.0 + jnp.exp(-jnp.abs(z)))
            log_beta = jnp.minimum(z, 0.0) - sp
            log_stay = -jnp.maximum(z, 0.0) - sp
            if masked:
                log_stay = jnp.where(strict, log_stay, 0.0)
            hi, lo = _split2(log_stay)
            after = _dot(hi, later) + _dot(lo, later) + runs[hh]
            w = jnp.exp(log_beta + after)
            if masked:
                w = jnp.where(strict, w, 0.0)
            new_accs.append(accs[hh] + _dot(w.astype(BF16), vb))
            new_runs.append(runs[hh] + jnp.sum(log_stay, axis=-1, keepdims=True))
        return tuple(new_runs), tuple(new_accs)

    run0 = (jnp.zeros((tq, 1), F32), jnp.zeros((tq, 1), F32))
    acc0 = (jnp.zeros((tq, LANES), F32), jnp.zeros((tq, LANES), F32))

    def diagonal_only():
        return block(i, run0, acc0, True)

    def diagonal_and_previous():
        runs, accs = block(i, run0, acc0, True)
        return block(i - 1, runs, accs, False)

    runs, accs = lax.cond(i > 0, diagonal_and_previous, diagonal_only)

    def cond(c):
        left, runs, _ = c
        alive = jnp.maximum(jnp.max(runs[0]), jnp.max(runs[1])) >= -SB_LOG_UNDERFLOW
        return (left > 0) & alive

    def body(c):
        left, runs, accs = c
        runs, accs = block(left - 1, runs, accs, False)
        return left - 1, runs, accs

    _, _, accs = lax.while_loop(cond, body, (jnp.maximum(i - 1, 0), runs, accs))
    o_ref[...] = jnp.where(lane < HEAD_DIM, accs[0], accs[1]).astype(o_ref.dtype)


def _sb_attention(qkv, bsz, seq, tq):
    nq = seq // tq
    pairs = SB_WIDTH // LANES
    kern = functools.partial(_sb_attn_kernel, tq=tq)
    return pl.pallas_call(
        kern,
        grid=(bsz, pairs, nq),
        in_specs=[
            pl.BlockSpec((tq, LANES), lambda b, p, i: (b * nq + i, p)),
            pl.BlockSpec((seq, LANES), lambda b, p, i: (b, pairs + p)),
            pl.BlockSpec((seq, LANES), lambda b, p, i: (b, 2 * pairs + p)),
        ],
        out_specs=pl.BlockSpec((tq, LANES), lambda b, p, i: (b * nq + i, p)),
        out_shape=jax.ShapeDtypeStruct((bsz * seq, SB_WIDTH), BF16),
        compiler_params=_params("parallel", "parallel", "arbitrary"),
        name="stickbreak_attention",
    )(qkv, qkv, qkv)


def _ssd_kernel(z_ref, xs_ref, bc_ref, dt_ref, cwx_ref, cbx_ref, cwb_ref, cbb_ref, dtb_ref, alog_ref, dskip_ref,
                gn_ref, y_ref, xtail, btail, state):
    c = pl.program_id(1)
    cl = SSM_CHUNK
    halo = SUBLANES
    assert SSM_CONV == 4

    @pl.when(c == 0)
    def _():
        xtail[...] = jnp.zeros_like(xtail)
        btail[...] = jnp.zeros_like(btail)
        state[...] = jnp.zeros_like(state)

    def shift_rows(x, tail, k):
        rolled = pltpu.roll(x, k, 0)
        wrapped = pltpu.roll(tail, k, 0)
        row = lax.broadcasted_iota(jnp.int32, tail.shape, 0)
        return jnp.concatenate([jnp.where(row < k, wrapped, rolled[:halo]), rolled[halo:]], axis=0)

    def conv_silu(tails, raw_ref, w_ref, b_ref):
        x = raw_ref[...]
        x1 = shift_rows(x, tails[0], 1)
        u = w_ref[1:2, :] * x + w_ref[0:1, :] * x1
        y = b_ref[...] + w_ref[3:4, :] * x + w_ref[2:3, :] * x1 + shift_rows(u, tails[1], 2)
        tails[0] = x[cl - halo:, :]
        tails[1] = u[cl - halo:, :]
        return _silu(y)

    xs = conv_silu(xtail, xs_ref, cwx_ref, cbx_ref)
    bc = conv_silu(btail, bc_ref, cwb_ref, cbb_ref)

    dt = _softplus(dt_ref[...] + dtb_ref[...])
    da = dt * (-jnp.exp(alog_ref[...]))
    row = lax.broadcasted_iota(jnp.int32, (cl, cl), 0)
    col = lax.broadcasted_iota(jnp.int32, (cl, cl), 1)
    causal = col <= row
    tri = jnp.where(causal, 1.0, 0.0).astype(BF16)
    d_hi, d_mid, d_lo = _split3(da)
    cs = _dot(tri, d_hi) + _dot(tri, d_mid) + _dot(tri, d_lo)
    cs_t = cs.T
    ecs = jnp.exp(cs)
    dec_end = jnp.exp(cs[cl - 1:cl, :] - cs)

    head_of_lane = jnp.right_shift(lax.broadcasted_iota(jnp.int32, (LANES, SSM_INNER), 1), 6)
    expand = jnp.where(lax.broadcasted_iota(jnp.int32, (LANES, SSM_INNER), 0) == head_of_lane, 1.0, 0.0).astype(BF16)

    def per_head_lanes(v):
        a, b, c3 = _split3(v)
        return _dot(a, expand) + _dot(b, expand) + _dot(c3, expand)

    dt_e = per_head_lanes(dt)
    ecs_e = per_head_lanes(ecs)
    dec_end_e = per_head_lanes(dec_end)

    xdt = xs * dt_e
    xdt_b = xdt.astype(BF16)
    xw_b = (xdt * dec_end_e).astype(BF16)
    lane = lax.broadcasted_iota(jnp.int32, (cl, LANES), 1)
    first = lane < SSM_HEADDIM
    gs = SSM_STATE
    heads_per_group = SSM_HEADS // SSM_GROUPS
    for g in range(SSM_GROUPS):
        bg = bc[:, g * gs:(g + 1) * gs]
        cg_b = bc[:, (SSM_GROUPS + g) * gs:(SSM_GROUPS + g + 1) * gs].astype(BF16)
        cb = _dot_nt(cg_b, bg.astype(BF16))
        cols = slice(g * SSM_GROUP_WIDTH, (g + 1) * SSM_GROUP_WIDTH)
        prev = state[:, cols]
        y_off = _dot(cg_b, prev.astype(BF16)) * ecs_e[:, cols]
        state[:, cols] = prev * ecs_e[cl - 1:cl, cols] + _dot(bg.T.astype(BF16), xw_b[:, cols])
        for pr in range(heads_per_group // 2):
            pcols = slice(g * SSM_GROUP_WIDTH + pr * LANES, g * SSM_GROUP_WIDTH + (pr + 1) * LANES)
            x_pair = xdt_b[:, pcols]
            ys = []
            for hh in range(2):
                h = g * heads_per_group + 2 * pr + hh
                seg = jnp.where(causal, cs[:, h:h + 1] - cs_t[h:h + 1, :], NEG_BIG)
                ys.append(_dot((cb * jnp.exp(seg)).astype(BF16), x_pair))
            y_diag = jnp.where(first, ys[0], ys[1])
            y_pair = y_diag + y_off[:, pr * LANES:(pr + 1) * LANES] + xs[:, pcols] * dskip_ref[:, pcols]
            y_ref[:, pcols] = y_pair * _silu(z_ref[:, pcols])
    for g in range(SSM_GROUPS):
        cols = slice(g * SSM_GROUP_WIDTH, (g + 1) * SSM_GROUP_WIDTH)
        y_ref[:, cols] = _rms(y_ref[:, cols], gn_ref[:, cols])


def _ssd(proj, bsz, seq, cwx, cbx, cwb, cbb, dt_bias, a_log, d_skip, gnorm):
    cl = SSM_CHUNK
    nc = seq // cl

    def rows(width, colblk):
        return pl.BlockSpec((cl, width), lambda b, c: (b * nc + c, colblk))

    def const(r, width):
        return pl.BlockSpec((r, width), lambda b, c: (0, 0))

    return pl.pallas_call(
        _ssd_kernel,
        grid=(bsz, nc),
        in_specs=[
            rows(SSM_INNER, 0),
            rows(SSM_INNER, 1),
            rows(SSM_BC_WIDTH, 2 * SSM_INNER // SSM_BC_WIDTH),
            rows(LANES, (2 * SSM_INNER + SSM_BC_WIDTH) // LANES),
            const(SSM_CONV, SSM_INNER), const(1, SSM_INNER), const(SSM_CONV, SSM_BC_WIDTH), const(1, SSM_BC_WIDTH),
            const(1, LANES), const(1, LANES), const(1, SSM_INNER), const(1, SSM_INNER),
        ],
        out_specs=pl.BlockSpec((cl, SSM_INNER), lambda b, c: (b * nc + c, 0)),
        out_shape=jax.ShapeDtypeStruct((bsz * seq, SSM_INNER), F32),
        scratch_shapes=[
            pltpu.VMEM((2, SUBLANES, SSM_INNER), F32),
            pltpu.VMEM((2, SUBLANES, SSM_BC_WIDTH), F32),
            pltpu.VMEM((SSM_STATE, SSM_INNER), F32),
        ],
        compiler_params=_params("parallel", "arbitrary"),
        name="ssd_scan",
    )(proj, proj, proj, proj, cwx, cbx, cwb, cbb, dt_bias, a_log, d_skip, gnorm)


def _mix_ffn_kernel(*refs, tm, tiles_per_seq, final_norm):
    x_ref, a_ref, b_ref, wa_ref, wb_ref, g_ref, wg_ref, wu_ref, cw_ref, cb_ref, wd_ref = refs[:11]
    pos = 11
    fn_ref = None
    if final_norm:
        fn_ref = refs[pos]
        pos += 1
    o_ref, gbuf, act_ref = refs[pos:pos + 3]
    halo = SUBLANES

    o_ref[...] = x_ref[...] + _dot(a_ref[...].astype(BF16), wa_ref[...]) + _dot(b_ref[...].astype(BF16), wb_ref[...])
    hn = _rms(o_ref[...], g_ref[...]).astype(BF16)

    @pl.when(pl.program_id(0) % tiles_per_seq == 0)
    def _():
        gbuf[0:halo, :] = jnp.zeros((halo, gbuf.shape[1]), F32)

    step = MXU_WIDTH
    for c0 in range(0, FFN_DIM, step):
        cols = slice(c0, c0 + step)
        gate = _dot(hn, wg_ref[:, cols])
        up = _dot(hn, wu_ref[:, cols])
        gbuf[halo:halo + tm, cols] = gate
        conv = cb_ref[:, cols] + cw_ref[FFN_CONV - 1:FFN_CONV, cols] * gate
        for t in range(FFN_CONV - 1):
            back = FFN_CONV - 1 - t
            conv = conv + cw_ref[t:t + 1, cols] * gbuf[halo - back:halo - back + tm, cols]
        gbuf[0:halo, cols] = gbuf[tm:tm + halo, cols]
        act_ref[:, cols] = (_silu(conv) * up).astype(BF16)

    y = o_ref[...] + _dot(act_ref[...], wd_ref[...])
    if final_norm:
        y = _rms(y, fn_ref[...])
    o_ref[...] = y


def _mix_ffn(x, a, b, wa, wb, g, wg, wu, cw, cb, wd, seq, *, tm, final_gain=None):
    t, d = x.shape
    f = wg.shape[1]
    final_norm = final_gain is not None
    row = lambda w: pl.BlockSpec((tm, w), lambda i: (i, 0))
    in_specs = [row(d), row(a.shape[1]), row(b.shape[1]), _resident(wa.shape), _resident(wb.shape),
                _resident(g.shape), _resident(wg.shape), _resident(wu.shape), _resident(cw.shape),
                _resident(cb.shape), _resident(wd.shape)]
    args = [x, a, b, wa, wb, g, wg, wu, cw, cb, wd]
    if final_norm:
        in_specs.append(_resident(final_gain.shape))
        args.append(final_gain)
    kern = functools.partial(_mix_ffn_kernel, tm=tm, tiles_per_seq=seq // tm, final_norm=final_norm)
    return pl.pallas_call(
        kern,
        grid=(t // tm,),
        in_specs=in_specs,
        out_specs=row(d),
        out_shape=jax.ShapeDtypeStruct((t, d), F32),
        scratch_shapes=[pltpu.VMEM((tm + SUBLANES, f), F32), pltpu.VMEM((tm, f), BF16)],
        compiler_params=_params("arbitrary"),
        name="mix_ffn",
    )(*args)


def _row(v):
    return v.reshape(1, -1).astype(F32)


def _pad_cols(w, n):
    return jnp.pad(w, ((0, 0), (0, n - w.shape[1])))


def _rope_lane_constants():
    lane = jnp.arange(LANES)
    half_a = HEAD_DIM // 2
    inv_a = 1.0 / (ROPE_THETA ** (jnp.arange(half_a, dtype=F32) * (2.0 / HEAD_DIM)))
    freq_a = inv_a[lane % half_a]
    sign_a = jnp.where((lane % HEAD_DIM) < half_a, -1.0, 1.0).astype(F32)
    half_b = MLA_ROPE // 2
    inv_b = 1.0 / (ROPE_THETA ** (jnp.arange(half_b, dtype=F32) * (2.0 / MLA_ROPE)))
    in_rope = (lane >= MLA_NOPE) & (lane < MLA_NOPE + MLA_ROPE)
    freq_b = jnp.where(in_rope, inv_b[(lane - MLA_NOPE) % half_b], 0.0)
    sign_b = jnp.where(in_rope, jnp.where((lane - MLA_NOPE) < half_b, -1.0, 1.0), 0.0).astype(F32)
    return _row(freq_a), _row(sign_a), _row(freq_b), _row(sign_b)


def _mla_slot_weights(w_uq, w_ukv):
    rq = w_uq.shape[0]
    wq = w_uq.reshape(rq, MLA_HEADS, MLA_NOPE + MLA_ROPE)
    wq = jnp.pad(wq, ((0, 0), (0, 0), (0, MLA_SLOT - MLA_NOPE - MLA_ROPE))).reshape(rq, MLA_WIDTH)
    rk = w_ukv.shape[0]
    wkv = w_ukv.reshape(rk, MLA_HEADS, MLA_NOPE + MLA_V)
    wk = jnp.pad(wkv[:, :, :MLA_NOPE], ((0, 0), (0, 0), (0, MLA_SLOT - MLA_NOPE))).reshape(rk, MLA_WIDTH)
    wv = jnp.pad(wkv[:, :, MLA_NOPE:], ((0, 0), (0, 0), (0, MLA_SLOT - MLA_V))).reshape(rk, MLA_WIDTH)
    return wq.astype(BF16), jnp.concatenate([wk, wv], axis=1).astype(BF16)


def kernel(x, positions,
           l0_norm_mix, l0_w_in, l0_mla_q_norm, l0_mla_w_uq, l0_mla_kv_norm, l0_mla_w_ukv, l0_w_out,
           l0_norm_ffn, l0_ffn_w_gate, l0_ffn_w_up, l0_ffn_conv_w, l0_ffn_conv_b, l0_ffn_w_down,
           l1_norm_mix, l1_w_in, l1_ssm_conv_w, l1_ssm_conv_b, l1_ssm_dt_bias, l1_ssm_a_log, l1_ssm_d,
           l1_ssm_norm, l1_w_out,
           l1_norm_ffn, l1_ffn_w_gate, l1_ffn_w_up, l1_ffn_conv_w, l1_ffn_conv_b, l1_ffn_w_down,
           final_norm):
    bsz, seq, d = x.shape
    t = bsz * seq
    xf = x.reshape(t, d)
    tm = 512

    pos_lanes = jnp.broadcast_to(positions.reshape(t, 1).astype(F32), (t, LANES))
    freq_a, sign_a, freq_b, sign_b = _rope_lane_constants()
    cos_a, sin_a = _rope_tables(pos_lanes, freq_a, sign_a, tm)
    cos_b, sin_b = _rope_tables(pos_lanes, freq_b, sign_b, tm)

    nd = DSW_WIDTH
    gw = DSW_GROUP_WIDTH
    w_qkv = jnp.concatenate([l0_w_in[:, part * nd + gi * gw:part * nd + (gi + 1) * gw]
                             for gi in range(len(DSW_GROUPS)) for part in range(3)], axis=1).astype(BF16)
    w_cq = l0_w_in[:, 3 * nd:3 * nd + MLA_Q_RANK]
    w_ckv = l0_w_in[:, 3 * nd + MLA_Q_RANK:3 * nd + MLA_Q_RANK + MLA_KV_RANK]
    w_kpe = l0_w_in[:, 3 * nd + MLA_Q_RANK + MLA_KV_RANK:]
    w_kpe_slot = jnp.pad(w_kpe, ((0, 0), (MLA_NOPE, MLA_SLOT - MLA_NOPE - MLA_ROPE)))
    w_lat = jnp.concatenate([w_cq, w_ckv, w_kpe_slot], axis=1).astype(BF16)
    wq_slot, wkv_slot = _mla_slot_weights(l0_mla_w_uq, l0_mla_w_ukv)
    g0, g1, g2, q_mla, kv_mla = _l0_in_proj(xf, _row(l0_norm_mix), w_qkv, w_lat, _row(l0_mla_q_norm), wq_slot,
                                            _row(l0_mla_kv_norm), wkv_slot, cos_a, sin_a, cos_b, sin_b,
                                            bsz, seq, tm)
    y_a = _dilated_attention(g0, g1, g2, bsz, seq)
    y_b = _mla_attention(q_mla, kv_mla, bsz, seq, tq=1024, tk=512)

    w_out0 = l0_w_out.astype(BF16)
    x2 = _mix_ffn(xf, y_a, y_b, w_out0[:DSW_GROUP_WIDTH], w_out0[DSW_GROUP_WIDTH:], _row(l0_norm_ffn),
                  l0_ffn_w_gate.astype(BF16), l0_ffn_w_up.astype(BF16), l0_ffn_conv_w, _row(l0_ffn_conv_b),
                  l0_ffn_w_down.astype(BF16), seq, tm=tm)

    o_dt = 2 * SSM_INNER + SSM_BC_WIDTH
    w_ssm = _pad_cols(l1_w_in[:, :o_dt + SSM_HEADS], SSM_PROJ_WIDTH).astype(BF16)
    w_sb = l1_w_in[:, o_dt + SSM_HEADS:].astype(BF16)
    proj, qkv_sb = _l1_in_proj(x2, _row(l1_norm_mix), w_ssm, w_sb, tm)

    cw = l1_ssm_conv_w
    cb = _row(l1_ssm_conv_b)
    y_c = _ssd(proj, bsz, seq, cw[:, :SSM_INNER], cb[:, :SSM_INNER], cw[:, SSM_INNER:], cb[:, SSM_INNER:],
               _pad_cols(_row(l1_ssm_dt_bias), LANES), _pad_cols(_row(l1_ssm_a_log), LANES),
               _row(jnp.repeat(l1_ssm_d, SSM_HEADDIM)), _row(l1_ssm_norm))
    y_d = _sb_attention(qkv_sb, bsz, seq, tq=256)

    w_out1 = l1_w_out.astype(BF16)
    out = _mix_ffn(x2, y_c, y_d, w_out1[:SSM_INNER], w_out1[SSM_INNER:], _row(l1_norm_ffn),
                   l1_ffn_w_gate.astype(BF16), l1_ffn_w_up.astype(BF16), l1_ffn_conv_w, _row(l1_ffn_conv_b),
                   l1_ffn_w_down.astype(BF16), seq, tm=tm, final_gain=_row(final_norm))
    return out.reshape(bsz, seq, d)
```

```python
import functools
import math

import jax
import jax.numpy as jnp
from jax import lax
from jax.experimental import pallas as pl
from jax.experimental.pallas import tpu as pltpu

F32 = jnp.float32
BF16 = jnp.bfloat16

LANES = 128
SUBLANES = 8
MXU_WIDTH = 256
VMEM_LIMIT_BYTES = 56 * 1024 * 1024

D_MODEL = 1024
HEAD_DIM = 64
ROPE_THETA = 10000.0
ROPE_PACK = 4
NORM_EPS = 1e-6

DSW_GROUPS = ((128, 1), (512, 4), (2048, 16))
DSW_HEADS_PER_GROUP = 4
DSW_HEADS = DSW_HEADS_PER_GROUP * len(DSW_GROUPS)
DSW_BLK = 128
DSW_WIDTH = DSW_HEADS * HEAD_DIM
DSW_GROUP_WIDTH = DSW_HEADS_PER_GROUP * HEAD_DIM
DSW_UNITS_PER_BODY = 4

MLA_HEADS = 12
MLA_Q_RANK = 256
MLA_KV_RANK = 128
MLA_NOPE = 64
MLA_ROPE = 32
MLA_V = 64
MLA_SLOT = 128
MLA_WIDTH = MLA_HEADS * MLA_SLOT
MLA_LAT_WIDTH =MLA_Q_RANK + MLA_KV_RANK + MLA_SLOT
MLA_Q_SCALE = (MLA_NOPE + MLA_ROPE) ** -0.5 * math.log2(math.e)

SSM_INNER = 1024
SSM_HEADDIM = 64
SSM_HEADS = 16
SSM_STATE = 128
SSM_GROUPS = 2
SSM_CONV = 4
SSM_CHUNK = 128
SSM_GROUP_WIDTH = SSM_INNER // SSM_GROUPS
SSM_BC_WIDTH = 2 * SSM_GROUPS * SSM_STATE
SSM_DT_PAD = LANES
SSM_PROJ_WIDTH = 2 * SSM_INNER + SSM_BC_WIDTH + SSM_DT_PAD

SB_HEADS = 8
SB_WIDTH = SB_HEADS * HEAD_DIM
SB_LOG_UNDERFLOW = 104.0

FFN_DIM = 2816
FFN_CONV = 3

NEG_BIG = -1e30


def _params(*sem):
    return pltpu.CompilerParams(dimension_semantics=sem, vmem_limit_bytes=VMEM_LIMIT_BYTES)


def _rms(x, g):
    return x * lax.rsqrt(jnp.mean(x * x, axis=-1, keepdims=True) + NORM_EPS) * g


def _silu(x):
    return x * (1.0 / (1.0 + jnp.exp(-x)))


def _log1p(e):
    u = 1.0 + e
    return jnp.where(u == 1.0, e, jnp.log(u) * (e / (u - 1.0)))


def _softplus(x):
    return jnp.maximum(x, 0.0) + _log1p(jnp.exp(-jnp.abs(x)))


def _swap_halves(x, half):
    lane = lax.broadcasted_iota(jnp.int32, x.shape, 1)
    up = pltpu.roll(x, LANES - half, 1)
    down = pltpu.roll(x, half, 1)
    return jnp.where((lane & half) == 0, up, down)


def _rope_tile(x, cos, sin, half):
    return x * cos + _swap_halves(x, half) * sin


def _split2(x):
    hi = x.astype(BF16)
    return hi, (x - hi.astype(F32)).astype(BF16)


def _split3(x):
    hi = x.astype(BF16)
    r1 = x - hi.astype(F32)
    mid = r1.astype(BF16)
    lo = (r1 - mid.astype(F32)).astype(BF16)
    return hi, mid, lo


def _dot(a, b):
    return jnp.dot(a, b, preferred_element_type=F32)


def _dot_nt(a, b):
    return lax.dot_general(a, b, (((1,), (1,)), ((), ())), preferred_element_type=F32)


def _keep_head(x2, lane, second):
    mine = (lane >= HEAD_DIM) if second else (lane < HEAD_DIM)
    return jnp.where(mine, x2.astype(F32), 0.0).astype(BF16)


def _resident(shape):
    return pl.BlockSpec(shape, lambda *_: (0,) * len(shape), pipeline_mode=pl.Buffered(1))


def _rope_table_kernel(pos_ref, freq_ref, sela_ref, selb_ref, signa_ref, signb_ref, restb_ref,
                       cosa_ref, sina_ref, cosb_ref, sinb_ref):
    ang = pos_ref[...] * freq_ref[...]
    cos3, sin3 = _split3(jnp.cos(ang)), _split3(jnp.sin(ang))
    rows = ang.shape[0]

    def spread(parts, sel):
        return _dot(parts[0], sel) + _dot(parts[1], sel) + _dot(parts[2], sel)

    for g in range(ROPE_PACK):
        tokens = pl.ds(g, rows, stride=ROPE_PACK)
        cosa_ref[tokens, :] = spread(cos3, sela_ref[g])
        sina_ref[tokens, :] = spread(sin3, sela_ref[g]) * signa_ref[...]
        cosb_ref[tokens, :] = spread(cos3, selb_ref[g]) + restb_ref[...]
        sinb_ref[tokens, :] = spread(sin3, selb_ref[g]) * signb_ref[...]


def _rope_tables(pos_packed, freq, sel_a, sel_b, sign_a, sign_b, rest_b, rows):
    packed = pos_packed.shape[0]
    t = packed * ROPE_PACK
    const = pl.BlockSpec((1, LANES), lambda i: (0, 0))
    sel = pl.BlockSpec((ROPE_PACK, LANES, LANES), lambda i: (0, 0, 0))
    out = pl.BlockSpec((rows * ROPE_PACK, LANES), lambda i: (i, 0))
    return pl.pallas_call(
        _rope_table_kernel,
        grid=(packed // rows,),
        in_specs=[pl.BlockSpec((rows, LANES), lambda i: (i, 0)), const, sel, sel, const, const, const],
        out_specs=[out] * 4,
        out_shape=[jax.ShapeDtypeStruct((t, LANES), F32)] * 4,
        compiler_params=_params("parallel"),
        name="rope_tables",
    )(pos_packed, freq, sel_a, sel_b, sign_a, sign_b, rest_b)


def _l0_in_kernel(x_ref, g_ref, wqkv_ref, wlat_ref, gq_ref, wq_ref, gkv_ref, wkv_ref,
                  cosa_ref, sina_ref, cosb_ref, sinb_ref, g0_ref, g1_ref, g2_ref, qm_ref, kvm_ref, slabs):
    tm = x_ref.shape[0]
    hn = _rms(x_ref[...], g_ref[...]).astype(BF16)
    cos_a, sin_a = cosa_ref[...], sina_ref[...]
    cos_b, sin_b = cosb_ref[...], sinb_ref[...]
    half_a, half_b = HEAD_DIM // 2, MLA_ROPE // 2
    step = MXU_WIDTH
    halves = step // LANES
    group_refs = (g0_ref, g1_ref, g2_ref)
    slab = 0
    for gi, (_, dil) in enumerate(DSW_GROUPS):
        for part in range(3):
            c0 = (3 * gi + part) * DSW_GROUP_WIDTH
            acc = _dot(hn, wqkv_ref[:, c0:c0 + step])
            for h in range(halves):
                tile = acc[:, h * LANES:(h + 1) * LANES]
                if part < 2:
                    tile = _rope_tile(tile, cos_a, sin_a, half_a)
                if part == 0:
                    tile = tile * HEAD_DIM ** -0.5
                col = part * DSW_GROUP_WIDTH + h * LANES
                if dil == 1:
                    g0_ref[:, col:col + LANES] = tile.astype(BF16)
                else:
                    slabs[slab] = tile
                    for r in range(dil):
                        rows = slabs[slab, pl.ds(r, tm // dil, stride=dil), :]
                        group_refs[gi][r, :, col:col + LANES] = rows.astype(BF16)
                    slab += 1

    lat = _dot(hn, wlat_ref[...])
    cq = _rms(lat[:, :MLA_Q_RANK], gq_ref[...]).astype(BF16)
    ckv = _rms(lat[:, MLA_Q_RANK:MLA_Q_RANK + MLA_KV_RANK], gkv_ref[...]).astype(BF16)
    kpe = _rope_tile(lat[:, MLA_Q_RANK + MLA_KV_RANK:], cos_b, sin_b, half_b)
    for c0 in range(0, MLA_WIDTH, step):
        acc = _dot(cq, wq_ref[:, c0:c0 + step])
        for h in range(halves):
            sl = slice(h * LANES, (h + 1) * LANES)
            tile = _rope_tile(acc[:, sl], cos_b, sin_b, half_b) * MLA_Q_SCALE
            qm_ref[:, c0 + h * LANES:c0 + (h + 1) * LANES] = tile.astype(BF16)
    lane = lax.broadcasted_iota(jnp.int32, kpe.shape, 1)
    one_hot = jnp.where(lane == MLA_V, 1.0, 0.0)
    for c0 in range(0, 2 * MLA_WIDTH, step):
        acc = _dot(ckv, wkv_ref[:, c0:c0 + step])
        extra = kpe if c0 < MLA_WIDTH else one_hot
        for h in range(halves):
            sl = slice(h * LANES, (h + 1) * LANES)
            kvm_ref[:, c0 + h * LANES:c0 + (h + 1) * LANES] = (acc[:, sl] + extra).astype(BF16)


def _l0_in_proj(x, g, w_qkv, w_lat, gq, wq, gkv, wkv, cos_a, sin_a, cos_b, sin_b, bsz, seq, tm):
    t, d = x.shape
    tps = seq // tm
    gw = 3 * DSW_GROUP_WIDTH
    row = lambda w: pl.BlockSpec((tm, w), lambda i: (i, 0))

    def residue_major(dil):
        return pl.BlockSpec((None, dil, tm // dil, gw), lambda i: (i // tps, 0, i % tps, 0))

    dils = [dil for _, dil in DSW_GROUPS]
    n_slabs = sum(gw // LANES for dil in dils if dil > 1)
    return pl.pallas_call(
        _l0_in_kernel,
        grid=(t // tm,),
        in_specs=[row(d), _resident((1, d)), _resident(w_qkv.shape), _resident(w_lat.shape),
                  _resident(gq.shape), _resident(wq.shape), _resident(gkv.shape), _resident(wkv.shape),
                  row(LANES), row(LANES), row(LANES), row(LANES)],
        out_specs=[row(gw), residue_major(dils[1]), residue_major(dils[2]), row(MLA_WIDTH), row(2 * MLA_WIDTH)],
        out_shape=[jax.ShapeDtypeStruct((t, gw), BF16),
                   jax.ShapeDtypeStruct((bsz, dils[1], seq // dils[1], gw), BF16),
                   jax.ShapeDtypeStruct((bsz, dils[2], seq // dils[2], gw), BF16),
                   jax.ShapeDtypeStruct((t, MLA_WIDTH), BF16),
                   jax.ShapeDtypeStruct((t, 2 * MLA_WIDTH), BF16)],
        scratch_shapes=[pltpu.VMEM((n_slabs, tm, LANES), F32)],
        compiler_params=_params("parallel"),
        name="l0_in_proj",
    )(x, g, w_qkv, w_lat, gq, wq, gkv, wkv, cos_a, sin_a, cos_b, sin_b)


def _l1_in_kernel(x_ref, g_ref, wssm_ref, wsb_ref, proj_ref, qkv_ref):
    hn = _rms(x_ref[...], g_ref[...]).astype(BF16)
    step = 3 * LANES
    for c0 in range(0, SSM_PROJ_WIDTH, step):
        proj_ref[:, c0:c0 + step] = _dot(hn, wssm_ref[:, c0:c0 + step])
    step = MXU_WIDTH
    for c0 in range(0, 3 * SB_WIDTH, step):
        acc = _dot(hn, wsb_ref[:, c0:c0 + step])
        if c0 < SB_WIDTH:
            acc = acc * HEAD_DIM ** -0.5
        qkv_ref[:, c0:c0 + step] = acc.astype(BF16)


def _l1_in_proj(x, g, w_ssm, w_sb, tm):
    t, d = x.shape
    row = lambda w: pl.BlockSpec((tm, w), lambda i: (i, 0))
    return pl.pallas_call(
        _l1_in_kernel,
        grid=(t // tm,),
        in_specs=[row(d), _resident((1, d)), _resident(w_ssm.shape), _resident(w_sb.shape)],
        out_specs=[row(SSM_PROJ_WIDTH), row(3 * SB_WIDTH)],
        out_shape=[jax.ShapeDtypeStruct((t, SSM_PROJ_WIDTH), F32),
                   jax.ShapeDtypeStruct((t, 3 * SB_WIDTH), BF16)],
        compiler_params=_params("parallel"),
        name="l1_in_proj",
    )(x, g, w_ssm, w_sb)


def _dilated_kernel(c0_ref, p0_ref, c1_ref, p1_ref, c2_ref, p2_ref, y_ref, o_acc, l_acc):
    blk = DSW_BLK
    gw = DSW_GROUP_WIDTH
    pairs = gw // LANES
    qcols, kcols, vcols = slice(0, gw), slice(gw, 2 * gw), slice(2 * gw, 3 * gw)
    first_key = jnp.where(pl.program_id(1) > 0, 0, blk)
    qi = lax.broadcasted_iota(jnp.int32, (blk, 2 * blk), 0)
    kj = lax.broadcasted_iota(jnp.int32, (blk, 2 * blk), 1)
    window = (kj >= qi) & (kj <= qi + blk)
    lane = lax.broadcasted_iota(jnp.int32, (blk, LANES), 1)
    first = lane < HEAD_DIM

    def unit(q, k2, v2, lowest_key):
        valid = window & (kj >= lowest_key)
        res = []
        for pair in range(pairs):
            sl = slice(pair * LANES, (pair + 1) * LANES)
            kk, vv = k2[:, sl], v2[:, sl]
            outs, lses = [], []
            for hh in range(2):
                s = jnp.where(valid, _dot_nt(_keep_head(q[:, sl], lane, hh), kk), NEG_BIG)
                m = jnp.max(s, axis=-1, keepdims=True)
                p = jnp.exp(s - m)
                l = jnp.sum(p, axis=-1, keepdims=True)
                outs.append(_dot((p * (1.0 / l)).astype(BF16), vv))
                lses.append(jnp.broadcast_to(m + jnp.log(l), (blk, LANES)))
            res.append((jnp.where(first, outs[0], outs[1]), jnp.where(first, lses[0], lses[1])))
        return res

    def put(rows, res):
        for pair, (o, lse) in enumerate(res):
            o_acc[pair, rows, :] = o
            l_acc[pair, rows, :] = lse

    def merge(rows, res):
        for pair, (o, lse) in enumerate(res):
            o_old, l_old = o_acc[pair, rows, :], l_acc[pair, rows, :]
            top = jnp.maximum(l_old, lse)
            l_new = top + jnp.log(jnp.exp(l_old - top) + jnp.exp(lse - top))
            o_acc[pair, rows, :] = o_old * jnp.exp(l_old - l_new) + o * jnp.exp(lse - l_new)
            l_acc[pair, rows, :] = l_new

    def stacked(prev, cur):
        return jnp.concatenate([prev, cur], axis=0)

    head = slice(0, blk)

    def head_unit(cur, prev):
        return unit(cur(head, qcols), stacked(prev(kcols), cur(head, kcols)),
                    stacked(prev(vcols), cur(head, vcols)), first_key)

    def later_unit(cur, r0):
        keys = pl.ds(r0 - blk, 2 * blk)
        return unit(cur(pl.ds(r0, blk), qcols), cur(keys, kcols), cur(keys, vcols), 0)

    units_per_body = DSW_UNITS_PER_BODY

    cur0 = lambda rows, cols: c0_ref[rows, cols]
    put(head, head_unit(cur0, lambda cols: p0_ref[:, cols]))
    later_blocks = c0_ref.shape[0] // blk - 1
    group0 = 3
    assert later_blocks % group0 == 0

    def dense_blocks(u, carry):
        for k in range(group0):
            r0 = pl.multiple_of((1 + u * group0 + k) * blk, blk)
            put(pl.ds(r0, blk), later_unit(cur0, r0))
        return carry

    lax.fori_loop(0, later_blocks // group0, dense_blocks, 0)

    for cur_ref, prev_ref in ((c1_ref, p1_ref), (c2_ref, p2_ref)):
        dil, blocks = cur_ref.shape[0], cur_ref.shape[1] // blk
        residues_per_body = max(1, units_per_body // blocks)
        assert dil % residues_per_body == 0

        def residues(u, carry, cur_ref=cur_ref, prev_ref=prev_ref, dil=dil, blocks=blocks,
                     residues_per_body=residues_per_body):
            for k in range(residues_per_body):
                r = u * residues_per_body + k
                cur = lambda rows, cols, r=r: cur_ref[r, rows, cols]
                merge(pl.ds(r, blk, stride=dil), head_unit(cur, lambda cols, r=r: prev_ref[r, :, cols]))
                for n in range(1, blocks):
                    merge(pl.ds(n * blk * dil + r, blk, stride=dil), later_unit(cur, n * blk))
            return carry

        lax.fori_loop(0, dil // residues_per_body, residues, 0)

    for pair in range(pairs):
        y_ref[:, pair * LANES:(pair + 1) * LANES] = o_acc[pair].astype(y_ref.dtype)


def _dilated_attention(g0, g1, g2, bsz, seq):
    blk = DSW_BLK
    span = blk * max(dil for _, dil in DSW_GROUPS)
    nspan = seq // span
    gw = 3 * DSW_GROUP_WIDTH
    d1, d2 = g1.shape[1], g2.shape[1]
    blocks_per_span0 = span // blk

    def residue_major(dil, rows, prev):
        per_span = span // dil // rows
        if prev:
            return pl.BlockSpec((None, dil, rows, gw), lambda b, s: (b, 0, jnp.maximum(s * per_span - 1, 0), 0))
        return pl.BlockSpec((None, dil, rows, gw), lambda b, s: (b, 0, s, 0))

    return pl.pallas_call(
        _dilated_kernel,
        grid=(bsz, nspan),
        in_specs=[
            pl.BlockSpec((span, gw), lambda b, s: (b * nspan + s, 0)),
            pl.BlockSpec((blk, gw), lambda b, s: (jnp.maximum((b * nspan + s) * blocks_per_span0 - 1, 0), 0)),
            residue_major(d1, span // d1, False), residue_major(d1, blk, True),
            residue_major(d2, span // d2, False), residue_major(d2, blk, True),
        ],
        out_specs=pl.BlockSpec((span, DSW_GROUP_WIDTH), lambda b, s: (b * nspan + s, 0)),
        out_shape=jax.ShapeDtypeStruct((bsz * seq, DSW_GROUP_WIDTH), BF16),
        scratch_shapes=[pltpu.VMEM((DSW_GROUP_WIDTH // LANES, span, LANES), F32)] * 2,
        compiler_params=_params("parallel", "arbitrary"),
        name="dilated_attention",
    )(g0, g0, g1, g1, g2, g2)


def _mla_attn_kernel(q_ref, k_ref, v_ref, o_ref, *, tq, tk):
    i = pl.program_id(2)
    assert (tq // tk) % 2 == 0
    lane = lax.broadcasted_iota(jnp.int32, (tq, LANES), 1)
    slots = [slice(hh * MLA_SLOT, (hh + 1) * MLA_SLOT) for hh in range(2)]

    def update(q, m, acc, start, width, slot, visible=None):
        s = _dot_nt(q, k_ref[pl.ds(start, width), slot])
        if visible is not None:
            s = jnp.where(visible, s, NEG_BIG)
        m_new = jnp.maximum(m, jnp.max(s, axis=-1, keepdims=True))
        p = jnp.exp2(s - m_new)
        return m_new, jnp.exp2(m - m_new) * acc + _dot(p.astype(BF16), v_ref[pl.ds(start, width), slot])

    def block(j, carry, visible=None):
        start = pl.multiple_of(j * tk, tk)
        return tuple(update(q_ref[:, slots[hh]], *carry[hh], start, tk, slots[hh], visible) for hh in range(2))

    def block_pair(jp, carry):
        return block(2 * jp + 1, block(2 * jp, carry))

    init = tuple((jnp.full((tq, 1), NEG_BIG, F32), jnp.zeros((tq, LANES), F32)) for _ in range(2))
    diag_blocks = tq // tk
    carry = lax.fori_loop(0, i * (diag_blocks // 2), block_pair, init)
    row = lax.broadcasted_iota(jnp.int32, (tq, tk), 0)
    col = lax.broadcasted_iota(jnp.int32, (tq, tk), 1)
    for b in range(diag_blocks):
        carry = block(i * diag_blocks + b, carry, col + b * tk <= row)
    (_, acc0), (_, acc1) = carry
    o0 = acc0 * (1.0 / acc0[:, MLA_V:MLA_V + 1])
    o1 = acc1 * (1.0 / acc1[:, MLA_V:MLA_V + 1])
    o_ref[...] = jnp.where(lane < MLA_V, o0, pltpu.roll(o1, MLA_V, 1)).astype(o_ref.dtype)


def _mla_attention(q, kv, bsz, seq, tq, tk):
    nq = seq // tq
    pairs = MLA_HEADS // 2
    kern = functools.partial(_mla_attn_kernel, tq=tq, tk=tk)
    return pl.pallas_call(
        kern,
        grid=(bsz, pairs, nq),
        in_specs=[
            pl.BlockSpec((tq, 2 * MLA_SLOT), lambda b, p, i: (b * nq + i, p)),
            pl.BlockSpec((seq, 2 * MLA_SLOT), lambda b, p, i: (b, p)),
            pl.BlockSpec((seq, 2 * MLA_SLOT), lambda b, p, i: (b, pairs + p)),
        ],
        out_specs=pl.BlockSpec((tq, 2 * MLA_V), lambda b, p, i: (b * nq + i, p)),
        out_shape=jax.ShapeDtypeStruct((bsz * seq, MLA_HEADS * MLA_V), BF16),
        compiler_params=_params("parallel", "parallel", "arbitrary"),
        name="mla_attention",
    )(q, kv, kv)


def _sb_attn_kernel(q_ref, k_ref, v_ref, o_ref, *, tq, tiles):
    first = pl.program_id(2) * tiles
    lane = lax.broadcasted_iota(jnp.int32, (tq, LANES), 1)
    row = lax.broadcasted_iota(jnp.int32, (tq, tq), 0)
    col = lax.broadcasted_iota(jnp.int32, (tq, tq), 1)
    later = jnp.where(row > col, 1.0, 0.0).astype(BF16)
    strict = jnp.concatenate([col < row] * 2, axis=0)
    qs = [jnp.concatenate([_keep_head(q_ref[t * tq:(t + 1) * tq, :], lane, hh) for hh in range(2)], axis=0)
          for t in range(tiles)]

    def block(q2, blk, run, acc, masked):
        start = pl.multiple_of(blk * tq, tq)
        z = _dot_nt(q2, k_ref[pl.ds(start, tq), :])
        log_beta = jnp.minimum(z, 0.0) - jnp.log(1.0 + jnp.exp(-jnp.abs(z)))
        log_stay = log_beta - z
        if masked:
            log_stay = jnp.where(strict, log_stay, 0.0)
        parts = _dot(jnp.concatenate(_split2(log_stay), axis=0), later)
        after = parts[:2 * tq] + parts[2 * tq:] + run
        w = jnp.exp(log_beta + after)
        if masked:
            w = jnp.where(strict, w, 0.0)
        acc = acc + _dot(w.astype(BF16), v_ref[pl.ds(start, tq), :])
        return run + jnp.sum(log_stay, axis=-1, keepdims=True), acc

    run0 = jnp.zeros((2 * tq, 1), F32)
    acc0 = jnp.zeros((2 * tq, LANES), F32)

    def opening(first_has_left):
        states = []
        for t in range(tiles):
            run, acc = block(qs[t], first + t, run0, acc0, True)
            if t > 0 or first_has_left:
                run, acc = block(qs[t], first + t - 1, run, acc, False)
            states.append((run, acc))
        return tuple(states)

    states = lax.cond(first > 0, functools.partial(opening, True), functools.partial(opening, False))

    for t in range(tiles):

        def cond(c):
            left, run, _ = c
            return (left > 0) & (jnp.max(run) >= -SB_LOG_UNDERFLOW)

        def body(c, t=t):
            left, run, acc = c
            run, acc = block(qs[t], left - 1, run, acc, False)
            return left - 1, run, acc

        _, _, acc = lax.while_loop(cond, body, (jnp.maximum(first + t - 1, 0),) + states[t])
        o_ref[t * tq:(t + 1) * tq, :] = jnp.where(lane < HEAD_DIM, acc[:tq], acc[tq:]).astype(o_ref.dtype)


def _sb_attention(qkv, bsz, seq, tq, tiles):
    nq = seq // (tq * tiles)
    pairs = SB_WIDTH // LANES
    kern = functools.partial(_sb_attn_kernel, tq=tq, tiles=tiles)
    return pl.pallas_call(
        kern,
        grid=(bsz, pairs, nq),
        in_specs=[
            pl.BlockSpec((tq * tiles, LANES), lambda b, p, i: (b * nq + i, p)),
            pl.BlockSpec((seq, LANES), lambda b, p, i: (b, pairs + p)),
            pl.BlockSpec((seq, LANES), lambda b, p, i: (b, 2 * pairs + p)),
        ],
        out_specs=pl.BlockSpec((tq * tiles, LANES), lambda b, p, i: (b * nq + i, p)),
        out_shape=jax.ShapeDtypeStruct((bsz * seq, SB_WIDTH), BF16),
        compiler_params=_params("parallel", "parallel", "arbitrary"),
        name="stickbreak_attention",
    )(qkv, qkv, qkv)


def _ssd_kernel(z_ref, xs_ref, bc_ref, dt_ref, cwx_ref, cbx_ref, cwb_ref, cbb_ref, dtb_ref, alog_ref, dskip_ref,
                gn_ref, y_ref, xtail, btail, state):
    c = pl.program_id(1)
    cl = SSM_CHUNK
    halo = SUBLANES
    assert SSM_CONV == 4

    @pl.when(c == 0)
    def _():
        xtail[...] = jnp.zeros_like(xtail)
        btail[...] = jnp.zeros_like(btail)
        state[...] = jnp.zeros_like(state)

    def shift_rows(x, tail, k):
        rolled = pltpu.roll(x, k, 0)
        wrapped = pltpu.roll(tail, k, 0)
        row = lax.broadcasted_iota(jnp.int32, tail.shape, 0)
        return jnp.concatenate([jnp.where(row < k, wrapped, rolled[:halo]), rolled[halo:]], axis=0)

    def conv_silu(tails, raw_ref, w_ref, b_ref):
        x = raw_ref[...]
        x1 = shift_rows(x, tails[0], 1)
        u = w_ref[1:2, :] * x + w_ref[0:1, :] * x1
        y = b_ref[...] + w_ref[3:4, :] * x + w_ref[2:3, :] * x1 + shift_rows(u, tails[1], 2)
        tails[0] = x[cl - halo:, :]
        tails[1] = u[cl - halo:, :]
        return _silu(y)

    xs = conv_silu(xtail, xs_ref, cwx_ref, cbx_ref)
    bc = conv_silu(btail, bc_ref, cwb_ref, cbb_ref)

    dt = _softplus(dt_ref[...] + dtb_ref[...])
    da = dt * (-jnp.exp(alog_ref[...]))
    row = lax.broadcasted_iota(jnp.int32, (cl, cl), 0)
    col = lax.broadcasted_iota(jnp.int32, (cl, cl), 1)
    causal = col <= row
    tri = jnp.where(causal, 1.0, 0.0).astype(BF16)
    d_hi, d_mid, d_lo = _split3(da)
    cs = _dot(tri, d_hi) + _dot(tri, d_mid) + _dot(tri, d_lo)
    cs_t = cs.T
    ecs = jnp.exp(cs)
    dec_end = jnp.exp(cs[cl - 1:cl, :] - cs)

    head_of_lane = jnp.right_shift(lax.broadcasted_iota(jnp.int32, (LANES, SSM_INNER), 1), 6)
    expand = jnp.where(lax.broadcasted_iota(jnp.int32, (LANES, SSM_INNER), 0) == head_of_lane, 1.0, 0.0).astype(BF16)

    def per_head_lanes(v):
        a, b, c3 = _split3(v)
        return _dot(a, expand) + _dot(b, expand) + _dot(c3, expand)

    dt_e = per_head_lanes(dt)
    ecs_e = per_head_lanes(ecs)
    dec_end_e = per_head_lanes(dec_end)

    xdt = xs * dt_e
    xdt_b = xdt.astype(BF16)
    xw_b = (xdt * dec_end_e).astype(BF16)
    lane = lax.broadcasted_iota(jnp.int32, (cl, LANES), 1)
    first = lane < SSM_HEADDIM
    gs = SSM_STATE
    heads_per_group = SSM_HEADS // SSM_GROUPS
    for g in range(SSM_GROUPS):
        bg = bc[:, g * gs:(g + 1) * gs]
        cg_b = bc[:, (SSM_GROUPS + g) * gs:(SSM_GROUPS + g + 1) * gs].astype(BF16)
        cb = _dot_nt(cg_b, bg.astype(BF16))
        cols = slice(g * SSM_GROUP_WIDTH, (g + 1) * SSM_GROUP_WIDTH)
        prev = state[:, cols]
        y_off = _dot(cg_b, prev.astype(BF16)) * ecs_e[:, cols]
        state[:, cols] = prev * ecs_e[cl - 1:cl, cols] + _dot(bg.T.astype(BF16), xw_b[:, cols])
        for pr in range(heads_per_group // 2):
            pcols = slice(g * SSM_GROUP_WIDTH + pr * LANES, g * SSM_GROUP_WIDTH + (pr + 1) * LANES)
            x_pair = xdt_b[:, pcols]
            ys = []
            for hh in range(2):
                h = g * heads_per_group + 2 * pr + hh
                seg = jnp.where(causal, cs[:, h:h + 1] - cs_t[h:h + 1, :], NEG_BIG)
                ys.append(_dot((cb * jnp.exp(seg)).astype(BF16), x_pair))
            y_diag = jnp.where(first, ys[0], ys[1])
            y_pair = y_diag + y_off[:, pr * LANES:(pr + 1) * LANES] + xs[:, pcols] * dskip_ref[:, pcols]
            y_ref[:, pcols] = y_pair * _silu(z_ref[:, pcols])
    for g in range(SSM_GROUPS):
        cols = slice(g * SSM_GROUP_WIDTH, (g + 1) * SSM_GROUP_WIDTH)
        y_ref[:, cols] = _rms(y_ref[:, cols], gn_ref[:, cols])


def _ssd(proj, bsz, seq, cwx, cbx, cwb, cbb, dt_bias, a_log, d_skip, gnorm):
    cl = SSM_CHUNK
    nc = seq // cl

    def rows(width, colblk):
        return pl.BlockSpec((cl, width), lambda b, c: (b * nc + c, colblk))

    def const(r, width):
        return pl.BlockSpec((r, width), lambda b, c: (0, 0))

    return pl.pallas_call(
        _ssd_kernel,
        grid=(bsz, nc),
        in_specs=[
            rows(SSM_INNER, 0),
            rows(SSM_INNER, 1),
            rows(SSM_BC_WIDTH, 2 * SSM_INNER // SSM_BC_WIDTH),
            rows(LANES, (2 * SSM_INNER + SSM_BC_WIDTH) // LANES),
            const(SSM_CONV, SSM_INNER), const(1, SSM_INNER), const(SSM_CONV, SSM_BC_WIDTH), const(1, SSM_BC_WIDTH),
            const(1, LANES), const(1, LANES), const(1, SSM_INNER), const(1, SSM_INNER),
        ],
        out_specs=pl.BlockSpec((cl, SSM_INNER), lambda b, c: (b * nc + c, 0)),
        out_shape=jax.ShapeDtypeStruct((bsz * seq, SSM_INNER), F32),
        scratch_shapes=[
            pltpu.VMEM((2, SUBLANES, SSM_INNER), F32),
            pltpu.VMEM((2, SUBLANES, SSM_BC_WIDTH), F32),
            pltpu.VMEM((SSM_STATE, SSM_INNER), F32),
        ],
        compiler_params=_params("parallel", "arbitrary"),
        name="ssd_scan",
    )(proj, proj, proj, proj, cwx, cbx, cwb, cbb, dt_bias, a_log, d_skip, gnorm)


def _mix_ffn_kernel(*refs, tm, tiles_per_seq, final_norm):
    x_ref, a_ref, b_ref, wa_ref, wb_ref, g_ref, wg_ref, wu_ref, cw_ref, cb_ref, wd_ref = refs[:11]
    pos = 11
    fn_ref = None
    if final_norm:
        fn_ref = refs[pos]
        pos += 1
    o_ref, gbuf, act_ref = refs[pos:pos + 3]
    halo = SUBLANES

    o_ref[...] = x_ref[...] + _dot(a_ref[...].astype(BF16), wa_ref[...]) + _dot(b_ref[...].astype(BF16), wb_ref[...])
    hn = _rms(o_ref[...], g_ref[...]).astype(BF16)

    @pl.when(pl.program_id(0) % tiles_per_seq == 0)
    def _():
        gbuf[0:halo, :] = jnp.zeros((halo, gbuf.shape[1]), F32)

    step = MXU_WIDTH
    for c0 in range(0, FFN_DIM, step):
        cols = slice(c0, c0 + step)
        gate = _dot(hn, wg_ref[:, cols])
        up = _dot(hn, wu_ref[:, cols])
        gbuf[halo:halo + tm, cols] = gate
        conv = cb_ref[:, cols] + cw_ref[FFN_CONV - 1:FFN_CONV, cols] * gate
        for t in range(FFN_CONV - 1):
            back = FFN_CONV - 1 - t
            conv = conv + cw_ref[t:t + 1, cols] * gbuf[halo - back:halo - back + tm, cols]
        gbuf[0:halo, cols] = gbuf[tm:tm + halo, cols]
        act_ref[:, cols] = (_silu(conv) * up).astype(BF16)

    y = o_ref[...] + _dot(act_ref[...], wd_ref[...])
    if final_norm:
        y = _rms(y, fn_ref[...])
    o_ref[...] = y


def _mix_ffn(x, a, b, wa, wb, g, wg, wu, cw, cb, wd, seq, *, tm, final_gain=None):
    t, d = x.shape
    f = wg.shape[1]
    final_norm = final_gain is not None
    row = lambda w: pl.BlockSpec((tm, w), lambda i: (i, 0))
    in_specs = [row(d), row(a.shape[1]), row(b.shape[1]), _resident(wa.shape), _resident(wb.shape),
                _resident(g.shape), _resident(wg.shape), _resident(wu.shape), _resident(cw.shape),
                _resident(cb.shape), _resident(wd.shape)]
    args = [x, a, b, wa, wb, g, wg, wu, cw, cb, wd]
    if final_norm:
        in_specs.append(_resident(final_gain.shape))
        args.append(final_gain)
    kern = functools.partial(_mix_ffn_kernel, tm=tm, tiles_per_seq=seq // tm, final_norm=final_norm)
    return pl.pallas_call(
        kern,
        grid=(t // tm,),
        in_specs=in_specs,
        out_specs=row(d),
        out_shape=jax.ShapeDtypeStruct((t, d), F32),
        scratch_shapes=[pltpu.VMEM((tm + SUBLANES, f), F32), pltpu.VMEM((tm, f), BF16)],
        compiler_params=_params("arbitrary"),
        name="mix_ffn",
    )(*args)


def _row(v):
    return v.reshape(1, -1).astype(F32)


def _pad_cols(w, n):
    return jnp.pad(w, ((0, 0), (0, n - w.shape[1])))


def _rope_lane_constants():
    lane = jnp.arange(LANES)
    half_a = HEAD_DIM // 2
    inv_a = 1.0 / (ROPE_THETA ** (jnp.arange(half_a, dtype=F32) * (2.0 / HEAD_DIM)))
    freq_a = inv_a[lane % half_a]
    sign_a = jnp.where((lane % HEAD_DIM) < half_a, -1.0, 1.0).astype(F32)
    half_b = MLA_ROPE // 2
    assert HEAD_DIM == 2 * MLA_ROPE
    in_rope = (lane >= MLA_NOPE) & (lane < MLA_NOPE + MLA_ROPE)
    sign_b = jnp.where(in_rope, jnp.where((lane - MLA_NOPE) < half_b, -1.0, 1.0), 0.0).astype(F32)
    rest_b = jnp.where(in_rope, 0.0, 1.0).astype(F32)
    src = jnp.arange(LANES)[:, None]
    packed_tok = jnp.arange(ROPE_PACK)[:, None, None]
    sel_a = (src == packed_tok * half_a + lane % half_a).astype(BF16)
    sel_b = (in_rope & (src == packed_tok * half_a + 2 * ((lane - MLA_NOPE) % half_b))).astype(BF16)
    return _row(freq_a), sel_a, sel_b, _row(sign_a), _row(sign_b), _row(rest_b)


def _mla_slot_weights(w_uq, w_ukv):
    rq = w_uq.shape[0]
    wq = w_uq.reshape(rq, MLA_HEADS, MLA_NOPE + MLA_ROPE)
    wq = jnp.pad(wq, ((0, 0), (0, 0), (0, MLA_SLOT - MLA_NOPE - MLA_ROPE))).reshape(rq, MLA_WIDTH)
    rk = w_ukv.shape[0]
    wkv = w_ukv.reshape(rk, MLA_HEADS, MLA_NOPE + MLA_V)
    wk = jnp.pad(wkv[:, :, :MLA_NOPE], ((0, 0), (0, 0), (0, MLA_SLOT - MLA_NOPE))).reshape(rk, MLA_WIDTH)
    wv = jnp.pad(wkv[:, :, MLA_NOPE:], ((0, 0), (0, 0), (0, MLA_SLOT - MLA_V))).reshape(rk, MLA_WIDTH)
    return wq.astype(BF16), jnp.concatenate([wk, wv], axis=1).astype(BF16)


def kernel(x, positions,
           l0_norm_mix, l0_w_in, l0_mla_q_norm, l0_mla_w_uq, l0_mla_kv_norm, l0_mla_w_ukv, l0_w_out,
           l0_norm_ffn, l0_ffn_w_gate, l0_ffn_w_up, l0_ffn_conv_w, l0_ffn_conv_b, l0_ffn_w_down,
           l1_norm_mix, l1_w_in, l1_ssm_conv_w, l1_ssm_conv_b, l1_ssm_dt_bias, l1_ssm_a_log, l1_ssm_d,
           l1_ssm_norm, l1_w_out,
           l1_norm_ffn, l1_ffn_w_gate, l1_ffn_w_up, l1_ffn_conv_w, l1_ffn_conv_b, l1_ffn_w_down,
           final_norm):
    bsz, seq, d = x.shape
    t = bsz * seq
    xf = x.reshape(t, d)
    tm = 512

    pos_packed = jnp.repeat(positions.reshape(t // ROPE_PACK, ROPE_PACK).astype(F32), LANES // ROPE_PACK, axis=1)
    cos_a, sin_a, cos_b, sin_b = _rope_tables(pos_packed, *_rope_lane_constants(), rows=tm)

    nd = DSW_WIDTH
    gw = DSW_GROUP_WIDTH
    w_qkv = jnp.concatenate([l0_w_in[:, part * nd + gi * gw:part * nd + (gi + 1) * gw]
                             for gi in range(len(DSW_GROUPS)) for part in range(3)], axis=1).astype(BF16)
    w_cq = l0_w_in[:, 3 * nd:3 * nd + MLA_Q_RANK]
    w_ckv = l0_w_in[:, 3 * nd + MLA_Q_RANK:3 * nd + MLA_Q_RANK + MLA_KV_RANK]
    w_kpe = l0_w_in[:, 3 * nd + MLA_Q_RANK + MLA_KV_RANK:]
    w_kpe_slot = jnp.pad(w_kpe, ((0, 0), (MLA_NOPE, MLA_SLOT - MLA_NOPE - MLA_ROPE)))
    w_lat = jnp.concatenate([w_cq, w_ckv, w_kpe_slot], axis=1).astype(BF16)
    wq_slot, wkv_slot = _mla_slot_weights(l0_mla_w_uq, l0_mla_w_ukv)
    g0, g1, g2, q_mla, kv_mla = _l0_in_proj(xf, _row(l0_norm_mix), w_qkv, w_lat, _row(l0_mla_q_norm), wq_slot,
                                            _row(l0_mla_kv_norm), wkv_slot, cos_a, sin_a, cos_b, sin_b,
                                            bsz, seq, tm)
    y_a = _dilated_attention(g0, g1, g2, bsz, seq)
    y_b = _mla_attention(q_mla, kv_mla, bsz, seq, tq=1024, tk=512)

    w_out0 = l0_w_out.astype(BF16)
    x2 = _mix_ffn(xf, y_a, y_b, w_out0[:DSW_GROUP_WIDTH], w_out0[DSW_GROUP_WIDTH:], _row(l0_norm_ffn),
                  l0_ffn_w_gate.astype(BF16), l0_ffn_w_up.astype(BF16), l0_ffn_conv_w, _row(l0_ffn_conv_b),
                  l0_ffn_w_down.astype(BF16), seq, tm=tm)

    o_dt = 2 * SSM_INNER + SSM_BC_WIDTH
    w_ssm = _pad_cols(l1_w_in[:, :o_dt + SSM_HEADS], SSM_PROJ_WIDTH).astype(BF16)
    w_sb = l1_w_in[:, o_dt + SSM_HEADS:].astype(BF16)
    proj, qkv_sb = _l1_in_proj(x2, _row(l1_norm_mix), w_ssm, w_sb, tm)

    cw = l1_ssm_conv_w
    cb = _row(l1_ssm_conv_b)
    y_c = _ssd(proj, bsz, seq, cw[:, :SSM_INNER], cb[:, :SSM_INNER], cw[:, SSM_INNER:], cb[:, SSM_INNER:],
               _pad_cols(_row(l1_ssm_dt_bias), LANES), _pad_cols(_row(l1_ssm_a_log), LANES),
               _row(jnp.repeat(l1_ssm_d, SSM_HEADDIM)), _row(l1_ssm_norm))
    y_d = _sb_attention(qkv_sb, bsz, seq, tq=256, tiles=2)

    w_out1 = l1_w_out.astype(BF16)
    out = _mix_ffn(x2, y_c, y_d, w_out1[:SSM_INNER], w_out1[SSM_INNER:], _row(l1_norm_ffn),
                   l1_ffn_w_gate.astype(BF16), l1_ffn_w_up.astype(BF16), l1_ffn_conv_w, _row(l1_ffn_conv_b),
                   l1_ffn_w_down.astype(BF16), seq, tm=tm, final_gain=_row(final_norm))
    return out.reshape(bsz, seq, d)
```

```python
import functools
import math

import jax
import jax.numpy as jnp
from jax import lax
from jax.experimental import pallas as pl
from jax.experimental.pallas import tpu as pltpu

F32 = jnp.float32
BF16 = jnp.bfloat16

LANES = 128
SUBLANES = 8
MXU_WIDTH = 256
VMEM_LIMIT_BYTES = 56 * 1024 * 1024

D_MODEL = 1024
HEAD_DIM = 64
ROPE_THETA = 10000.0
ROPE_PACK = 4
NORM_EPS = 1e-6

DSW_GROUPS = ((128, 1), (512, 4), (2048, 16))
DSW_HEADS_PER_GROUP = 4
DSW_HEADS = DSW_HEADS_PER_GROUP * len(DSW_GROUPS)
DSW_BLK = 128
DSW_WIDTH = DSW_HEADS * HEAD_DIM
DSW_GROUP_WIDTH = DSW_HEADS_PER_GROUP * HEAD_DIM
DSW_UNITS_PER_BODY = 4

MLA_HEADS = 12
MLA_Q_RANK = 256
MLA_KV_RANK = 128
MLA_NOPE = 64
MLA_ROPE = 32
MLA_V = 64
MLA_SLOT = 128
MLA_WIDTH = MLA_HEADS * MLA_SLOT
MLA_LAT_WIDTH =MLA_Q_RANK + MLA_KV_RANK + MLA_SLOT
MLA_Q_SCALE = (MLA_NOPE + MLA_ROPE) ** -0.5 * math.log2(math.e)

SSM_INNER = 1024
SSM_HEADDIM = 64
SSM_HEADS = 16
SSM_STATE = 128
SSM_GROUPS = 2
SSM_CONV = 4
SSM_CHUNK = 128
SSM_GROUP_WIDTH = SSM_INNER // SSM_GROUPS
SSM_BC_WIDTH = 2 * SSM_GROUPS * SSM_STATE
SSM_DT_PAD = LANES
SSM_PROJ_WIDTH = 2 * SSM_INNER + SSM_BC_WIDTH + SSM_DT_PAD

SB_HEADS = 8
SB_WIDTH = SB_HEADS * HEAD_DIM
SB_LOG_UNDERFLOW = 104.0

FFN_DIM = 2816
FFN_CONV = 3

NEG_BIG = -1e30


def _params(*sem):
    return pltpu.CompilerParams(dimension_semantics=sem, vmem_limit_bytes=VMEM_LIMIT_BYTES)


def _rms(x, g):
    return x * lax.rsqrt(jnp.mean(x * x, axis=-1, keepdims=True) + NORM_EPS) * g


def _silu(x):
    return x * (1.0 / (1.0 + jnp.exp(-x)))


def _log1p(e):
    u = 1.0 + e
    return jnp.where(u == 1.0, e, jnp.log(u) * (e / (u - 1.0)))


def _softplus(x):
    return jnp.maximum(x, 0.0) + _log1p(jnp.exp(-jnp.abs(x)))


def _swap_halves(x, half):
    lane = lax.broadcasted_iota(jnp.int32, x.shape, 1)
    up = pltpu.roll(x, LANES - half, 1)
    down = pltpu.roll(x, half, 1)
    return jnp.where((lane & half) == 0, up, down)


def _rope_tile(x, cos, sin, half):
    return x * cos + _swap_halves(x, half) * sin


def _split2(x):
    hi = x.astype(BF16)
    return hi, (x - hi.astype(F32)).astype(BF16)


def _split3(x):
    hi = x.astype(BF16)
    r1 = x - hi.astype(F32)
    mid = r1.astype(BF16)
    lo = (r1 - mid.astype(F32)).astype(BF16)
    return hi, mid, lo


def _dot(a, b):
    return jnp.dot(a, b, preferred_element_type=F32)


def _dot_nt(a, b):
    return lax.dot_general(a, b, (((1,), (1,)), ((), ())), preferred_element_type=F32)


def _keep_head(x2, lane, second):
    mine = (lane >= HEAD_DIM) if second else (lane < HEAD_DIM)
    return jnp.where(mine, x2.astype(F32), 0.0).astype(BF16)


def _resident(shape):
    return pl.BlockSpec(shape, lambda *_: (0,) * len(shape), pipeline_mode=pl.Buffered(1))


def _rope_table_kernel(pos_ref, freq_ref, sela_ref, selb_ref, signa_ref, signb_ref, restb_ref,
                       cosa_ref, sina_ref, cosb_ref, sinb_ref):
    ang = pos_ref[...] * freq_ref[...]
    cos3 = jnp.concatenate(_split3(jnp.cos(ang)), axis=1)
    sin3 = jnp.concatenate(_split3(jnp.sin(ang)), axis=1)
    rows = ang.shape[0]

    def spread(parts, sel):
        return _dot(parts, jnp.concatenate([sel] * 3, axis=0))

    for g in range(ROPE_PACK):
        tokens = pl.ds(g, rows, stride=ROPE_PACK)
        cosa_ref[tokens, :] = spread(cos3, sela_ref[g])
        sina_ref[tokens, :] = spread(sin3, sela_ref[g]) * signa_ref[...]
        cosb_ref[tokens, :] = spread(cos3, selb_ref[g]) + restb_ref[...]
        sinb_ref[tokens, :] = spread(sin3, selb_ref[g]) * signb_ref[...]


def _rope_tables(pos_packed, freq, sel_a, sel_b, sign_a, sign_b, rest_b, rows):
    packed = pos_packed.shape[0]
    t = packed * ROPE_PACK
    const = pl.BlockSpec((1, LANES), lambda i: (0, 0))
    sel = pl.BlockSpec((ROPE_PACK, LANES, LANES), lambda i: (0, 0, 0))
    out = pl.BlockSpec((rows * ROPE_PACK, LANES), lambda i: (i, 0))
    return pl.pallas_call(
        _rope_table_kernel,
        grid=(packed // rows,),
        in_specs=[pl.BlockSpec((rows, LANES), lambda i: (i, 0)), const, sel, sel, const, const, const],
        out_specs=[out] * 4,
        out_shape=[jax.ShapeDtypeStruct((t, LANES), F32)] * 4,
        compiler_params=_params("parallel"),
        name="rope_tables",
    )(pos_packed, freq, sel_a, sel_b, sign_a, sign_b, rest_b)


def _l0_in_kernel(x_ref, g_ref, wqkv_ref, wlat_ref, gq_ref, wq_ref, gkv_ref, wkv_ref,
                  cosa_ref, sina_ref, cosb_ref, sinb_ref, g0_ref, g1_ref, g2_ref, qm_ref, kvm_ref, slabs):
    tm = x_ref.shape[0]
    hn = _rms(x_ref[...], g_ref[...]).astype(BF16)
    cos_a, sin_a = cosa_ref[...], sina_ref[...]
    cos_b, sin_b = cosb_ref[...], sinb_ref[...]
    half_a, half_b = HEAD_DIM // 2, MLA_ROPE // 2
    step = MXU_WIDTH
    halves = step // LANES
    group_refs = (g0_ref, g1_ref, g2_ref)
    slab = 0
    for gi, (_, dil) in enumerate(DSW_GROUPS):
        for part in range(3):
            c0 = (3 * gi + part) * DSW_GROUP_WIDTH
            acc = _dot(hn, wqkv_ref[:, c0:c0 + step])
            for h in range(halves):
                tile = acc[:, h * LANES:(h + 1) * LANES]
                if part < 2:
                    tile = _rope_tile(tile, cos_a, sin_a, half_a)
                if part == 0:
                    tile = tile * HEAD_DIM ** -0.5
                col = part * DSW_GROUP_WIDTH + h * LANES
                if dil == 1:
                    g0_ref[:, col:col + LANES] = tile.astype(BF16)
                else:
                    slabs[slab] = tile
                    for r in range(dil):
                        rows = slabs[slab, pl.ds(r, tm // dil, stride=dil), :]
                        group_refs[gi][r, :, col:col + LANES] = rows.astype(BF16)
                    slab += 1

    lat = _dot(hn, wlat_ref[...])
    cq = _rms(lat[:, :MLA_Q_RANK], gq_ref[...]).astype(BF16)
    ckv = _rms(lat[:, MLA_Q_RANK:MLA_Q_RANK + MLA_KV_RANK], gkv_ref[...]).astype(BF16)
    kpe = _rope_tile(lat[:, MLA_Q_RANK + MLA_KV_RANK:], cos_b, sin_b, half_b)
    for c0 in range(0, MLA_WIDTH, step):
        acc = _dot(cq, wq_ref[:, c0:c0 + step])
        for h in range(halves):
            sl = slice(h * LANES, (h + 1) * LANES)
            tile = _rope_tile(acc[:, sl], cos_b, sin_b, half_b) * MLA_Q_SCALE
            qm_ref[:, c0 + h * LANES:c0 + (h + 1) * LANES] = tile.astype(BF16)
    lane = lax.broadcasted_iota(jnp.int32, kpe.shape, 1)
    one_hot = jnp.where(lane == MLA_V, 1.0, 0.0)
    for c0 in range(0, 2 * MLA_WIDTH, step):
        acc = _dot(ckv, wkv_ref[:, c0:c0 + step])
        extra = kpe if c0 < MLA_WIDTH else one_hot
        for h in range(halves):
            sl = slice(h * LANES, (h + 1) * LANES)
            kvm_ref[:, c0 + h * LANES:c0 + (h + 1) * LANES] = (acc[:, sl] + extra).astype(BF16)


def _l0_in_proj(x, g, w_qkv, w_lat, gq, wq, gkv, wkv, cos_a, sin_a, cos_b, sin_b, bsz, seq, tm):
    t, d = x.shape
    tps = seq // tm
    gw = 3 * DSW_GROUP_WIDTH
    row = lambda w: pl.BlockSpec((tm, w), lambda i: (i, 0))

    def residue_major(dil):
        return pl.BlockSpec((None, dil, tm // dil, gw), lambda i: (i // tps, 0, i % tps, 0))

    dils = [dil for _, dil in DSW_GROUPS]
    n_slabs = sum(gw // LANES for dil in dils if dil > 1)
    return pl.pallas_call(
        _l0_in_kernel,
        grid=(t // tm,),
        in_specs=[row(d), _resident((1, d)), _resident(w_qkv.shape), _resident(w_lat.shape),
                  _resident(gq.shape), _resident(wq.shape), _resident(gkv.shape), _resident(wkv.shape),
                  row(LANES), row(LANES), row(LANES), row(LANES)],
        out_specs=[row(gw), residue_major(dils[1]), residue_major(dils[2]), row(MLA_WIDTH), row(2 * MLA_WIDTH)],
        out_shape=[jax.ShapeDtypeStruct((t, gw), BF16),
                   jax.ShapeDtypeStruct((bsz, dils[1], seq // dils[1], gw), BF16),
                   jax.ShapeDtypeStruct((bsz, dils[2], seq // dils[2], gw), BF16),
                   jax.ShapeDtypeStruct((t, MLA_WIDTH), BF16),
                   jax.ShapeDtypeStruct((t, 2 * MLA_WIDTH), BF16)],
        scratch_shapes=[pltpu.VMEM((n_slabs, tm, LANES), F32)],
        compiler_params=_params("parallel"),
        name="l0_in_proj",
    )(x, g, w_qkv, w_lat, gq, wq, gkv, wkv, cos_a, sin_a, cos_b, sin_b)


def _l1_in_kernel(x_ref, g_ref, wssm_ref, wsb_ref, proj_ref, qkv_ref):
    hn = _rms(x_ref[...], g_ref[...]).astype(BF16)
    step = MXU_WIDTH
    for c0 in range(0, SSM_PROJ_WIDTH, step):
        c1 = min(c0 + step, SSM_PROJ_WIDTH)
        proj_ref[:, c0:c1] = _dot(hn, wssm_ref[:, c0:c1])
    for c0 in range(0, 3 * SB_WIDTH, step):
        acc = _dot(hn, wsb_ref[:, c0:c0 + step])
        if c0 < SB_WIDTH:
            acc = acc * HEAD_DIM ** -0.5
        qkv_ref[:, c0:c0 + step] = acc.astype(BF16)


def _l1_in_proj(x, g, w_ssm, w_sb, tm):
    t, d = x.shape
    row = lambda w: pl.BlockSpec((tm, w), lambda i: (i, 0))
    return pl.pallas_call(
        _l1_in_kernel,
        grid=(t // tm,),
        in_specs=[row(d), _resident((1, d)), _resident(w_ssm.shape), _resident(w_sb.shape)],
        out_specs=[row(SSM_PROJ_WIDTH), row(3 * SB_WIDTH)],
        out_shape=[jax.ShapeDtypeStruct((t, SSM_PROJ_WIDTH), F32),
                   jax.ShapeDtypeStruct((t, 3 * SB_WIDTH), BF16)],
        compiler_params=_params("parallel"),
        name="l1_in_proj",
    )(x, g, w_ssm, w_sb)


def _dilated_kernel(c0_ref, p0_ref, c1_ref, p1_ref, c2_ref, p2_ref, y_ref, o_acc, m_acc, l_acc):
    blk = DSW_BLK
    gw = DSW_GROUP_WIDTH
    pairs = gw // LANES
    qcols, kcols, vcols = slice(0, gw), slice(gw, 2 * gw), slice(2 * gw, 3 * gw)
    first_key = jnp.where(pl.program_id(1) > 0, 0, blk)
    qi = lax.broadcasted_iota(jnp.int32, (blk, 2 * blk), 0)
    kj = lax.broadcasted_iota(jnp.int32, (blk, 2 * blk), 1)
    window = (kj >= qi) & (kj <= qi + blk)
    lane = lax.broadcasted_iota(jnp.int32, (blk, LANES), 1)
    first = lane < HEAD_DIM

    def unit(q, k2, v2, lowest_key):
        valid = window & (kj >= lowest_key)
        res = []
        for pair in range(pairs):
            sl = slice(pair * LANES, (pair + 1) * LANES)
            kk, vv = k2[:, sl], v2[:, sl]
            outs, maxes, sums = [], [], []
            for hh in range(2):
                s = jnp.where(valid, _dot_nt(_keep_head(q[:, sl], lane, hh), kk), NEG_BIG)
                m = jnp.max(s, axis=-1, keepdims=True)
                p = jnp.exp(s - m)
                outs.append(_dot(p.astype(BF16), vv))
                maxes.append(jnp.broadcast_to(m, (blk, LANES)))
                sums.append(jnp.broadcast_to(jnp.sum(p, axis=-1, keepdims=True), (blk, LANES)))
            res.append(tuple(jnp.where(first, a, b) for a, b in (outs, maxes, sums)))
        return res

    def put(rows, res):
        for pair, (o, m, l) in enumerate(res):
            o_acc[pair, rows, :] = o
            m_acc[pair, rows, :] = m
            l_acc[pair, rows, :] = l

    def merge(rows, res):
        for pair, (o, m, l) in enumerate(res):
            m_old = m_acc[pair, rows, :]
            m_new = jnp.maximum(m_old, m)
            keep, add = jnp.exp(m_old - m_new), jnp.exp(m - m_new)
            o_acc[pair, rows, :] = o_acc[pair, rows, :] * keep + o * add
            l_acc[pair, rows, :] = l_acc[pair, rows, :] * keep + l * add
            m_acc[pair, rows, :] = m_new

    def stacked(prev, cur):
        return jnp.concatenate([prev, cur], axis=0)

    head = slice(0, blk)

    def head_unit(cur, prev):
        return unit(cur(head, qcols), stacked(prev(kcols), cur(head, kcols)),
                    stacked(prev(vcols), cur(head, vcols)), first_key)

    def later_unit(cur, r0):
        keys = pl.ds(r0 - blk, 2 * blk)
        return unit(cur(pl.ds(r0, blk), qcols), cur(keys, kcols), cur(keys, vcols), 0)

    units_per_body = DSW_UNITS_PER_BODY

    cur0 = lambda rows, cols: c0_ref[rows, cols]
    put(head, head_unit(cur0, lambda cols: p0_ref[:, cols]))
    later_blocks = c0_ref.shape[0] // blk - 1
    group0 = 3
    assert later_blocks % group0 == 0

    def dense_blocks(u, carry):
        for k in range(group0):
            r0 = pl.multiple_of((1 + u * group0 + k) * blk, blk)
            put(pl.ds(r0, blk), later_unit(cur0, r0))
        return carry

    lax.fori_loop(0, later_blocks // group0, dense_blocks, 0)

    for cur_ref, prev_ref in ((c1_ref, p1_ref), (c2_ref, p2_ref)):
        dil, blocks = cur_ref.shape[0], cur_ref.shape[1] // blk
        residues_per_body = max(1, units_per_body // blocks)
        assert dil % residues_per_body == 0

        def residues(u, carry, cur_ref=cur_ref, prev_ref=prev_ref, dil=dil, blocks=blocks,
                     residues_per_body=residues_per_body):
            for k in range(residues_per_body):
                r = u * residues_per_body + k
                cur = lambda rows, cols, r=r: cur_ref[r, rows, cols]
                merge(pl.ds(r, blk, stride=dil), head_unit(cur, lambda cols, r=r: prev_ref[r, :, cols]))
                for n in range(1, blocks):
                    merge(pl.ds(n * blk * dil + r, blk, stride=dil), later_unit(cur, n * blk))
            return carry

        lax.fori_loop(0, dil // residues_per_body, residues, 0)

    for pair in range(pairs):
        y_ref[:, pair * LANES:(pair + 1) * LANES] = (o_acc[pair] * (1.0 / l_acc[pair])).astype(y_ref.dtype)


def _dilated_attention(g0, g1, g2, bsz, seq):
    blk = DSW_BLK
    span = blk * max(dil for _, dil in DSW_GROUPS)
    nspan = seq // span
    gw = 3 * DSW_GROUP_WIDTH
    d1, d2 = g1.shape[1], g2.shape[1]
    blocks_per_span0 = span // blk

    def residue_major(dil, rows, prev):
        per_span = span // dil // rows
        if prev:
            return pl.BlockSpec((None, dil, rows, gw), lambda b, s: (b, 0, jnp.maximum(s * per_span - 1, 0), 0))
        return pl.BlockSpec((None, dil, rows, gw), lambda b, s: (b, 0, s, 0))

    return pl.pallas_call(
        _dilated_kernel,
        grid=(bsz, nspan),
        in_specs=[
            pl.BlockSpec((span, gw), lambda b, s: (b * nspan + s, 0)),
            pl.BlockSpec((blk, gw), lambda b, s: (jnp.maximum((b * nspan + s) * blocks_per_span0 - 1, 0), 0)),
            residue_major(d1, span // d1, False), residue_major(d1, blk, True),
            residue_major(d2, span // d2, False), residue_major(d2, blk, True),
        ],
        out_specs=pl.BlockSpec((span, DSW_GROUP_WIDTH), lambda b, s: (b * nspan + s, 0)),
        out_shape=jax.ShapeDtypeStruct((bsz * seq, DSW_GROUP_WIDTH), BF16),
        scratch_shapes=[pltpu.VMEM((DSW_GROUP_WIDTH // LANES, span, LANES), F32)] * 3,
        compiler_params=_params("parallel", "arbitrary"),
        name="dilated_attention",
    )(g0, g0, g1, g1, g2, g2)


def _mla_attn_kernel(q_ref, k_ref, v_ref, o_ref, *, tq, tk, heads):
    i = pl.program_id(2)
    assert (tq // tk) % 2 == 0 and heads % 2 == 0
    lane = lax.broadcasted_iota(jnp.int32, (tq, LANES), 1)
    slots = [slice(hh * MLA_SLOT, (hh + 1) * MLA_SLOT) for hh in range(heads)]

    def update(q, m, acc, start, width, slot, visible=None):
        s = _dot_nt(q, k_ref[pl.ds(start, width), slot])
        if visible is not None:
            s = jnp.where(visible, s, NEG_BIG)
        m_new = jnp.maximum(m, jnp.max(s, axis=-1, keepdims=True))
        p = jnp.exp2(s - m_new)
        return m_new, jnp.exp2(m - m_new) * acc + _dot(p.astype(BF16), v_ref[pl.ds(start, width), slot])

    def block(j, carry, visible=None):
        start = pl.multiple_of(j * tk, tk)
        return tuple(update(q_ref[:, slots[hh]], *carry[hh], start, tk, slots[hh], visible) for hh in range(heads))

    def block_pair(jp, carry):
        return block(2 * jp + 1, block(2 * jp, carry))

    init = tuple((jnp.full((tq, 1), NEG_BIG, F32), jnp.zeros((tq, LANES), F32)) for _ in range(heads))
    diag_blocks = tq // tk
    carry = lax.fori_loop(0, i * (diag_blocks // 2), block_pair, init)
    row = lax.broadcasted_iota(jnp.int32, (tq, tk), 0)
    col = lax.broadcasted_iota(jnp.int32, (tq, tk), 1)
    for b in range(diag_blocks):
        carry = block(i * diag_blocks + b, carry, col + b * tk <= row)
    outs = [acc * (1.0 / acc[:, MLA_V:MLA_V + 1]) for _, acc in carry]
    for pr in range(heads // 2):
        packed = jnp.where(lane < MLA_V, outs[2 * pr], pltpu.roll(outs[2 * pr + 1], MLA_V, 1))
        o_ref[:, pr * LANES:(pr + 1) * LANES] = packed.astype(o_ref.dtype)


def _mla_attention(q, kv, bsz, seq, tq, tk, heads):
    nq = seq // tq
    groups = MLA_HEADS // heads
    kern = functools.partial(_mla_attn_kernel, tq=tq, tk=tk, heads=heads)
    return pl.pallas_call(
        kern,
        grid=(bsz, groups, nq),
        in_specs=[
            pl.BlockSpec((tq, heads * MLA_SLOT), lambda b, p, i: (b * nq + i, p)),
            pl.BlockSpec((seq, heads * MLA_SLOT), lambda b, p, i: (b, p)),
            pl.BlockSpec((seq, heads * MLA_SLOT), lambda b, p, i: (b, groups + p)),
        ],
        out_specs=pl.BlockSpec((tq, heads * MLA_V), lambda b, p, i: (b * nq + i, p)),
        out_shape=jax.ShapeDtypeStruct((bsz * seq, MLA_HEADS * MLA_V), BF16),
        compiler_params=_params("parallel", "parallel", "arbitrary"),
        name="mla_attention",
    )(q, kv, kv)


def _sb_attn_kernel(q_ref, k_ref, v_ref, o_ref, *, tq, tiles):
    first = pl.program_id(2) * tiles
    lane = lax.broadcasted_iota(jnp.int32, (tq, LANES), 1)
    row = lax.broadcasted_iota(jnp.int32, (tq, tq), 0)
    col = lax.broadcasted_iota(jnp.int32, (tq, tq), 1)
    later = jnp.where(row > col, 1.0, 0.0).astype(BF16)
    later2 = jnp.concatenate([later, later], axis=0)
    strict = jnp.concatenate([col < row] * 2, axis=0)
    qs = [jnp.concatenate([_keep_head(q_ref[t * tq:(t + 1) * tq, :], lane, hh) for hh in range(2)], axis=0)
          for t in range(tiles)]

    def block(q2, blk, run, acc, masked):
        start = pl.multiple_of(blk * tq, tq)
        z = _dot_nt(q2, k_ref[pl.ds(start, tq), :])
        log_beta = jnp.minimum(z, 0.0) - jnp.log(1.0 + jnp.exp(-jnp.abs(z)))
        log_stay = log_beta - z
        if masked:
            log_stay = jnp.where(strict, log_stay, 0.0)
        after = _dot(jnp.concatenate(_split2(log_stay), axis=1), later2) + run
        w = jnp.exp(log_beta + after)
        if masked:
            w = jnp.where(strict, w, 0.0)
        acc = acc + _dot(w.astype(BF16), v_ref[pl.ds(start, tq), :])
        return run + jnp.sum(log_stay, axis=-1, keepdims=True), acc

    run0 = jnp.zeros((2 * tq, 1), F32)
    acc0 = jnp.zeros((2 * tq, LANES), F32)

    def opening(first_has_left):
        states = []
        for t in range(tiles):
            run, acc = block(qs[t], first + t, run0, acc0, True)
            if t > 0 or first_has_left:
                run, acc = block(qs[t], first + t - 1, run, acc, False)
            states.append((run, acc))
        return tuple(states)

    states = lax.cond(first > 0, functools.partial(opening, True), functools.partial(opening, False))

    for t in range(tiles):

        def cond(c):
            left, run, _ = c
            return (left > 0) & (jnp.max(run) >= -SB_LOG_UNDERFLOW)

        def body(c, t=t):
            left, run, acc = c
            run, acc = block(qs[t], left - 1, run, acc, False)
            return left - 1, run, acc

        _, _, acc = lax.while_loop(cond, body, (jnp.maximum(first + t - 1, 0),) + states[t])
        o_ref[t * tq:(t + 1) * tq, :] = jnp.where(lane < HEAD_DIM, acc[:tq], acc[tq:]).astype(o_ref.dtype)


def _sb_attention(qkv, bsz, seq, tq, tiles):
    nq = seq // (tq * tiles)
    pairs = SB_WIDTH // LANES
    kern = functools.partial(_sb_attn_kernel, tq=tq, tiles=tiles)
    return pl.pallas_call(
        kern,
        grid=(bsz, pairs, nq),
        in_specs=[
            pl.BlockSpec((tq * tiles, LANES), lambda b, p, i: (b * nq + i, p)),
            pl.BlockSpec((seq, LANES), lambda b, p, i: (b, pairs + p)),
            pl.BlockSpec((seq, LANES), lambda b, p, i: (b, 2 * pairs + p)),
        ],
        out_specs=pl.BlockSpec((tq * tiles, LANES), lambda b, p, i: (b * nq + i, p)),
        out_shape=jax.ShapeDtypeStruct((bsz * seq, SB_WIDTH), BF16),
        compiler_params=_params("parallel", "parallel", "arbitrary"),
        name="stickbreak_attention",
    )(qkv, qkv, qkv)


def _ssd_kernel(z_ref, xs_ref, bc_ref, dt_ref, cwx_ref, cbx_ref, cwb_ref, cbb_ref, dtb_ref, alog_ref, dskip_ref,
                gn_ref, y_ref, xtail, btail, state):
    c = pl.program_id(1)
    cl = SSM_CHUNK
    halo = SUBLANES
    assert SSM_CONV == 4

    @pl.when(c == 0)
    def _():
        xtail[...] = jnp.zeros_like(xtail)
        btail[...] = jnp.zeros_like(btail)
        state[...] = jnp.zeros_like(state)

    def shift_rows(x, tail, k):
        rolled = pltpu.roll(x, k, 0)
        wrapped = pltpu.roll(tail, k, 0)
        row = lax.broadcasted_iota(jnp.int32, tail.shape, 0)
        return jnp.concatenate([jnp.where(row < k, wrapped, rolled[:halo]), rolled[halo:]], axis=0)

    def conv_silu(tails, raw_ref, w_ref, b_ref):
        x = raw_ref[...]
        x1 = shift_rows(x, tails[0], 1)
        u = w_ref[1:2, :] * x + w_ref[0:1, :] * x1
        y = b_ref[...] + w_ref[3:4, :] * x + w_ref[2:3, :] * x1 + shift_rows(u, tails[1], 2)
        tails[0] = x[cl - halo:, :]
        tails[1] = u[cl - halo:, :]
        return _silu(y)

    xs = conv_silu(xtail, xs_ref, cwx_ref, cbx_ref)
    bc = conv_silu(btail, bc_ref, cwb_ref, cbb_ref)

    dt = _softplus(dt_ref[...] + dtb_ref[...])
    da = dt * (-jnp.exp(alog_ref[...]))
    row = lax.broadcasted_iota(jnp.int32, (cl, cl), 0)
    col = lax.broadcasted_iota(jnp.int32, (cl, cl), 1)
    causal = col <= row
    tri = jnp.where(causal, 1.0, 0.0).astype(BF16)
    cs = _dot(jnp.concatenate([tri] * 3, axis=1), jnp.concatenate(_split3(da), axis=0))
    cs_t = cs.T
    ecs = jnp.exp(cs)
    dec_end = jnp.exp(cs[cl - 1:cl, :] - cs)

    head_of_lane = jnp.right_shift(lax.broadcasted_iota(jnp.int32, (LANES, SSM_INNER), 1), 6)
    expand = jnp.where(lax.broadcasted_iota(jnp.int32, (LANES, SSM_INNER), 0) == head_of_lane, 1.0, 0.0).astype(BF16)

    expand3 = jnp.concatenate([expand] * 3, axis=0)

    def per_head_lanes(v):
        return _dot(jnp.concatenate(_split3(v), axis=1), expand3)

    dt_e = per_head_lanes(dt)
    ecs_e = per_head_lanes(ecs)
    dec_end_e = per_head_lanes(dec_end)

    xdt = xs * dt_e
    xdt_b = xdt.astype(BF16)
    xw_b = (xdt * dec_end_e).astype(BF16)
    lane = lax.broadcasted_iota(jnp.int32, (cl, LANES), 1)
    first = lane < SSM_HEADDIM
    gs = SSM_STATE
    heads_per_group = SSM_HEADS // SSM_GROUPS
    for g in range(SSM_GROUPS):
        bg = bc[:, g * gs:(g + 1) * gs]
        cg_b = bc[:, (SSM_GROUPS + g) * gs:(SSM_GROUPS + g + 1) * gs].astype(BF16)
        cb = _dot_nt(cg_b, bg.astype(BF16))
        cols = slice(g * SSM_GROUP_WIDTH, (g + 1) * SSM_GROUP_WIDTH)
        prev = state[:, cols]
        y_off = _dot(cg_b, prev.astype(BF16)) * ecs_e[:, cols]
        state[:, cols] = prev * ecs_e[cl - 1:cl, cols] + _dot(bg.T.astype(BF16), xw_b[:, cols])
        for pr in range(heads_per_group // 2):
            pcols = slice(g * SSM_GROUP_WIDTH + pr * LANES, g * SSM_GROUP_WIDTH + (pr + 1) * LANES)
            x_pair = xdt_b[:, pcols]
            ys = []
            for hh in range(2):
                h = g * heads_per_group + 2 * pr + hh
                seg = jnp.where(causal, cs[:, h:h + 1] - cs_t[h:h + 1, :], NEG_BIG)
                ys.append(_dot((cb * jnp.exp(seg)).astype(BF16), x_pair))
            y_diag = jnp.where(first, ys[0], ys[1])
            y_pair = y_diag + y_off[:, pr * LANES:(pr + 1) * LANES] + xs[:, pcols] * dskip_ref[:, pcols]
            y_ref[:, pcols] = y_pair * _silu(z_ref[:, pcols])
    for g in range(SSM_GROUPS):
        cols = slice(g * SSM_GROUP_WIDTH, (g + 1) * SSM_GROUP_WIDTH)
        y_ref[:, cols] = _rms(y_ref[:, cols], gn_ref[:, cols])


def _ssd(proj, bsz, seq, cwx, cbx, cwb, cbb, dt_bias, a_log, d_skip, gnorm):
    cl = SSM_CHUNK
    nc = seq // cl

    def rows(width, colblk):
        return pl.BlockSpec((cl, width), lambda b, c: (b * nc + c, colblk))

    def const(r, width):
        return pl.BlockSpec((r, width), lambda b, c: (0, 0))

    return pl.pallas_call(
        _ssd_kernel,
        grid=(bsz, nc),
        in_specs=[
            rows(SSM_INNER, 0),
            rows(SSM_INNER, 1),
            rows(SSM_BC_WIDTH, 2 * SSM_INNER // SSM_BC_WIDTH),
            rows(LANES, (2 * SSM_INNER + SSM_BC_WIDTH) // LANES),
            const(SSM_CONV, SSM_INNER), const(1, SSM_INNER), const(SSM_CONV, SSM_BC_WIDTH), const(1, SSM_BC_WIDTH),
            const(1, LANES), const(1, LANES), const(1, SSM_INNER), const(1, SSM_INNER),
        ],
        out_specs=pl.BlockSpec((cl, SSM_INNER), lambda b, c: (b * nc + c, 0)),
        out_shape=jax.ShapeDtypeStruct((bsz * seq, SSM_INNER), F32),
        scratch_shapes=[
            pltpu.VMEM((2, SUBLANES, SSM_INNER), F32),
            pltpu.VMEM((2, SUBLANES, SSM_BC_WIDTH), F32),
            pltpu.VMEM((SSM_STATE, SSM_INNER), F32),
        ],
        compiler_params=_params("parallel", "arbitrary"),
        name="ssd_scan",
    )(proj, proj, proj, proj, cwx, cbx, cwb, cbb, dt_bias, a_log, d_skip, gnorm)


def _mix_ffn_kernel(*refs, tm, tiles_per_seq, final_norm):
    x_ref, a_ref, b_ref, wa_ref, wb_ref, g_ref, wg_ref, wu_ref, cw_ref, cb_ref, wd_ref = refs[:11]
    pos = 11
    fn_ref = None
    if final_norm:
        fn_ref = refs[pos]
        pos += 1
    o_ref, gbuf, act_ref = refs[pos:pos + 3]
    halo = SUBLANES

    o_ref[...] = x_ref[...] + _dot(a_ref[...].astype(BF16), wa_ref[...]) + _dot(b_ref[...].astype(BF16), wb_ref[...])
    hn = _rms(o_ref[...], g_ref[...]).astype(BF16)

    @pl.when(pl.program_id(0) % tiles_per_seq == 0)
    def _():
        gbuf[0:halo, :] = jnp.zeros((halo, gbuf.shape[1]), F32)

    step = MXU_WIDTH
    for c0 in range(0, FFN_DIM, step):
        cols = slice(c0, c0 + step)
        gate = _dot(hn, wg_ref[:, cols])
        up = _dot(hn, wu_ref[:, cols])
        gbuf[halo:halo + tm, cols] = gate
        conv = cb_ref[:, cols] + cw_ref[FFN_CONV - 1:FFN_CONV, cols] * gate
        for t in range(FFN_CONV - 1):
            back = FFN_CONV - 1 - t
            conv = conv + cw_ref[t:t + 1, cols] * gbuf[halo - back:halo - back + tm, cols]
        gbuf[0:halo, cols] = gbuf[tm:tm + halo, cols]
        act_ref[:, cols] = (_silu(conv) * up).astype(BF16)

    y = o_ref[...] + _dot(act_ref[...], wd_ref[...])
    if final_norm:
        y = _rms(y, fn_ref[...])
    o_ref[...] = y


def _mix_ffn(x, a, b, wa, wb, g, wg, wu, cw, cb, wd, seq, *, tm, final_gain=None):
    t, d = x.shape
    f = wg.shape[1]
    final_norm = final_gain is not None
    row = lambda w: pl.BlockSpec((tm, w), lambda i: (i, 0))
    in_specs = [row(d), row(a.shape[1]), row(b.shape[1]), _resident(wa.shape), _resident(wb.shape),
                _resident(g.shape), _resident(wg.shape), _resident(wu.shape), _resident(cw.shape),
                _resident(cb.shape), _resident(wd.shape)]
    args = [x, a, b, wa, wb, g, wg, wu, cw, cb, wd]
    if final_norm:
        in_specs.append(_resident(final_gain.shape))
        args.append(final_gain)
    kern = functools.partial(_mix_ffn_kernel, tm=tm, tiles_per_seq=seq // tm, final_norm=final_norm)
    return pl.pallas_call(
        kern,
        grid=(t // tm,),
        in_specs=in_specs,
        out_specs=row(d),
        out_shape=jax.ShapeDtypeStruct((t, d), F32),
        scratch_shapes=[pltpu.VMEM((tm + SUBLANES, f), F32), pltpu.VMEM((tm, f), BF16)],
        compiler_params=_params("arbitrary"),
        name="mix_ffn",
    )(*args)


def _row(v):
    return v.reshape(1, -1).astype(F32)


def _pad_cols(w, n):
    return jnp.pad(w, ((0, 0), (0, n - w.shape[1])))


def _rope_lane_constants():
    lane = jnp.arange(LANES)
    half_a = HEAD_DIM // 2
    inv_a = 1.0 / (ROPE_THETA ** (jnp.arange(half_a, dtype=F32) * (2.0 / HEAD_DIM)))
    freq_a = inv_a[lane % half_a]
    sign_a = jnp.where((lane % HEAD_DIM) < half_a, -1.0, 1.0).astype(F32)
    half_b = MLA_ROPE // 2
    assert HEAD_DIM == 2 * MLA_ROPE
    in_rope = (lane >= MLA_NOPE) & (lane < MLA_NOPE + MLA_ROPE)
    sign_b = jnp.where(in_rope, jnp.where((lane - MLA_NOPE) < half_b, -1.0, 1.0), 0.0).astype(F32)
    rest_b = jnp.where(in_rope, 0.0, 1.0).astype(F32)
    src = jnp.arange(LANES)[:, None]
    packed_tok = jnp.arange(ROPE_PACK)[:, None, None]
    sel_a = (src == packed_tok * half_a + lane % half_a).astype(BF16)
    sel_b = (in_rope & (src == packed_tok * half_a + 2 * ((lane - MLA_NOPE) % half_b))).astype(BF16)
    return _row(freq_a), sel_a, sel_b, _row(sign_a), _row(sign_b), _row(rest_b)


def _mla_slot_weights(w_uq, w_ukv):
    rq = w_uq.shape[0]
    wq = w_uq.reshape(rq, MLA_HEADS, MLA_NOPE + MLA_ROPE)
    wq = jnp.pad(wq, ((0, 0), (0, 0), (0, MLA_SLOT - MLA_NOPE - MLA_ROPE))).reshape(rq, MLA_WIDTH)
    rk = w_ukv.shape[0]
    wkv = w_ukv.reshape(rk, MLA_HEADS, MLA_NOPE + MLA_V)
    wk = jnp.pad(wkv[:, :, :MLA_NOPE], ((0, 0), (0, 0), (0, MLA_SLOT - MLA_NOPE))).reshape(rk, MLA_WIDTH)
    wv = jnp.pad(wkv[:, :, MLA_NOPE:], ((0, 0), (0, 0), (0, MLA_SLOT - MLA_V))).reshape(rk, MLA_WIDTH)
    return wq.astype(BF16), jnp.concatenate([wk, wv], axis=1).astype(BF16)


def kernel(x, positions,
           l0_norm_mix, l0_w_in, l0_mla_q_norm, l0_mla_w_uq, l0_mla_kv_norm, l0_mla_w_ukv, l0_w_out,
           l0_norm_ffn, l0_ffn_w_gate, l0_ffn_w_up, l0_ffn_conv_w, l0_ffn_conv_b, l0_ffn_w_down,
           l1_norm_mix, l1_w_in, l1_ssm_conv_w, l1_ssm_conv_b, l1_ssm_dt_bias, l1_ssm_a_log, l1_ssm_d,
           l1_ssm_norm, l1_w_out,
           l1_norm_ffn, l1_ffn_w_gate, l1_ffn_w_up, l1_ffn_conv_w, l1_ffn_conv_b, l1_ffn_w_down,
           final_norm):
    bsz, seq, d = x.shape
    t = bsz * seq
    xf = x.reshape(t, d)
    tm = 512

    pos_packed = jnp.repeat(positions.reshape(t // ROPE_PACK, ROPE_PACK).astype(F32), LANES // ROPE_PACK, axis=1)
    cos_a, sin_a, cos_b, sin_b = _rope_tables(pos_packed, *_rope_lane_constants(), rows=tm)

    nd = DSW_WIDTH
    gw = DSW_GROUP_WIDTH
    w_qkv = jnp.concatenate([l0_w_in[:, part * nd + gi * gw:part * nd + (gi + 1) * gw]
                             for gi in range(len(DSW_GROUPS)) for part in range(3)], axis=1).astype(BF16)
    w_cq = l0_w_in[:, 3 * nd:3 * nd + MLA_Q_RANK]
    w_ckv = l0_w_in[:, 3 * nd + MLA_Q_RANK:3 * nd + MLA_Q_RANK + MLA_KV_RANK]
    w_kpe = l0_w_in[:, 3 * nd + MLA_Q_RANK + MLA_KV_RANK:]
    w_kpe_slot = jnp.pad(w_kpe, ((0, 0), (MLA_NOPE, MLA_SLOT - MLA_NOPE - MLA_ROPE)))
    w_lat = jnp.concatenate([w_cq, w_ckv, w_kpe_slot], axis=1).astype(BF16)
    wq_slot, wkv_slot = _mla_slot_weights(l0_mla_w_uq, l0_mla_w_ukv)
    g0, g1, g2, q_mla, kv_mla = _l0_in_proj(xf, _row(l0_norm_mix), w_qkv, w_lat, _row(l0_mla_q_norm), wq_slot,
                                            _row(l0_mla_kv_norm), wkv_slot, cos_a, sin_a, cos_b, sin_b,
                                            bsz, seq, tm)
    y_a = _dilated_attention(g0, g1, g2, bsz, seq)
    y_b = _mla_attention(q_mla, kv_mla, bsz, seq, tq=1024, tk=512, heads=4)

    w_out0 = l0_w_out.astype(BF16)
    x2 = _mix_ffn(xf, y_a, y_b, w_out0[:DSW_GROUP_WIDTH], w_out0[DSW_GROUP_WIDTH:], _row(l0_norm_ffn),
                  l0_ffn_w_gate.astype(BF16), l0_ffn_w_up.astype(BF16), l0_ffn_conv_w, _row(l0_ffn_conv_b),
                  l0_ffn_w_down.astype(BF16), seq, tm=tm)

    o_dt = 2 * SSM_INNER + SSM_BC_WIDTH
    w_ssm = _pad_cols(l1_w_in[:, :o_dt + SSM_HEADS], SSM_PROJ_WIDTH).astype(BF16)
    w_sb = l1_w_in[:, o_dt + SSM_HEADS:].astype(BF16)
    proj, qkv_sb = _l1_in_proj(x2, _row(l1_norm_mix), w_ssm, w_sb, tm)

    cw = l1_ssm_conv_w
    cb = _row(l1_ssm_conv_b)
    y_c = _ssd(proj, bsz, seq, cw[:, :SSM_INNER], cb[:, :SSM_INNER], cw[:, SSM_INNER:], cb[:, SSM_INNER:],
               _pad_cols(_row(l1_ssm_dt_bias), LANES), _pad_cols(_row(l1_ssm_a_log), LANES),
               _row(jnp.repeat(l1_ssm_d, SSM_HEADDIM)), _row(l1_ssm_norm))
    y_d = _sb_attention(qkv_sb, bsz, seq, tq=256, tiles=4)

    w_out1 = l1_w_out.astype(BF16)
    out = _mix_ffn(x2, y_c, y_d, w_out1[:SSM_INNER], w_out1[SSM_INNER:], _row(l1_norm_ffn),
                   l1_ffn_w_gate.astype(BF16), l1_ffn_w_up.astype(BF16), l1_ffn_conv_w, _row(l1_ffn_conv_b),
                   l1_ffn_w_down.astype(BF16), seq, tm=tm, final_gain=_row(final_norm))
    return out.reshape(bsz, seq, d)
```

```python
import functools
import math

import jax
import jax.numpy as jnp
from jax import lax
from jax.experimental import pallas as pl
from jax.experimental.pallas import tpu as pltpu

F32 = jnp.float32
BF16 = jnp.bfloat16

LANES = 128
SUBLANES = 8
MXU_WIDTH = 256
VMEM_LIMIT_BYTES = 56 * 1024 * 1024

D_MODEL = 1024
HEAD_DIM = 64
ROPE_THETA = 10000.0
ROPE_PACK = 4
NORM_EPS = 1e-6

DSW_GROUPS = ((128, 1), (512, 4), (2048, 16))
DSW_HEADS_PER_GROUP = 4
DSW_HEADS = DSW_HEADS_PER_GROUP * len(DSW_GROUPS)
DSW_BLK = 128
DSW_WIDTH = DSW_HEADS * HEAD_DIM
DSW_GROUP_WIDTH = DSW_HEADS_PER_GROUP * HEAD_DIM
DSW_UNITS_PER_BODY = 4

MLA_HEADS = 12
MLA_Q_RANK = 256
MLA_KV_RANK = 128
MLA_NOPE = 64
MLA_ROPE = 32
MLA_V = 64
MLA_SLOT = 128
MLA_WIDTH = MLA_HEADS * MLA_SLOT
MLA_LAT_WIDTH =MLA_Q_RANK + MLA_KV_RANK + MLA_SLOT
MLA_Q_SCALE = (MLA_NOPE + MLA_ROPE) ** -0.5 * math.log2(math.e)

SSM_INNER = 1024
SSM_HEADDIM = 64
SSM_HEADS = 16
SSM_STATE = 128
SSM_GROUPS = 2
SSM_CONV = 4
SSM_CHUNK = 128
SSM_GROUP_WIDTH = SSM_INNER // SSM_GROUPS
SSM_BC_WIDTH = 2 * SSM_GROUPS * SSM_STATE
SSM_DT_PAD = LANES
SSM_PROJ_WIDTH = 2 * SSM_INNER + SSM_BC_WIDTH + SSM_DT_PAD

SB_HEADS = 8
SB_WIDTH = SB_HEADS * HEAD_DIM
SB_LOG_UNDERFLOW = 104.0

FFN_DIM = 2816
FFN_CONV = 3

NEG_BIG = -1e30

ROW_TILE = 512
MLA_TQ, MLA_TK, MLA_HEADS_PER_STEP = 1024, 512, 4
SB_TQ, SB_TILES_PER_STEP = 256, 4


def _params(*sem):
    return pltpu.CompilerParams(dimension_semantics=sem, vmem_limit_bytes=VMEM_LIMIT_BYTES)


def _rms(x, g):
    return x * lax.rsqrt(jnp.mean(x * x, axis=-1, keepdims=True) + NORM_EPS) * g


def _silu(x):
    return x * (1.0 / (1.0 + jnp.exp(-x)))


def _log1p(e):
    u = 1.0 + e
    return jnp.where(u == 1.0, e, jnp.log(u) * (e / (u - 1.0)))


def _softplus(x):
    return jnp.maximum(x, 0.0) + _log1p(jnp.exp(-jnp.abs(x)))


def _swap_halves(x, half):
    lane = lax.broadcasted_iota(jnp.int32, x.shape, 1)
    up = pltpu.roll(x, LANES - half, 1)
    down = pltpu.roll(x, half, 1)
    return jnp.where((lane & half) == 0, up, down)


def _rope_tile(x, cos, sin, half):
    return x * cos + _swap_halves(x, half) * sin


def _split2(x):
    hi = x.astype(BF16)
    return hi, (x - hi.astype(F32)).astype(BF16)


def _split3(x):
    hi = x.astype(BF16)
    r1 = x - hi.astype(F32)
    mid = r1.astype(BF16)
    lo = (r1 - mid.astype(F32)).astype(BF16)
    return hi, mid, lo


def _dot(a, b):
    return jnp.dot(a, b, preferred_element_type=F32)


def _dot_nt(a, b):
    return lax.dot_general(a, b, (((1,), (1,)), ((), ())), preferred_element_type=F32)


def _keep_head(x2, lane, second):
    mine = (lane >= HEAD_DIM) if second else (lane < HEAD_DIM)
    return jnp.where(mine, x2.astype(F32), 0.0).astype(BF16)


def _resident(shape):
    return pl.BlockSpec(shape, lambda *_: (0,) * len(shape), pipeline_mode=pl.Buffered(1))


def _rope_table_kernel(pos_ref, freq_ref, sela_ref, selb_ref, signa_ref, signb_ref, restb_ref,
                       cosa_ref, sina_ref, cosb_ref, sinb_ref):
    ang = pos_ref[...] * freq_ref[...]
    cos3 = jnp.concatenate(_split3(jnp.cos(ang)), axis=1)
    sin3 = jnp.concatenate(_split3(jnp.sin(ang)), axis=1)
    rows = ang.shape[0]

    def spread(parts, sel):
        return _dot(parts, jnp.concatenate([sel] * 3, axis=0))

    for g in range(ROPE_PACK):
        tokens = pl.ds(g, rows, stride=ROPE_PACK)
        cosa_ref[tokens, :] = spread(cos3, sela_ref[g])
        sina_ref[tokens, :] = spread(sin3, sela_ref[g]) * signa_ref[...]
        cosb_ref[tokens, :] = spread(cos3, selb_ref[g]) + restb_ref[...]
        sinb_ref[tokens, :] = spread(sin3, selb_ref[g]) * signb_ref[...]


def _rope_tables(pos_packed, freq, sel_a, sel_b, sign_a, sign_b, rest_b, rows):
    packed = pos_packed.shape[0]
    t = packed * ROPE_PACK
    const = pl.BlockSpec((1, LANES), lambda i: (0, 0))
    sel = pl.BlockSpec((ROPE_PACK, LANES, LANES), lambda i: (0, 0, 0))
    out = pl.BlockSpec((rows * ROPE_PACK, LANES), lambda i: (i, 0))
    return pl.pallas_call(
        _rope_table_kernel,
        grid=(packed // rows,),
        in_specs=[pl.BlockSpec((rows, LANES), lambda i: (i, 0)), const, sel, sel, const, const, const],
        out_specs=[out] * 4,
        out_shape=[jax.ShapeDtypeStruct((t, LANES), F32)] * 4,
        compiler_params=_params("parallel"),
        name="rope_tables",
    )(pos_packed, freq, sel_a, sel_b, sign_a, sign_b, rest_b)


def _l0_in_kernel(x_ref, g_ref, wqkv_ref, wlat_ref, gq_ref, wq_ref, gkv_ref, wkv_ref,
                  cosa_ref, sina_ref, cosb_ref, sinb_ref, g0_ref, g1_ref, g2_ref, qm_ref, kvm_ref, slabs):
    tm = x_ref.shape[0]
    hn = _rms(x_ref[...], g_ref[...]).astype(BF16)
    cos_a, sin_a = cosa_ref[...], sina_ref[...]
    cos_b, sin_b = cosb_ref[...], sinb_ref[...]
    half_a, half_b = HEAD_DIM // 2, MLA_ROPE // 2
    step = MXU_WIDTH
    halves = step // LANES
    group_refs = (g0_ref, g1_ref, g2_ref)
    slab = 0
    for gi, (_, dil) in enumerate(DSW_GROUPS):
        for part in range(3):
            c0 = (3 * gi + part) * DSW_GROUP_WIDTH
            acc = _dot(hn, wqkv_ref[:, c0:c0 + step])
            for h in range(halves):
                tile = acc[:, h * LANES:(h + 1) * LANES]
                if part < 2:
                    tile = _rope_tile(tile, cos_a, sin_a, half_a)
                if part == 0:
                    tile = tile * HEAD_DIM ** -0.5
                col = part * DSW_GROUP_WIDTH + h * LANES
                if dil == 1:
                    g0_ref[:, col:col + LANES] = tile.astype(BF16)
                else:
                    slabs[slab] = tile
                    for r in range(dil):
                        rows = slabs[slab, pl.ds(r, tm // dil, stride=dil), :]
                        group_refs[gi][r, :, col:col + LANES] = rows.astype(BF16)
                    slab += 1

    lat = _dot(hn, wlat_ref[...])
    cq = _rms(lat[:, :MLA_Q_RANK], gq_ref[...]).astype(BF16)
    ckv = _rms(lat[:, MLA_Q_RANK:MLA_Q_RANK + MLA_KV_RANK], gkv_ref[...]).astype(BF16)
    kpe = _rope_tile(lat[:, MLA_Q_RANK + MLA_KV_RANK:], cos_b, sin_b, half_b)
    for c0 in range(0, MLA_WIDTH, step):
        acc = _dot(cq, wq_ref[:, c0:c0 + step])
        for h in range(halves):
            sl = slice(h * LANES, (h + 1) * LANES)
            tile = _rope_tile(acc[:, sl], cos_b, sin_b, half_b) * MLA_Q_SCALE
            qm_ref[:, c0 + h * LANES:c0 + (h + 1) * LANES] = tile.astype(BF16)
    lane = lax.broadcasted_iota(jnp.int32, kpe.shape, 1)
    one_hot = jnp.where(lane == MLA_V, 1.0, 0.0)
    for c0 in range(0, 2 * MLA_WIDTH, step):
        acc = _dot(ckv, wkv_ref[:, c0:c0 + step])
        extra = kpe if c0 < MLA_WIDTH else one_hot
        for h in range(halves):
            sl = slice(h * LANES, (h + 1) * LANES)
            kvm_ref[:, c0 + h * LANES:c0 + (h + 1) * LANES] = (acc[:, sl] + extra).astype(BF16)


def _l0_in_proj(x, g, w_qkv, w_lat, gq, wq, gkv, wkv, cos_a, sin_a, cos_b, sin_b, bsz, seq, tm):
    t, d = x.shape
    tps = seq // tm
    gw = 3 * DSW_GROUP_WIDTH
    row = lambda w: pl.BlockSpec((tm, w), lambda i: (i, 0))

    def residue_major(dil):
        return pl.BlockSpec((None, dil, tm // dil, gw), lambda i: (i // tps, 0, i % tps, 0))

    dils = [dil for _, dil in DSW_GROUPS]
    n_slabs = sum(gw // LANES for dil in dils if dil > 1)
    return pl.pallas_call(
        _l0_in_kernel,
        grid=(t // tm,),
        in_specs=[row(d), _resident((1, d)), _resident(w_qkv.shape), _resident(w_lat.shape),
                  _resident(gq.shape), _resident(wq.shape), _resident(gkv.shape), _resident(wkv.shape),
                  row(LANES), row(LANES), row(LANES), row(LANES)],
        out_specs=[row(gw), residue_major(dils[1]), residue_major(dils[2]), row(MLA_WIDTH), row(2 * MLA_WIDTH)],
        out_shape=[jax.ShapeDtypeStruct((t, gw), BF16),
                   jax.ShapeDtypeStruct((bsz, dils[1], seq // dils[1], gw), BF16),
                   jax.ShapeDtypeStruct((bsz, dils[2], seq // dils[2], gw), BF16),
                   jax.ShapeDtypeStruct((t, MLA_WIDTH), BF16),
                   jax.ShapeDtypeStruct((t, 2 * MLA_WIDTH), BF16)],
        scratch_shapes=[pltpu.VMEM((n_slabs, tm, LANES), F32)],
        compiler_params=_params("parallel"),
        name="l0_in_proj",
    )(x, g, w_qkv, w_lat, gq, wq, gkv, wkv, cos_a, sin_a, cos_b, sin_b)


def _l1_in_kernel(x_ref, g_ref, wssm_ref, wsb_ref, proj_ref, qkv_ref):
    hn = _rms(x_ref[...], g_ref[...]).astype(BF16)
    step = MXU_WIDTH
    for c0 in range(0, SSM_PROJ_WIDTH, step):
        c1 = min(c0 + step, SSM_PROJ_WIDTH)
        proj_ref[:, c0:c1] = _dot(hn, wssm_ref[:, c0:c1])
    for c0 in range(0, 3 * SB_WIDTH, step):
        acc = _dot(hn, wsb_ref[:, c0:c0 + step])
        if c0 < SB_WIDTH:
            acc = acc * HEAD_DIM ** -0.5
        qkv_ref[:, c0:c0 + step] = acc.astype(BF16)


def _l1_in_proj(x, g, w_ssm, w_sb, tm):
    t, d = x.shape
    row = lambda w: pl.BlockSpec((tm, w), lambda i: (i, 0))
    return pl.pallas_call(
        _l1_in_kernel,
        grid=(t // tm,),
        in_specs=[row(d), _resident((1, d)), _resident(w_ssm.shape), _resident(w_sb.shape)],
        out_specs=[row(SSM_PROJ_WIDTH), row(3 * SB_WIDTH)],
        out_shape=[jax.ShapeDtypeStruct((t, SSM_PROJ_WIDTH), F32),
                   jax.ShapeDtypeStruct((t, 3 * SB_WIDTH), BF16)],
        compiler_params=_params("parallel"),
        name="l1_in_proj",
    )(x, g, w_ssm, w_sb)


def _dilated_kernel(c0_ref, p0_ref, c1_ref, p1_ref, c2_ref, p2_ref, y_ref, o_acc, m_acc, l_acc):
    blk = DSW_BLK
    gw = DSW_GROUP_WIDTH
    pairs = gw // LANES
    qcols, kcols, vcols = slice(0, gw), slice(gw, 2 * gw), slice(2 * gw, 3 * gw)
    first_key = jnp.where(pl.program_id(1) > 0, 0, blk)
    qi = lax.broadcasted_iota(jnp.int32, (blk, 2 * blk), 0)
    kj = lax.broadcasted_iota(jnp.int32, (blk, 2 * blk), 1)
    window = (kj >= qi) & (kj <= qi + blk)
    lane = lax.broadcasted_iota(jnp.int32, (blk, LANES), 1)
    first = lane < HEAD_DIM

    def unit(q, k2, v2, lowest_key):
        valid = window & (kj >= lowest_key)
        res = []
        for pair in range(pairs):
            sl = slice(pair * LANES, (pair + 1) * LANES)
            kk, vv = k2[:, sl], v2[:, sl]
            outs, maxes, sums = [], [], []
            for hh in range(2):
                s = jnp.where(valid, _dot_nt(_keep_head(q[:, sl], lane, hh), kk), NEG_BIG)
                m = jnp.max(s, axis=-1, keepdims=True)
                p = jnp.exp(s - m)
                outs.append(_dot(p.astype(BF16), vv))
                maxes.append(jnp.broadcast_to(m, (blk, LANES)))
                sums.append(jnp.broadcast_to(jnp.sum(p, axis=-1, keepdims=True), (blk, LANES)))
            res.append(tuple(jnp.where(first, a, b) for a, b in (outs, maxes, sums)))
        return res

    def put(rows, res):
        for pair, (o, m, l) in enumerate(res):
            o_acc[pair, rows, :] = o
            m_acc[pair, rows, :] = m
            l_acc[pair, rows, :] = l

    def merge(rows, res):
        for pair, (o, m, l) in enumerate(res):
            m_old = m_acc[pair, rows, :]
            m_new = jnp.maximum(m_old, m)
            keep, add = jnp.exp(m_old - m_new), jnp.exp(m - m_new)
            o_acc[pair, rows, :] = o_acc[pair, rows, :] * keep + o * add
            l_acc[pair, rows, :] = l_acc[pair, rows, :] * keep + l * add
            m_acc[pair, rows, :] = m_new

    def stacked(prev, cur):
        return jnp.concatenate([prev, cur], axis=0)

    head = slice(0, blk)

    def head_unit(cur, prev):
        return unit(cur(head, qcols), stacked(prev(kcols), cur(head, kcols)),
                    stacked(prev(vcols), cur(head, vcols)), first_key)

    def later_unit(cur, r0):
        keys = pl.ds(r0 - blk, 2 * blk)
        return unit(cur(pl.ds(r0, blk), qcols), cur(keys, kcols), cur(keys, vcols), 0)

    units_per_body = DSW_UNITS_PER_BODY

    cur0 = lambda rows, cols: c0_ref[rows, cols]
    put(head, head_unit(cur0, lambda cols: p0_ref[:, cols]))
    later_blocks = c0_ref.shape[0] // blk - 1
    group0 = 5
    assert later_blocks % group0 == 0

    def dense_blocks(u, carry):
        for k in range(group0):
            r0 = pl.multiple_of((1 + u * group0 + k) * blk, blk)
            put(pl.ds(r0, blk), later_unit(cur0, r0))
        return carry

    lax.fori_loop(0, later_blocks // group0, dense_blocks, 0)

    for cur_ref, prev_ref in ((c1_ref, p1_ref), (c2_ref, p2_ref)):
        dil, blocks = cur_ref.shape[0], cur_ref.shape[1] // blk
        residues_per_body = max(1, units_per_body // blocks)
        assert dil % residues_per_body == 0

        def residues(u, carry, cur_ref=cur_ref, prev_ref=prev_ref, dil=dil, blocks=blocks,
                     residues_per_body=residues_per_body):
            for k in range(residues_per_body):
                r = u * residues_per_body + k
                cur = lambda rows, cols, r=r: cur_ref[r, rows, cols]
                merge(pl.ds(r, blk, stride=dil), head_unit(cur, lambda cols, r=r: prev_ref[r, :, cols]))
                for n in range(1, blocks):
                    merge(pl.ds(n * blk * dil + r, blk, stride=dil), later_unit(cur, n * blk))
            return carry

        lax.fori_loop(0, dil // residues_per_body, residues, 0)

    for pair in range(pairs):
        y_ref[:, pair * LANES:(pair + 1) * LANES] = (o_acc[pair] * (1.0 / l_acc[pair])).astype(y_ref.dtype)


def _dilated_attention(g0, g1, g2, bsz, seq):
    blk = DSW_BLK
    span = blk * max(dil for _, dil in DSW_GROUPS)
    nspan = seq // span
    gw = 3 * DSW_GROUP_WIDTH
    d1, d2 = g1.shape[1], g2.shape[1]
    blocks_per_span0 = span // blk

    def residue_major(dil, rows, prev):
        per_span = span // dil // rows
        if prev:
            return pl.BlockSpec((None, dil, rows, gw), lambda b, s: (b, 0, jnp.maximum(s * per_span - 1, 0), 0))
        return pl.BlockSpec((None, dil, rows, gw), lambda b, s: (b, 0, s, 0))

    return pl.pallas_call(
        _dilated_kernel,
        grid=(bsz, nspan),
        in_specs=[
            pl.BlockSpec((span, gw), lambda b, s: (b * nspan + s, 0)),
            pl.BlockSpec((blk, gw), lambda b, s: (jnp.maximum((b * nspan + s) * blocks_per_span0 - 1, 0), 0)),
            residue_major(d1, span // d1, False), residue_major(d1, blk, True),
            residue_major(d2, span // d2, False), residue_major(d2, blk, True),
        ],
        out_specs=pl.BlockSpec((span, DSW_GROUP_WIDTH), lambda b, s: (b * nspan + s, 0)),
        out_shape=jax.ShapeDtypeStruct((bsz * seq, DSW_GROUP_WIDTH), BF16),
        scratch_shapes=[pltpu.VMEM((DSW_GROUP_WIDTH // LANES, span, LANES), F32)] * 3,
        compiler_params=_params("parallel", "arbitrary"),
        name="dilated_attention",
    )(g0, g0, g1, g1, g2, g2)


def _mla_attn_kernel(q_ref, k_ref, v_ref, o_ref, *, tq, tk, heads):
    i = pl.program_id(2)
    assert (tq // tk) % 2 == 0 and heads % 2 == 0
    lane = lax.broadcasted_iota(jnp.int32, (tq, LANES), 1)
    slots = [slice(hh * MLA_SLOT, (hh + 1) * MLA_SLOT) for hh in range(heads)]

    def update(q, m, acc, start, width, slot, visible=None):
        s = _dot_nt(q, k_ref[pl.ds(start, width), slot])
        if visible is not None:
            s = jnp.where(visible, s, NEG_BIG)
        m_new = jnp.maximum(m, jnp.max(s, axis=-1, keepdims=True))
        p = jnp.exp2(s - m_new)
        return m_new, jnp.exp2(m - m_new) * acc + _dot(p.astype(BF16), v_ref[pl.ds(start, width), slot])

    def block(j, carry):
        start = pl.multiple_of(j * tk, tk)
        return tuple(update(q_ref[:, slots[hh]], *carry[hh], start, tk, slots[hh]) for hh in range(heads))

    def block_pair(jp, carry):
        return block(2 * jp + 1, block(2 * jp, carry))

    init = tuple((jnp.full((tq, 1), NEG_BIG, F32), jnp.zeros((tq, LANES), F32)) for _ in range(heads))
    diag_blocks = tq // tk
    carry = lax.fori_loop(0, i * (diag_blocks // 2), block_pair, init)
    for b in range(diag_blocks):
        r0 = b * tk
        row = lax.broadcasted_iota(jnp.int32, (tq - r0, tk), 0)
        col = lax.broadcasted_iota(jnp.int32, (tq - r0, tk), 1)
        start = pl.multiple_of((i * diag_blocks + b) * tk, tk)
        new = []
        for hh in range(heads):
            m, acc = carry[hh]
            m_low, acc_low = update(q_ref[r0:, slots[hh]], m[r0:], acc[r0:], start, tk, slots[hh], col <= row)
            if r0:
                m_low = jnp.concatenate([m[:r0], m_low], axis=0)
                acc_low = jnp.concatenate([acc[:r0], acc_low], axis=0)
            new.append((m_low, acc_low))
        carry = tuple(new)
    outs = [acc * (1.0 / acc[:, MLA_V:MLA_V + 1]) for _, acc in carry]
    for pr in range(heads // 2):
        packed = jnp.where(lane < MLA_V, outs[2 * pr], pltpu.roll(outs[2 * pr + 1], MLA_V, 1))
        o_ref[:, pr * LANES:(pr + 1) * LANES] = packed.astype(o_ref.dtype)


def _mla_attention(q, kv, bsz, seq, tq, tk, heads):
    nq = seq // tq
    groups = MLA_HEADS // heads
    kern = functools.partial(_mla_attn_kernel, tq=tq, tk=tk, heads=heads)
    return pl.pallas_call(
        kern,
        grid=(bsz, groups, nq),
        in_specs=[
            pl.BlockSpec((tq, heads * MLA_SLOT), lambda b, p, i: (b * nq + i, p)),
            pl.BlockSpec((seq, heads * MLA_SLOT), lambda b, p, i: (b, p)),
            pl.BlockSpec((seq, heads * MLA_SLOT), lambda b, p, i: (b, groups + p)),
        ],
        out_specs=pl.BlockSpec((tq, heads * MLA_V), lambda b, p, i: (b * nq + i, p)),
        out_shape=jax.ShapeDtypeStruct((bsz * seq, MLA_HEADS * MLA_V), BF16),
        compiler_params=_params("parallel", "parallel", "arbitrary"),
        name="mla_attention",
    )(q, kv, kv)


def _sb_attn_kernel(q_ref, k_ref, v_ref, o_ref, *, tq, tiles):
    first = pl.program_id(2) * tiles
    lane = lax.broadcasted_iota(jnp.int32, (tq, LANES), 1)
    row = lax.broadcasted_iota(jnp.int32, (tq, tq), 0)
    col = lax.broadcasted_iota(jnp.int32, (tq, tq), 1)
    later = jnp.where(row > col, 1.0, 0.0).astype(BF16)
    later2 = jnp.concatenate([later, later], axis=0)
    strict = jnp.concatenate([col < row] * 2, axis=0)
    qs = [jnp.concatenate([_keep_head(q_ref[t * tq:(t + 1) * tq, :], lane, hh) for hh in range(2)], axis=0)
          for t in range(tiles)]

    def block(q2, blk, run, acc, masked):
        start = pl.multiple_of(blk * tq, tq)
        z = _dot_nt(q2, k_ref[pl.ds(start, tq), :])
        log_beta = jnp.minimum(z, 0.0) - jnp.log(1.0 + jnp.exp(-jnp.abs(z)))
        log_stay = log_beta - z
        if masked:
            log_stay = jnp.where(strict, log_stay, 0.0)
        after = _dot(jnp.concatenate(_split2(log_stay), axis=1), later2) + run
        w = jnp.exp(log_beta + after)
        if masked:
            w = jnp.where(strict, w, 0.0)
        acc = acc + _dot(w.astype(BF16), v_ref[pl.ds(start, tq), :])
        return run + jnp.sum(log_stay, axis=-1, keepdims=True), acc

    run0 = jnp.zeros((2 * tq, 1), F32)
    acc0 = jnp.zeros((2 * tq, LANES), F32)

    def opening(first_has_left):
        states = []
        for t in range(tiles):
            run, acc = block(qs[t], first + t, run0, acc0, True)
            if t > 0 or first_has_left:
                run, acc = block(qs[t], first + t - 1, run, acc, False)
            states.append((run, acc))
        return tuple(states)

    states = lax.cond(first > 0, functools.partial(opening, True), functools.partial(opening, False))

    def sweep_left(states):
        accs = []
        for t in range(tiles):

            def cond(c):
                left, run, _ = c
                return (left > 0) & (jnp.max(run) >= -SB_LOG_UNDERFLOW)

            def body(c, t=t):
                left, run, acc = c
                run, acc = block(qs[t], left - 1, run, acc, False)
                return left - 1, run, acc

            accs.append(lax.while_loop(cond, body, (jnp.maximum(first + t - 1, 0),) + states[t])[2])
        return tuple(accs)

    slowest = functools.reduce(jnp.maximum, [run for run, _ in states])
    accs = lax.cond(jnp.max(slowest) >= -SB_LOG_UNDERFLOW, sweep_left, lambda s: tuple(acc for _, acc in s), states)
    for t in range(tiles):
        o_ref[t * tq:(t + 1) * tq, :] = jnp.where(lane < HEAD_DIM, accs[t][:tq], accs[t][tq:]).astype(o_ref.dtype)


def _sb_attention(qkv, bsz, seq, tq, tiles):
    nq = seq // (tq * tiles)
    pairs = SB_WIDTH // LANES
    kern = functools.partial(_sb_attn_kernel, tq=tq, tiles=tiles)
    return pl.pallas_call(
        kern,
        grid=(bsz, pairs, nq),
        in_specs=[
            pl.BlockSpec((tq * tiles, LANES), lambda b, p, i: (b * nq + i, p)),
            pl.BlockSpec((seq, LANES), lambda b, p, i: (b, pairs + p)),
            pl.BlockSpec((seq, LANES), lambda b, p, i: (b, 2 * pairs + p)),
        ],
        out_specs=pl.BlockSpec((tq * tiles, LANES), lambda b, p, i: (b * nq + i, p)),
        out_shape=jax.ShapeDtypeStruct((bsz * seq, SB_WIDTH), BF16),
        compiler_params=_params("parallel", "parallel", "arbitrary"),
        name="stickbreak_attention",
    )(qkv, qkv, qkv)


def _ssd_kernel(z_ref, xs_ref, bc_ref, dt_ref, cwx_ref, cbx_ref, cwb_ref, cbb_ref, dtb_ref, alog_ref, dskip_ref,
                gn_ref, y_ref, xtail, btail, state):
    c = pl.program_id(1)
    cl = SSM_CHUNK
    halo = SUBLANES
    assert SSM_CONV == 4

    @pl.when(c == 0)
    def _():
        xtail[...] = jnp.zeros_like(xtail)
        btail[...] = jnp.zeros_like(btail)
        state[...] = jnp.zeros_like(state)

    def shift_rows(x, tail, k):
        rolled = pltpu.roll(x, k, 0)
        wrapped = pltpu.roll(tail, k, 0)
        row = lax.broadcasted_iota(jnp.int32, tail.shape, 0)
        return jnp.concatenate([jnp.where(row < k, wrapped, rolled[:halo]), rolled[halo:]], axis=0)

    def conv_silu(tails, raw_ref, w_ref, b_ref):
        x = raw_ref[...]
        x1 = shift_rows(x, tails[0], 1)
        u = w_ref[1:2, :] * x + w_ref[0:1, :] * x1
        y = b_ref[...] + w_ref[3:4, :] * x + w_ref[2:3, :] * x1 + shift_rows(u, tails[1], 2)
        tails[0] = x[cl - halo:, :]
        tails[1] = u[cl - halo:, :]
        return _silu(y)

    xs = conv_silu(xtail, xs_ref, cwx_ref, cbx_ref)
    bc = conv_silu(btail, bc_ref, cwb_ref, cbb_ref)

    dt = _softplus(dt_ref[...] + dtb_ref[...])
    da = dt * (-jnp.exp(alog_ref[...]))
    row = lax.broadcasted_iota(jnp.int32, (cl, cl), 0)
    col = lax.broadcasted_iota(jnp.int32, (cl, cl), 1)
    causal = col <= row
    tri = jnp.where(causal, 1.0, 0.0).astype(BF16)
    cs = _dot(jnp.concatenate([tri] * 3, axis=1), jnp.concatenate(_split3(da), axis=0))
    cs_t = cs.T
    ecs = jnp.exp(cs)
    dec_end = jnp.exp(cs[cl - 1:cl, :] - cs)

    head_of_lane = jnp.right_shift(lax.broadcasted_iota(jnp.int32, (LANES, SSM_INNER), 1), 6)
    expand = jnp.where(lax.broadcasted_iota(jnp.int32, (LANES, SSM_INNER), 0) == head_of_lane, 1.0, 0.0).astype(BF16)

    expand3 = jnp.concatenate([expand] * 3, axis=0)

    def per_head_lanes(v):
        return _dot(jnp.concatenate(_split3(v), axis=1), expand3)

    dt_e = per_head_lanes(dt)
    ecs_e = per_head_lanes(ecs)
    dec_end_e = per_head_lanes(dec_end)

    xdt = xs * dt_e
    xdt_b = xdt.astype(BF16)
    xw_b = (xdt * dec_end_e).astype(BF16)
    lane = lax.broadcasted_iota(jnp.int32, (cl, LANES), 1)
    first = lane < SSM_HEADDIM
    gs = SSM_STATE
    heads_per_group = SSM_HEADS // SSM_GROUPS
    for g in range(SSM_GROUPS):
        bg = bc[:, g * gs:(g + 1) * gs]
        cg_b = bc[:, (SSM_GROUPS + g) * gs:(SSM_GROUPS + g + 1) * gs].astype(BF16)
        cb = _dot_nt(cg_b, bg.astype(BF16))
        cols = slice(g * SSM_GROUP_WIDTH, (g + 1) * SSM_GROUP_WIDTH)
        prev = state[:, cols]
        y_off = _dot(cg_b, prev.astype(BF16)) * ecs_e[:, cols]
        state[:, cols] = prev * ecs_e[cl - 1:cl, cols] + _dot(bg.T.astype(BF16), xw_b[:, cols])
        for pr in range(heads_per_group // 2):
            pcols = slice(g * SSM_GROUP_WIDTH + pr * LANES, g * SSM_GROUP_WIDTH + (pr + 1) * LANES)
            x_pair = xdt_b[:, pcols]
            ys = []
            for hh in range(2):
                h = g * heads_per_group + 2 * pr + hh
                seg = jnp.where(causal, cs[:, h:h + 1] - cs_t[h:h + 1, :], NEG_BIG)
                ys.append(_dot((cb * jnp.exp(seg)).astype(BF16), x_pair))
            y_diag = jnp.where(first, ys[0], ys[1])
            y_pair = y_diag + y_off[:, pr * LANES:(pr + 1) * LANES] + xs[:, pcols] * dskip_ref[:, pcols]
            y_ref[:, pcols] = y_pair * _silu(z_ref[:, pcols])
    for g in range(SSM_GROUPS):
        cols = slice(g * SSM_GROUP_WIDTH, (g + 1) * SSM_GROUP_WIDTH)
        y_ref[:, cols] = _rms(y_ref[:, cols], gn_ref[:, cols])


def _ssd(proj, bsz, seq, cwx, cbx, cwb, cbb, dt_bias, a_log, d_skip, gnorm):
    cl = SSM_CHUNK
    nc = seq // cl

    def rows(width, colblk):
        return pl.BlockSpec((cl, width), lambda b, c: (b * nc + c, colblk))

    def const(r, width):
        return pl.BlockSpec((r, width), lambda b, c: (0, 0))

    return pl.pallas_call(
        _ssd_kernel,
        grid=(bsz, nc),
        in_specs=[
            rows(SSM_INNER, 0),
            rows(SSM_INNER, 1),
            rows(SSM_BC_WIDTH, 2 * SSM_INNER // SSM_BC_WIDTH),
            rows(LANES, (2 * SSM_INNER + SSM_BC_WIDTH) // LANES),
            const(SSM_CONV, SSM_INNER), const(1, SSM_INNER), const(SSM_CONV, SSM_BC_WIDTH), const(1, SSM_BC_WIDTH),
            const(1, LANES), const(1, LANES), const(1, SSM_INNER), const(1, SSM_INNER),
        ],
        out_specs=pl.BlockSpec((cl, SSM_INNER), lambda b, c: (b * nc + c, 0)),
        out_shape=jax.ShapeDtypeStruct((bsz * seq, SSM_INNER), F32),
        scratch_shapes=[
            pltpu.VMEM((2, SUBLANES, SSM_INNER), F32),
            pltpu.VMEM((2, SUBLANES, SSM_BC_WIDTH), F32),
            pltpu.VMEM((SSM_STATE, SSM_INNER), F32),
        ],
        compiler_params=_params("parallel", "arbitrary"),
        name="ssd_scan",
    )(proj, proj, proj, proj, cwx, cbx, cwb, cbb, dt_bias, a_log, d_skip, gnorm)


def _mix_ffn_kernel(*refs, tm, tiles_per_seq, final_norm):
    x_ref, a_ref, b_ref, wa_ref, wb_ref, g_ref, wg_ref, wu_ref, cw_ref, cb_ref, wd_ref = refs[:11]
    pos = 11
    fn_ref = None
    if final_norm:
        fn_ref = refs[pos]
        pos += 1
    o_ref, gbuf, act_ref = refs[pos:pos + 3]
    halo = SUBLANES

    o_ref[...] = x_ref[...] + _dot(a_ref[...].astype(BF16), wa_ref[...]) + _dot(b_ref[...].astype(BF16), wb_ref[...])
    hn = _rms(o_ref[...], g_ref[...]).astype(BF16)

    @pl.when(pl.program_id(0) % tiles_per_seq == 0)
    def _():
        gbuf[0:halo, :] = jnp.zeros((halo, gbuf.shape[1]), F32)

    step = MXU_WIDTH
    for c0 in range(0, FFN_DIM, step):
        cols = slice(c0, c0 + step)
        gate = _dot(hn, wg_ref[:, cols])
        up = _dot(hn, wu_ref[:, cols])
        gbuf[halo:halo + tm, cols] = gate
        conv = cb_ref[:, cols] + cw_ref[FFN_CONV - 1:FFN_CONV, cols] * gate
        for t in range(FFN_CONV - 1):
            back = FFN_CONV - 1 - t
            conv = conv + cw_ref[t:t + 1, cols] * gbuf[halo - back:halo - back + tm, cols]
        gbuf[0:halo, cols] = gbuf[tm:tm + halo, cols]
        act_ref[:, cols] = (_silu(conv) * up).astype(BF16)

    y = o_ref[...] + _dot(act_ref[...], wd_ref[...])
    if final_norm:
        y = _rms(y, fn_ref[...])
    o_ref[...] = y


def _mix_ffn(x, a, b, wa, wb, g, wg, wu, cw, cb, wd, seq, *, tm, final_gain=None):
    t, d = x.shape
    f = wg.shape[1]
    final_norm = final_gain is not None
    row = lambda w: pl.BlockSpec((tm, w), lambda i: (i, 0))
    in_specs = [row(d), row(a.shape[1]), row(b.shape[1]), _resident(wa.shape), _resident(wb.shape),
                _resident(g.shape), _resident(wg.shape), _resident(wu.shape), _resident(cw.shape),
                _resident(cb.shape), _resident(wd.shape)]
    args = [x, a, b, wa, wb, g, wg, wu, cw, cb, wd]
    if final_norm:
        in_specs.append(_resident(final_gain.shape))
        args.append(final_gain)
    kern = functools.partial(_mix_ffn_kernel, tm=tm, tiles_per_seq=seq // tm, final_norm=final_norm)
    return pl.pallas_call(
        kern,
        grid=(t // tm,),
        in_specs=in_specs,
        out_specs=row(d),
        out_shape=jax.ShapeDtypeStruct((t, d), F32),
        scratch_shapes=[pltpu.VMEM((tm + SUBLANES, f), F32), pltpu.VMEM((tm, f), BF16)],
        compiler_params=_params("arbitrary"),
        name="mix_ffn",
    )(*args)


def _row(v):
    return v.reshape(1, -1).astype(F32)


def _pad_cols(w, n):
    return jnp.pad(w, ((0, 0), (0, n - w.shape[1])))


def _rope_lane_constants():
    lane = jnp.arange(LANES)
    half_a = HEAD_DIM // 2
    inv_a = 1.0 / (ROPE_THETA ** (jnp.arange(half_a, dtype=F32) * (2.0 / HEAD_DIM)))
    freq_a = inv_a[lane % half_a]
    sign_a = jnp.where((lane % HEAD_DIM) < half_a, -1.0, 1.0).astype(F32)
    half_b = MLA_ROPE // 2
    assert HEAD_DIM == 2 * MLA_ROPE
    in_rope = (lane >= MLA_NOPE) & (lane < MLA_NOPE + MLA_ROPE)
    sign_b = jnp.where(in_rope, jnp.where((lane - MLA_NOPE) < half_b, -1.0, 1.0), 0.0).astype(F32)
    rest_b = jnp.where(in_rope, 0.0, 1.0).astype(F32)
    src = jnp.arange(LANES)[:, None]
    packed_tok = jnp.arange(ROPE_PACK)[:, None, None]
    sel_a = (src == packed_tok * half_a + lane % half_a).astype(BF16)
    sel_b = (in_rope & (src == packed_tok * half_a + 2 * ((lane - MLA_NOPE) % half_b))).astype(BF16)
    return _row(freq_a), sel_a, sel_b, _row(sign_a), _row(sign_b), _row(rest_b)


def _mla_slot_weights(w_uq, w_ukv):
    rq = w_uq.shape[0]
    wq = w_uq.reshape(rq, MLA_HEADS, MLA_NOPE + MLA_ROPE)
    wq = jnp.pad(wq, ((0, 0), (0, 0), (0, MLA_SLOT - MLA_NOPE - MLA_ROPE))).reshape(rq, MLA_WIDTH)
    rk = w_ukv.shape[0]
    wkv = w_ukv.reshape(rk, MLA_HEADS, MLA_NOPE + MLA_V)
    wk = jnp.pad(wkv[:, :, :MLA_NOPE], ((0, 0), (0, 0), (0, MLA_SLOT - MLA_NOPE))).reshape(rk, MLA_WIDTH)
    wv = jnp.pad(wkv[:, :, MLA_NOPE:], ((0, 0), (0, 0), (0, MLA_SLOT - MLA_V))).reshape(rk, MLA_WIDTH)
    return wq.astype(BF16), jnp.concatenate([wk, wv], axis=1).astype(BF16)


def kernel(x, positions,
           l0_norm_mix, l0_w_in, l0_mla_q_norm, l0_mla_w_uq, l0_mla_kv_norm, l0_mla_w_ukv, l0_w_out,
           l0_norm_ffn, l0_ffn_w_gate, l0_ffn_w_up, l0_ffn_conv_w, l0_ffn_conv_b, l0_ffn_w_down,
           l1_norm_mix, l1_w_in, l1_ssm_conv_w, l1_ssm_conv_b, l1_ssm_dt_bias, l1_ssm_a_log, l1_ssm_d,
           l1_ssm_norm, l1_w_out,
           l1_norm_ffn, l1_ffn_w_gate, l1_ffn_w_up, l1_ffn_conv_w, l1_ffn_conv_b, l1_ffn_w_down,
           final_norm):
    bsz, seq, d = x.shape
    t = bsz * seq
    xf = x.reshape(t, d)
    tm = ROW_TILE
    span = DSW_BLK * max(dil for _, dil in DSW_GROUPS)
    assert d == D_MODEL and seq % tm == 0 and seq % span == 0 and seq % MLA_TQ == 0
    assert seq % (SB_TQ * SB_TILES_PER_STEP) == 0 and seq % SSM_CHUNK == 0 and t % (ROPE_PACK * tm) == 0

    pos_packed = jnp.repeat(positions.reshape(t // ROPE_PACK, ROPE_PACK).astype(F32), LANES // ROPE_PACK, axis=1)
    cos_a, sin_a, cos_b, sin_b = _rope_tables(pos_packed, *_rope_lane_constants(), rows=tm)

    nd = DSW_WIDTH
    gw = DSW_GROUP_WIDTH
    w_qkv = jnp.concatenate([l0_w_in[:, part * nd + gi * gw:part * nd + (gi + 1) * gw]
                             for gi in range(len(DSW_GROUPS)) for part in range(3)], axis=1).astype(BF16)
    w_cq = l0_w_in[:, 3 * nd:3 * nd + MLA_Q_RANK]
    w_ckv = l0_w_in[:, 3 * nd + MLA_Q_RANK:3 * nd + MLA_Q_RANK + MLA_KV_RANK]
    w_kpe = l0_w_in[:, 3 * nd + MLA_Q_RANK + MLA_KV_RANK:]
    w_kpe_slot = jnp.pad(w_kpe, ((0, 0), (MLA_NOPE, MLA_SLOT - MLA_NOPE - MLA_ROPE)))
    w_lat = jnp.concatenate([w_cq, w_ckv, w_kpe_slot], axis=1).astype(BF16)
    wq_slot, wkv_slot = _mla_slot_weights(l0_mla_w_uq, l0_mla_w_ukv)
    g0, g1, g2, q_mla, kv_mla = _l0_in_proj(xf, _row(l0_norm_mix), w_qkv, w_lat, _row(l0_mla_q_norm), wq_slot,
                                            _row(l0_mla_kv_norm), wkv_slot, cos_a, sin_a, cos_b, sin_b,
                                            bsz, seq, tm)
    y_a = _dilated_attention(g0, g1, g2, bsz, seq)
    y_b = _mla_attention(q_mla, kv_mla, bsz, seq, tq=MLA_TQ, tk=MLA_TK, heads=MLA_HEADS_PER_STEP)

    w_out0 = l0_w_out.astype(BF16)
    x2 = _mix_ffn(xf, y_a, y_b, w_out0[:DSW_GROUP_WIDTH], w_out0[DSW_GROUP_WIDTH:], _row(l0_norm_ffn),
                  l0_ffn_w_gate.astype(BF16), l0_ffn_w_up.astype(BF16), l0_ffn_conv_w, _row(l0_ffn_conv_b),
                  l0_ffn_w_down.astype(BF16), seq, tm=tm)

    o_dt = 2 * SSM_INNER + SSM_BC_WIDTH
    w_ssm = _pad_cols(l1_w_in[:, :o_dt + SSM_HEADS], SSM_PROJ_WIDTH).astype(BF16)
    w_sb = l1_w_in[:, o_dt + SSM_HEADS:].astype(BF16)
    proj, qkv_sb = _l1_in_proj(x2, _row(l1_norm_mix), w_ssm, w_sb, tm)

    cw = l1_ssm_conv_w
    cb = _row(l1_ssm_conv_b)
    y_c = _ssd(proj, bsz, seq, cw[:, :SSM_INNER], cb[:, :SSM_INNER], cw[:, SSM_INNER:], cb[:, SSM_INNER:],
               _pad_cols(_row(l1_ssm_dt_bias), LANES), _pad_cols(_row(l1_ssm_a_log), LANES),
               _row(jnp.repeat(l1_ssm_d, SSM_HEADDIM)), _row(l1_ssm_norm))
    y_d = _sb_attention(qkv_sb, bsz, seq, tq=SB_TQ, tiles=SB_TILES_PER_STEP)

    w_out1 = l1_w_out.astype(BF16)
    out = _mix_ffn(x2, y_c, y_d, w_out1[:SSM_INNER], w_out1[SSM_INNER:], _row(l1_norm_ffn),
                   l1_ffn_w_gate.astype(BF16), l1_ffn_w_up.astype(BF16), l1_ffn_conv_w, _row(l1_ffn_conv_b),
                   l1_ffn_w_down.astype(BF16), seq, tm=tm, final_gain=_row(final_norm))
    return out.reshape(bsz, seq, d)
```

```python
import functools
import math

import jax
import jax.numpy as jnp
from jax import lax
from jax.experimental import pallas as pl
from jax.experimental.pallas import tpu as pltpu

F32 = jnp.float32
BF16 = jnp.bfloat16

LANES = 128
SUBLANES = 8
MXU_WIDTH = 256
VMEM_LIMIT_BYTES = 56 * 1024 * 1024

D_MODEL = 1024
HEAD_DIM = 64
ROPE_THETA = 10000.0
ROPE_PACK = 4
NORM_EPS = 1e-6

DSW_GROUPS = ((128, 1), (512, 4), (2048, 16))
DSW_HEADS_PER_GROUP = 4
DSW_HEADS = DSW_HEADS_PER_GROUP * len(DSW_GROUPS)
DSW_BLK = 128
DSW_WIDTH = DSW_HEADS * HEAD_DIM
DSW_GROUP_WIDTH = DSW_HEADS_PER_GROUP * HEAD_DIM
DSW_UNITS_PER_BODY = 4

MLA_HEADS = 12
MLA_Q_RANK = 256
MLA_KV_RANK = 128
MLA_NOPE = 64
MLA_ROPE = 32
MLA_V = 64
MLA_SLOT = 128
MLA_WIDTH = MLA_HEADS * MLA_SLOT
MLA_LAT_WIDTH =MLA_Q_RANK + MLA_KV_RANK + MLA_SLOT
MLA_Q_SCALE = (MLA_NOPE + MLA_ROPE) ** -0.5 * math.log2(math.e)

SSM_INNER = 1024
SSM_HEADDIM = 64
SSM_HEADS = 16
SSM_STATE = 128
SSM_GROUPS = 2
SSM_CONV = 4
SSM_CHUNK = 128
SSM_GROUP_WIDTH = SSM_INNER // SSM_GROUPS
SSM_BC_WIDTH = 2 * SSM_GROUPS * SSM_STATE
SSM_DT_PAD = LANES
SSM_PROJ_WIDTH = 2 * SSM_INNER + SSM_BC_WIDTH + SSM_DT_PAD

SB_HEADS = 8
SB_WIDTH = SB_HEADS * HEAD_DIM
SB_LOG_UNDERFLOW = 104.0

FFN_DIM = 2816
FFN_CONV = 3

NEG_BIG = -1e30

ROW_TILE = 512
MLA_TQ, MLA_TK, MLA_HEADS_PER_STEP = 1024, 512, 4
SB_TQ, SB_TILES_PER_STEP = 256, 4


def _params(*sem):
    return pltpu.CompilerParams(dimension_semantics=sem, vmem_limit_bytes=VMEM_LIMIT_BYTES)


def _rms(x, g):
    return x * lax.rsqrt(jnp.mean(x * x, axis=-1, keepdims=True) + NORM_EPS) * g


def _silu(x):
    return x * (1.0 / (1.0 + jnp.exp(-x)))


def _log1p(e):
    u = 1.0 + e
    return jnp.where(u == 1.0, e, jnp.log(u) * (e / (u - 1.0)))


def _softplus(x):
    return jnp.maximum(x, 0.0) + _log1p(jnp.exp(-jnp.abs(x)))


def _swap_halves(x, half):
    lane = lax.broadcasted_iota(jnp.int32, x.shape, 1)
    up = pltpu.roll(x, LANES - half, 1)
    down = pltpu.roll(x, half, 1)
    return jnp.where((lane & half) == 0, up, down)


def _rope_tile(x, cos, sin, half):
    return x * cos + _swap_halves(x, half) * sin


def _split2(x):
    hi = x.astype(BF16)
    return hi, (x - hi.astype(F32)).astype(BF16)


def _split3(x):
    hi = x.astype(BF16)
    r1 = x - hi.astype(F32)
    mid = r1.astype(BF16)
    lo = (r1 - mid.astype(F32)).astype(BF16)
    return hi, mid, lo


def _dot(a, b):
    return jnp.dot(a, b, preferred_element_type=F32)


def _dot_nt(a, b):
    return lax.dot_general(a, b, (((1,), (1,)), ((), ())), preferred_element_type=F32)


def _keep_head(x2, lane, second):
    mine = (lane >= HEAD_DIM) if second else (lane < HEAD_DIM)
    return jnp.where(mine, x2.astype(F32), 0.0).astype(BF16)


def _resident(shape):
    return pl.BlockSpec(shape, lambda *_: (0,) * len(shape), pipeline_mode=pl.Buffered(1))


def _rope_table_kernel(pos_ref, freq_ref, sela_ref, selb_ref, signa_ref, signb_ref, restb_ref,
                       cosa_ref, sina_ref, cosb_ref, sinb_ref):
    ang = pos_ref[...] * freq_ref[...]
    cos3 = jnp.concatenate(_split3(jnp.cos(ang)), axis=1)
    sin3 = jnp.concatenate(_split3(jnp.sin(ang)), axis=1)
    rows = ang.shape[0]

    def spread(parts, sel):
        return _dot(parts, jnp.concatenate([sel] * 3, axis=0))

    for g in range(ROPE_PACK):
        tokens = pl.ds(g, rows, stride=ROPE_PACK)
        cosa_ref[tokens, :] = spread(cos3, sela_ref[g])
        sina_ref[tokens, :] = spread(sin3, sela_ref[g]) * signa_ref[...]
        cosb_ref[tokens, :] = spread(cos3, selb_ref[g]) + restb_ref[...]
        sinb_ref[tokens, :] = spread(sin3, selb_ref[g]) * signb_ref[...]


def _rope_tables(pos_packed, freq, sel_a, sel_b, sign_a, sign_b, rest_b, rows):
    packed = pos_packed.shape[0]
    t = packed * ROPE_PACK
    const = pl.BlockSpec((1, LANES), lambda i: (0, 0))
    sel = pl.BlockSpec((ROPE_PACK, LANES, LANES), lambda i: (0, 0, 0))
    out = pl.BlockSpec((rows * ROPE_PACK, LANES), lambda i: (i, 0))
    return pl.pallas_call(
        _rope_table_kernel,
        grid=(packed // rows,),
        in_specs=[pl.BlockSpec((rows, LANES), lambda i: (i, 0)), const, sel, sel, const, const, const],
        out_specs=[out] * 4,
        out_shape=[jax.ShapeDtypeStruct((t, LANES), F32)] * 4,
        compiler_params=_params("parallel"),
        name="rope_tables",
    )(pos_packed, freq, sel_a, sel_b, sign_a, sign_b, rest_b)


def _l0_in_kernel(x_ref, g_ref, wqkv_ref, wlat_ref, gq_ref, wq_ref, gkv_ref, wkv_ref,
                  cosa_ref, sina_ref, cosb_ref, sinb_ref, g0_ref, g1_ref, g2_ref, qm_ref, kvm_ref, slabs):
    tm = x_ref.shape[0]
    hn = _rms(x_ref[...], g_ref[...]).astype(BF16)
    cos_a, sin_a = cosa_ref[...], sina_ref[...]
    cos_b, sin_b = cosb_ref[...], sinb_ref[...]
    half_a, half_b = HEAD_DIM // 2, MLA_ROPE // 2
    step = MXU_WIDTH
    halves = step // LANES
    group_refs = (g0_ref, g1_ref, g2_ref)
    slab = 0
    for gi, (_, dil) in enumerate(DSW_GROUPS):
        for part in range(3):
            c0 = (3 * gi + part) * DSW_GROUP_WIDTH
            acc = _dot(hn, wqkv_ref[:, c0:c0 + step])
            for h in range(halves):
                tile = acc[:, h * LANES:(h + 1) * LANES]
                if part < 2:
                    tile = _rope_tile(tile, cos_a, sin_a, half_a)
                if part == 0:
                    tile = tile * HEAD_DIM ** -0.5
                col = part * DSW_GROUP_WIDTH + h * LANES
                if dil == 1:
                    g0_ref[:, col:col + LANES] = tile.astype(BF16)
                else:
                    slabs[slab] = tile
                    for r in range(dil):
                        rows = slabs[slab, pl.ds(r, tm // dil, stride=dil), :]
                        group_refs[gi][r, :, col:col + LANES] = rows.astype(BF16)
                    slab += 1

    lat = _dot(hn, wlat_ref[...])
    cq = _rms(lat[:, :MLA_Q_RANK], gq_ref[...]).astype(BF16)
    ckv = _rms(lat[:, MLA_Q_RANK:MLA_Q_RANK + MLA_KV_RANK], gkv_ref[...]).astype(BF16)
    kpe = _rope_tile(lat[:, MLA_Q_RANK + MLA_KV_RANK:], cos_b, sin_b, half_b)
    for c0 in range(0, MLA_WIDTH, step):
        acc = _dot(cq, wq_ref[:, c0:c0 + step])
        for h in range(halves):
            sl = slice(h * LANES, (h + 1) * LANES)
            tile = _rope_tile(acc[:, sl], cos_b, sin_b, half_b) * MLA_Q_SCALE
            qm_ref[:, c0 + h * LANES:c0 + (h + 1) * LANES] = tile.astype(BF16)
    lane = lax.broadcasted_iota(jnp.int32, kpe.shape, 1)
    one_hot = jnp.where(lane == MLA_V, 1.0, 0.0)
    for c0 in range(0, 2 * MLA_WIDTH, step):
        acc = _dot(ckv, wkv_ref[:, c0:c0 + step])
        extra = kpe if c0 < MLA_WIDTH else one_hot
        for h in range(halves):
            sl = slice(h * LANES, (h + 1) * LANES)
            kvm_ref[:, c0 + h * LANES:c0 + (h + 1) * LANES] = (acc[:, sl] + extra).astype(BF16)


def _l0_in_proj(x, g, w_qkv, w_lat, gq, wq, gkv, wkv, cos_a, sin_a, cos_b, sin_b, bsz, seq, tm):
    t, d = x.shape
    tps = seq // tm
    gw = 3 * DSW_GROUP_WIDTH
    row = lambda w: pl.BlockSpec((tm, w), lambda i: (i, 0))

    def residue_major(dil):
        return pl.BlockSpec((None, dil, tm // dil, gw), lambda i: (i // tps, 0, i % tps, 0))

    dils = [dil for _, dil in DSW_GROUPS]
    n_slabs = sum(gw // LANES for dil in dils if dil > 1)
    return pl.pallas_call(
        _l0_in_kernel,
        grid=(t // tm,),
        in_specs=[row(d), _resident((1, d)), _resident(w_qkv.shape), _resident(w_lat.shape),
                  _resident(gq.shape), _resident(wq.shape), _resident(gkv.shape), _resident(wkv.shape),
                  row(LANES), row(LANES), row(LANES), row(LANES)],
        out_specs=[row(gw), residue_major(dils[1]), residue_major(dils[2]), row(MLA_WIDTH), row(2 * MLA_WIDTH)],
        out_shape=[jax.ShapeDtypeStruct((t, gw), BF16),
                   jax.ShapeDtypeStruct((bsz, dils[1], seq // dils[1], gw), BF16),
                   jax.ShapeDtypeStruct((bsz, dils[2], seq // dils[2], gw), BF16),
                   jax.ShapeDtypeStruct((t, MLA_WIDTH), BF16),
                   jax.ShapeDtypeStruct((t, 2 * MLA_WIDTH), BF16)],
        scratch_shapes=[pltpu.VMEM((n_slabs, tm, LANES), F32)],
        compiler_params=_params("parallel"),
        name="l0_in_proj",
    )(x, g, w_qkv, w_lat, gq, wq, gkv, wkv, cos_a, sin_a, cos_b, sin_b)


def _l1_in_kernel(x_ref, g_ref, wssm_ref, wsb_ref, proj_ref, qkv_ref):
    hn = _rms(x_ref[...], g_ref[...]).astype(BF16)
    step = MXU_WIDTH
    for c0 in range(0, SSM_PROJ_WIDTH, step):
        c1 = min(c0 + step, SSM_PROJ_WIDTH)
        proj_ref[:, c0:c1] = _dot(hn, wssm_ref[:, c0:c1])
    for c0 in range(0, 3 * SB_WIDTH, step):
        acc = _dot(hn, wsb_ref[:, c0:c0 + step])
        if c0 < SB_WIDTH:
            acc = acc * HEAD_DIM ** -0.5
        qkv_ref[:, c0:c0 + step] = acc.astype(BF16)


def _l1_in_proj(x, g, w_ssm, w_sb, tm):
    t, d = x.shape
    row = lambda w: pl.BlockSpec((tm, w), lambda i: (i, 0))
    return pl.pallas_call(
        _l1_in_kernel,
        grid=(t // tm,),
        in_specs=[row(d), _resident((1, d)), _resident(w_ssm.shape), _resident(w_sb.shape)],
        out_specs=[row(SSM_PROJ_WIDTH), row(3 * SB_WIDTH)],
        out_shape=[jax.ShapeDtypeStruct((t, SSM_PROJ_WIDTH), F32),
                   jax.ShapeDtypeStruct((t, 3 * SB_WIDTH), BF16)],
        compiler_params=_params("parallel"),
        name="l1_in_proj",
    )(x, g, w_ssm, w_sb)


def _dilated_kernel(c0_ref, p0_ref, c1_ref, p1_ref, c2_ref, p2_ref, y_ref, o_acc, m_acc, l_acc):
    blk = DSW_BLK
    gw = DSW_GROUP_WIDTH
    pairs = gw // LANES
    qcols, kcols, vcols = slice(0, gw), slice(gw, 2 * gw), slice(2 * gw, 3 * gw)
    first_key = jnp.where(pl.program_id(1) > 0, 0, blk)
    qi = lax.broadcasted_iota(jnp.int32, (blk, 2 * blk), 0)
    kj = lax.broadcasted_iota(jnp.int32, (blk, 2 * blk), 1)
    window = (kj >= qi) & (kj <= qi + blk)
    lane = lax.broadcasted_iota(jnp.int32, (blk, LANES), 1)
    first = lane < HEAD_DIM

    def unit(q, k2, v2, lowest_key):
        valid = window & (kj >= lowest_key)
        res = []
        for pair in range(pairs):
            sl = slice(pair * LANES, (pair + 1) * LANES)
            kk, vv = k2[:, sl], v2[:, sl]
            outs, maxes, sums = [], [], []
            for hh in range(2):
                s = jnp.where(valid, _dot_nt(_keep_head(q[:, sl], lane, hh), kk), NEG_BIG)
                m = jnp.max(s, axis=-1, keepdims=True)
                p = jnp.exp(s - m)
                outs.append(_dot(p.astype(BF16), vv))
                maxes.append(jnp.broadcast_to(m, (blk, LANES)))
                sums.append(jnp.broadcast_to(jnp.sum(p, axis=-1, keepdims=True), (blk, LANES)))
            res.append(tuple(jnp.where(first, a, b) for a, b in (outs, maxes, sums)))
        return res

    def put(rows, res):
        for pair, (o, m, l) in enumerate(res):
            o_acc[pair, rows, :] = o
            m_acc[pair, rows, :] = m
            l_acc[pair, rows, :] = l

    def merge(rows, res):
        for pair, (o, m, l) in enumerate(res):
            m_old = m_acc[pair, rows, :]
            m_new = jnp.maximum(m_old, m)
            keep, add = jnp.exp(m_old - m_new), jnp.exp(m - m_new)
            o_acc[pair, rows, :] = o_acc[pair, rows, :] * keep + o * add
            l_acc[pair, rows, :] = l_acc[pair, rows, :] * keep + l * add
            m_acc[pair, rows, :] = m_new

    def stacked(prev, cur):
        return jnp.concatenate([prev, cur], axis=0)

    head = slice(0, blk)

    def head_unit(cur, prev):
        return unit(cur(head, qcols), stacked(prev(kcols), cur(head, kcols)),
                    stacked(prev(vcols), cur(head, vcols)), first_key)

    def later_unit(cur, r0):
        keys = pl.ds(r0 - blk, 2 * blk)
        return unit(cur(pl.ds(r0, blk), qcols), cur(keys, kcols), cur(keys, vcols), 0)

    units_per_body = DSW_UNITS_PER_BODY

    cur0 = lambda rows, cols: c0_ref[rows, cols]
    put(head, head_unit(cur0, lambda cols: p0_ref[:, cols]))
    later_blocks = c0_ref.shape[0] // blk - 1
    group0 = 5
    assert later_blocks % group0 == 0

    def dense_blocks(u, carry):
        for k in range(group0):
            r0 = pl.multiple_of((1 + u * group0 + k) * blk, blk)
            put(pl.ds(r0, blk), later_unit(cur0, r0))
        return carry

    lax.fori_loop(0, later_blocks // group0, dense_blocks, 0)

    for cur_ref, prev_ref in ((c1_ref, p1_ref), (c2_ref, p2_ref)):
        dil, blocks = cur_ref.shape[0], cur_ref.shape[1] // blk
        residues_per_body = max(1, units_per_body // blocks)
        assert dil % residues_per_body == 0

        def residues(u, carry, cur_ref=cur_ref, prev_ref=prev_ref, dil=dil, blocks=blocks,
                     residues_per_body=residues_per_body):
            for k in range(residues_per_body):
                r = u * residues_per_body + k
                cur = lambda rows, cols, r=r: cur_ref[r, rows, cols]
                merge(pl.ds(r, blk, stride=dil), head_unit(cur, lambda cols, r=r: prev_ref[r, :, cols]))
                for n in range(1, blocks):
                    merge(pl.ds(n * blk * dil + r, blk, stride=dil), later_unit(cur, n * blk))
            return carry

        lax.fori_loop(0, dil // residues_per_body, residues, 0)

    for pair in range(pairs):
        y_ref[:, pair * LANES:(pair + 1) * LANES] = (o_acc[pair] * (1.0 / l_acc[pair])).astype(y_ref.dtype)


def _dilated_attention(g0, g1, g2, bsz, seq):
    blk = DSW_BLK
    span = blk * max(dil for _, dil in DSW_GROUPS)
    nspan = seq // span
    gw = 3 * DSW_GROUP_WIDTH
    d1, d2 = g1.shape[1], g2.shape[1]
    blocks_per_span0 = span // blk

    def residue_major(dil, rows, prev):
        per_span = span // dil // rows
        if prev:
            return pl.BlockSpec((None, dil, rows, gw), lambda b, s: (b, 0, jnp.maximum(s * per_span - 1, 0), 0))
        return pl.BlockSpec((None, dil, rows, gw), lambda b, s: (b, 0, s, 0))

    return pl.pallas_call(
        _dilated_kernel,
        grid=(bsz, nspan),
        in_specs=[
            pl.BlockSpec((span, gw), lambda b, s: (b * nspan + s, 0)),
            pl.BlockSpec((blk, gw), lambda b, s: (jnp.maximum((b * nspan + s) * blocks_per_span0 - 1, 0), 0)),
            residue_major(d1, span // d1, False), residue_major(d1, blk, True),
            residue_major(d2, span // d2, False), residue_major(d2, blk, True),
        ],
        out_specs=pl.BlockSpec((span, DSW_GROUP_WIDTH), lambda b, s: (b * nspan + s, 0)),
        out_shape=jax.ShapeDtypeStruct((bsz * seq, DSW_GROUP_WIDTH), BF16),
        scratch_shapes=[pltpu.VMEM((DSW_GROUP_WIDTH // LANES, span, LANES), F32)] * 3,
        compiler_params=_params("parallel", "arbitrary"),
        name="dilated_attention",
    )(g0, g0, g1, g1, g2, g2)


def _mla_attn_kernel(q_ref, k_ref, v_ref, o_ref, *, tq, tk, heads):
    i = pl.program_id(2)
    assert (tq // tk) % 2 == 0 and heads % 2 == 0
    lane = lax.broadcasted_iota(jnp.int32, (tq, LANES), 1)
    slots = [slice(hh * MLA_SLOT, (hh + 1) * MLA_SLOT) for hh in range(heads)]

    def update(q, m, acc, start, width, slot, visible=None):
        s = _dot_nt(q, k_ref[pl.ds(start, width), slot])
        if visible is not None:
            s = jnp.where(visible, s, NEG_BIG)
        m_new = jnp.maximum(m, jnp.max(s, axis=-1, keepdims=True))
        p = jnp.exp2(s - m_new)
        return m_new, jnp.exp2(m - m_new) * acc + _dot(p.astype(BF16), v_ref[pl.ds(start, width), slot])

    def block(j, carry):
        start = pl.multiple_of(j * tk, tk)
        return tuple(update(q_ref[:, slots[hh]], *carry[hh], start, tk, slots[hh]) for hh in range(heads))

    def block_pair(jp, carry):
        return block(2 * jp + 1, block(2 * jp, carry))

    init = tuple((jnp.full((tq, 1), NEG_BIG, F32), jnp.zeros((tq, LANES), F32)) for _ in range(heads))
    diag_blocks = tq // tk
    carry = lax.fori_loop(0, i * (diag_blocks // 2), block_pair, init)
    for b in range(diag_blocks):
        r0 = b * tk
        row = lax.broadcasted_iota(jnp.int32, (tq - r0, tk), 0)
        col = lax.broadcasted_iota(jnp.int32, (tq - r0, tk), 1)
        start = pl.multiple_of((i * diag_blocks + b) * tk, tk)
        new = []
        for hh in range(heads):
            m, acc = carry[hh]
            m_low, acc_low = update(q_ref[r0:, slots[hh]], m[r0:], acc[r0:], start, tk, slots[hh], col <= row)
            if r0:
                m_low = jnp.concatenate([m[:r0], m_low], axis=0)
                acc_low = jnp.concatenate([acc[:r0], acc_low], axis=0)
            new.append((m_low, acc_low))
        carry = tuple(new)
    outs = [acc * (1.0 / acc[:, MLA_V:MLA_V + 1]) for _, acc in carry]
    for pr in range(heads // 2):
        packed = jnp.where(lane < MLA_V, outs[2 * pr], pltpu.roll(outs[2 * pr + 1], MLA_V, 1))
        o_ref[:, pr * LANES:(pr + 1) * LANES] = packed.astype(o_ref.dtype)


def _mla_attention(q, kv, bsz, seq, tq, tk, heads):
    nq = seq // tq
    groups = MLA_HEADS // heads
    kern = functools.partial(_mla_attn_kernel, tq=tq, tk=tk, heads=heads)
    return pl.pallas_call(
        kern,
        grid=(bsz, groups, nq),
        in_specs=[
            pl.BlockSpec((tq, heads * MLA_SLOT), lambda b, p, i: (b * nq + i, p)),
            pl.BlockSpec((seq, heads * MLA_SLOT), lambda b, p, i: (b, p)),
            pl.BlockSpec((seq, heads * MLA_SLOT), lambda b, p, i: (b, groups + p)),
        ],
        out_specs=pl.BlockSpec((tq, heads * MLA_V), lambda b, p, i: (b * nq + i, p)),
        out_shape=jax.ShapeDtypeStruct((bsz * seq, MLA_HEADS * MLA_V), BF16),
        compiler_params=_params("parallel", "parallel", "arbitrary"),
        name="mla_attention",
    )(q, kv, kv)


def _sb_attn_kernel(q_ref, k_ref, v_ref, o_ref, *, tq, tiles):
    first = pl.program_id(2) * tiles
    lane = lax.broadcasted_iota(jnp.int32, (tq, LANES), 1)
    row = lax.broadcasted_iota(jnp.int32, (tq, tq), 0)
    col = lax.broadcasted_iota(jnp.int32, (tq, tq), 1)
    later = jnp.where(row > col, 1.0, 0.0).astype(BF16)
    later2 = jnp.concatenate([later, later], axis=0)
    strict = jnp.concatenate([col < row] * 2, axis=0)
    qs = [jnp.concatenate([_keep_head(q_ref[t * tq:(t + 1) * tq, :], lane, hh) for hh in range(2)], axis=0)
          for t in range(tiles)]

    def block(q2, blk, run, acc, masked, present=None):
        start = pl.multiple_of(blk * tq, tq)
        z = _dot_nt(q2, k_ref[pl.ds(start, tq), :])
        log_beta = jnp.minimum(z, 0.0) - jnp.log(1.0 + jnp.exp(-jnp.abs(z)))
        log_stay = log_beta - z
        if masked:
            log_stay = jnp.where(strict, log_stay, 0.0)
        if present is not None:
            log_stay = log_stay * present
        after = _dot(jnp.concatenate(_split2(log_stay), axis=1), later2) + run
        w = jnp.exp(log_beta + after)
        if masked:
            w = jnp.where(strict, w, 0.0)
        if present is not None:
            w = w * present
        acc = acc + _dot(w.astype(BF16), v_ref[pl.ds(start, tq), :])
        return run + jnp.sum(log_stay, axis=-1, keepdims=True), acc

    run0 = jnp.zeros((2 * tq, 1), F32)
    acc0 = jnp.zeros((2 * tq, LANES), F32)
    has_left = jnp.where(first > 0, 1.0, 0.0)
    states = []
    for t in range(tiles):
        run, acc = block(qs[t], first + t, run0, acc0, True)
        if t == 0:
            states.append(block(qs[t], jnp.maximum(first - 1, 0), run, acc, False, has_left))
        else:
            states.append(block(qs[t], first + t - 1, run, acc, False))

    def may_matter(run):
        return jnp.max(run) >= -SB_LOG_UNDERFLOW

    alive = [may_matter(run) for run, _ in states]
    for t in range(tiles):

        def cond(c):
            left, alive_t, _, _ = c
            return (left > 0) & alive_t

        def body(c, t=t):
            left, _, run, acc = c
            run, acc = block(qs[t], left - 1, run, acc, False)
            return left - 1, may_matter(run), run, acc

        acc = lax.while_loop(cond, body, (jnp.maximum(first + t - 1, 0), alive[t]) + states[t])[3]
        o_ref[t * tq:(t + 1) * tq, :] = jnp.where(lane < HEAD_DIM, acc[:tq], acc[tq:]).astype(o_ref.dtype)


def _sb_attention(qkv, bsz, seq, tq, tiles):
    nq = seq // (tq * tiles)
    pairs = SB_WIDTH // LANES
    kern = functools.partial(_sb_attn_kernel, tq=tq, tiles=tiles)
    return pl.pallas_call(
        kern,
        grid=(bsz, pairs, nq),
        in_specs=[
            pl.BlockSpec((tq * tiles, LANES), lambda b, p, i: (b * nq + i, p)),
            pl.BlockSpec((seq, LANES), lambda b, p, i: (b, pairs + p)),
            pl.BlockSpec((seq, LANES), lambda b, p, i: (b, 2 * pairs + p)),
        ],
        out_specs=pl.BlockSpec((tq * tiles, LANES), lambda b, p, i: (b * nq + i, p)),
        out_shape=jax.ShapeDtypeStruct((bsz * seq, SB_WIDTH), BF16),
        compiler_params=_params("parallel", "parallel", "arbitrary"),
        name="stickbreak_attention",
    )(qkv, qkv, qkv)


def _ssd_kernel(z_ref, xs_ref, bc_ref, dt_ref, cwx_ref, cbx_ref, cwb_ref, cbb_ref, dtb_ref, alog_ref, dskip_ref,
                gn_ref, y_ref, xtail, btail, state):
    c = pl.program_id(1)
    cl = SSM_CHUNK
    halo = SUBLANES
    assert SSM_CONV == 4

    @pl.when(c == 0)
    def _():
        xtail[...] = jnp.zeros_like(xtail)
        btail[...] = jnp.zeros_like(btail)
        state[...] = jnp.zeros_like(state)

    def shift_rows(x, tail, k):
        rolled = pltpu.roll(x, k, 0)
        wrapped = pltpu.roll(tail, k, 0)
        row = lax.broadcasted_iota(jnp.int32, tail.shape, 0)
        return jnp.concatenate([jnp.where(row < k, wrapped, rolled[:halo]), rolled[halo:]], axis=0)

    def conv_silu(tails, raw_ref, w_ref, b_ref):
        x = raw_ref[...]
        x1 = shift_rows(x, tails[0], 1)
        u = w_ref[1:2, :] * x + w_ref[0:1, :] * x1
        y = b_ref[...] + w_ref[3:4, :] * x + w_ref[2:3, :] * x1 + shift_rows(u, tails[1], 2)
        tails[0] = x[cl - halo:, :]
        tails[1] = u[cl - halo:, :]
        return _silu(y)

    xs = conv_silu(xtail, xs_ref, cwx_ref, cbx_ref)
    bc = conv_silu(btail, bc_ref, cwb_ref, cbb_ref)

    dt = _softplus(dt_ref[...] + dtb_ref[...])
    da = dt * (-jnp.exp(alog_ref[...]))
    row = lax.broadcasted_iota(jnp.int32, (cl, cl), 0)
    col = lax.broadcasted_iota(jnp.int32, (cl, cl), 1)
    causal = col <= row
    tri = jnp.where(causal, 1.0, 0.0).astype(BF16)
    cs = _dot(jnp.concatenate([tri] * 3, axis=1), jnp.concatenate(_split3(da), axis=0))
    cs_t = cs.T
    ecs = jnp.exp(cs)
    dec_end = jnp.exp(cs[cl - 1:cl, :] - cs)

    head_of_lane = jnp.right_shift(lax.broadcasted_iota(jnp.int32, (LANES, SSM_INNER), 1), 6)
    expand = jnp.where(lax.broadcasted_iota(jnp.int32, (LANES, SSM_INNER), 0) == head_of_lane, 1.0, 0.0).astype(BF16)

    expand3 = jnp.concatenate([expand] * 3, axis=0)

    def per_head_lanes(v):
        return _dot(jnp.concatenate(_split3(v), axis=1), expand3)

    dt_e = per_head_lanes(dt)
    ecs_e = per_head_lanes(ecs)
    dec_end_e = per_head_lanes(dec_end)

    xdt = xs * dt_e
    xdt_b = xdt.astype(BF16)
    xw_b = (xdt * dec_end_e).astype(BF16)
    lane = lax.broadcasted_iota(jnp.int32, (cl, LANES), 1)
    first = lane < SSM_HEADDIM
    gs = SSM_STATE
    heads_per_group = SSM_HEADS // SSM_GROUPS
    for g in range(SSM_GROUPS):
        bg = bc[:, g * gs:(g + 1) * gs]
        cg_b = bc[:, (SSM_GROUPS + g) * gs:(SSM_GROUPS + g + 1) * gs].astype(BF16)
        cb = _dot_nt(cg_b, bg.astype(BF16))
        cols = slice(g * SSM_GROUP_WIDTH, (g + 1) * SSM_GROUP_WIDTH)
        prev = state[:, cols]
        y_off = _dot(cg_b, prev.astype(BF16)) * ecs_e[:, cols]
        state[:, cols] = prev * ecs_e[cl - 1:cl, cols] + _dot(bg.T.astype(BF16), xw_b[:, cols])
        for pr in range(heads_per_group // 2):
            pcols = slice(g * SSM_GROUP_WIDTH + pr * LANES, g * SSM_GROUP_WIDTH + (pr + 1) * LANES)
            x_pair = xdt_b[:, pcols]
            ys = []
            for hh in range(2):
                h = g * heads_per_group + 2 * pr + hh
                seg = jnp.where(causal, cs[:, h:h + 1] - cs_t[h:h + 1, :], NEG_BIG)
                ys.append(_dot((cb * jnp.exp(seg)).astype(BF16), x_pair))
            y_diag = jnp.where(first, ys[0], ys[1])
            y_pair = y_diag + y_off[:, pr * LANES:(pr + 1) * LANES] + xs[:, pcols] * dskip_ref[:, pcols]
            y_ref[:, pcols] = y_pair * _silu(z_ref[:, pcols])
    for g in range(SSM_GROUPS):
        cols = slice(g * SSM_GROUP_WIDTH, (g + 1) * SSM_GROUP_WIDTH)
        y_ref[:, cols] = _rms(y_ref[:, cols], gn_ref[:, cols])


def _ssd(proj, bsz, seq, cwx, cbx, cwb, cbb, dt_bias, a_log, d_skip, gnorm):
    cl = SSM_CHUNK
    nc = seq // cl

    def rows(width, colblk):
        return pl.BlockSpec((cl, width), lambda b, c: (b * nc + c, colblk))

    def const(r, width):
        return pl.BlockSpec((r, width), lambda b, c: (0, 0))

    return pl.pallas_call(
        _ssd_kernel,
        grid=(bsz, nc),
        in_specs=[
            rows(SSM_INNER, 0),
            rows(SSM_INNER, 1),
            rows(SSM_BC_WIDTH, 2 * SSM_INNER // SSM_BC_WIDTH),
            rows(LANES, (2 * SSM_INNER + SSM_BC_WIDTH) // LANES),
            const(SSM_CONV, SSM_INNER), const(1, SSM_INNER), const(SSM_CONV, SSM_BC_WIDTH), const(1, SSM_BC_WIDTH),
            const(1, LANES), const(1, LANES), const(1, SSM_INNER), const(1, SSM_INNER),
        ],
        out_specs=pl.BlockSpec((cl, SSM_INNER), lambda b, c: (b * nc + c, 0)),
        out_shape=jax.ShapeDtypeStruct((bsz * seq, SSM_INNER), F32),
        scratch_shapes=[
            pltpu.VMEM((2, SUBLANES, SSM_INNER), F32),
            pltpu.VMEM((2, SUBLANES, SSM_BC_WIDTH), F32),
            pltpu.VMEM((SSM_STATE, SSM_INNER), F32),
        ],
        compiler_params=_params("parallel", "arbitrary"),
        name="ssd_scan",
    )(proj, proj, proj, proj, cwx, cbx, cwb, cbb, dt_bias, a_log, d_skip, gnorm)


def _mix_ffn_kernel(*refs, tm, tiles_per_seq, final_norm):
    x_ref, a_ref, b_ref, wa_ref, wb_ref, g_ref, wg_ref, wu_ref, cw_ref, cb_ref, wd_ref = refs[:11]
    pos = 11
    fn_ref = None
    if final_norm:
        fn_ref = refs[pos]
        pos += 1
    o_ref, gbuf, act_ref = refs[pos:pos + 3]
    halo = SUBLANES

    o_ref[...] = x_ref[...] + _dot(a_ref[...].astype(BF16), wa_ref[...]) + _dot(b_ref[...].astype(BF16), wb_ref[...])
    hn = _rms(o_ref[...], g_ref[...]).astype(BF16)

    @pl.when(pl.program_id(0) % tiles_per_seq == 0)
    def _():
        gbuf[0:halo, :] = jnp.zeros((halo, gbuf.shape[1]), F32)

    step = MXU_WIDTH
    for c0 in range(0, FFN_DIM, step):
        cols = slice(c0, c0 + step)
        gate = _dot(hn, wg_ref[:, cols])
        up = _dot(hn, wu_ref[:, cols])
        gbuf[halo:halo + tm, cols] = gate
        conv = cb_ref[:, cols] + cw_ref[FFN_CONV - 1:FFN_CONV, cols] * gate
        for t in range(FFN_CONV - 1):
            back = FFN_CONV - 1 - t
            conv = conv + cw_ref[t:t + 1, cols] * gbuf[halo - back:halo - back + tm, cols]
        gbuf[0:halo, cols] = gbuf[tm:tm + halo, cols]
        act_ref[:, cols] = (_silu(conv) * up).astype(BF16)

    y = o_ref[...] + _dot(act_ref[...], wd_ref[...])
    if final_norm:
        y = _rms(y, fn_ref[...])
    o_ref[...] = y


def _mix_ffn(x, a, b, wa, wb, g, wg, wu, cw, cb, wd, seq, *, tm, final_gain=None):
    t, d = x.shape
    f = wg.shape[1]
    final_norm = final_gain is not None
    row = lambda w: pl.BlockSpec((tm, w), lambda i: (i, 0))
    in_specs = [row(d), row(a.shape[1]), row(b.shape[1]), _resident(wa.shape), _resident(wb.shape),
                _resident(g.shape), _resident(wg.shape), _resident(wu.shape), _resident(cw.shape),
                _resident(cb.shape), _resident(wd.shape)]
    args = [x, a, b, wa, wb, g, wg, wu, cw, cb, wd]
    if final_norm:
        in_specs.append(_resident(final_gain.shape))
        args.append(final_gain)
    kern = functools.partial(_mix_ffn_kernel, tm=tm, tiles_per_seq=seq // tm, final_norm=final_norm)
    return pl.pallas_call(
        kern,
        grid=(t // tm,),
        in_specs=in_specs,
        out_specs=row(d),
        out_shape=jax.ShapeDtypeStruct((t, d), F32),
        scratch_shapes=[pltpu.VMEM((tm + SUBLANES, f), F32), pltpu.VMEM((tm, f), BF16)],
        compiler_params=_params("arbitrary"),
        name="mix_ffn",
    )(*args)


def _row(v):
    return v.reshape(1, -1).astype(F32)


def _pad_cols(w, n):
    return jnp.pad(w, ((0, 0), (0, n - w.shape[1])))


def _rope_lane_constants():
    lane = jnp.arange(LANES)
    half_a = HEAD_DIM // 2
    inv_a = 1.0 / (ROPE_THETA ** (jnp.arange(half_a, dtype=F32) * (2.0 / HEAD_DIM)))
    freq_a = inv_a[lane % half_a]
    sign_a = jnp.where((lane % HEAD_DIM) < half_a, -1.0, 1.0).astype(F32)
    half_b = MLA_ROPE // 2
    assert HEAD_DIM == 2 * MLA_ROPE
    in_rope = (lane >= MLA_NOPE) & (lane < MLA_NOPE + MLA_ROPE)
    sign_b = jnp.where(in_rope, jnp.where((lane - MLA_NOPE) < half_b, -1.0, 1.0), 0.0).astype(F32)
    rest_b = jnp.where(in_rope, 0.0, 1.0).astype(F32)
    src = jnp.arange(LANES)[:, None]
    packed_tok = jnp.arange(ROPE_PACK)[:, None, None]
    sel_a = (src == packed_tok * half_a + lane % half_a).astype(BF16)
    sel_b = (in_rope & (src == packed_tok * half_a + 2 * ((lane - MLA_NOPE) % half_b))).astype(BF16)
    return _row(freq_a), sel_a, sel_b, _row(sign_a), _row(sign_b), _row(rest_b)


def _mla_slot_weights(w_uq, w_ukv):
    rq = w_uq.shape[0]
    wq = w_uq.reshape(rq, MLA_HEADS, MLA_NOPE + MLA_ROPE)
    wq = jnp.pad(wq, ((0, 0), (0, 0), (0, MLA_SLOT - MLA_NOPE - MLA_ROPE))).reshape(rq, MLA_WIDTH)
    rk = w_ukv.shape[0]
    wkv = w_ukv.reshape(rk, MLA_HEADS, MLA_NOPE + MLA_V)
    wk = jnp.pad(wkv[:, :, :MLA_NOPE], ((0, 0), (0, 0), (0, MLA_SLOT - MLA_NOPE))).reshape(rk, MLA_WIDTH)
    wv = jnp.pad(wkv[:, :, MLA_NOPE:], ((0, 0), (0, 0), (0, MLA_SLOT - MLA_V))).reshape(rk, MLA_WIDTH)
    return wq.astype(BF16), jnp.concatenate([wk, wv], axis=1).astype(BF16)


def kernel(x, positions,
           l0_norm_mix, l0_w_in, l0_mla_q_norm, l0_mla_w_uq, l0_mla_kv_norm, l0_mla_w_ukv, l0_w_out,
           l0_norm_ffn, l0_ffn_w_gate, l0_ffn_w_up, l0_ffn_conv_w, l0_ffn_conv_b, l0_ffn_w_down,
           l1_norm_mix, l1_w_in, l1_ssm_conv_w, l1_ssm_conv_b, l1_ssm_dt_bias, l1_ssm_a_log, l1_ssm_d,
           l1_ssm_norm, l1_w_out,
           l1_norm_ffn, l1_ffn_w_gate, l1_ffn_w_up, l1_ffn_conv_w, l1_ffn_conv_b, l1_ffn_w_down,
           final_norm):
    bsz, seq, d = x.shape
    t = bsz * seq
    xf = x.reshape(t, d)
    tm = ROW_TILE
    span = DSW_BLK * max(dil for _, dil in DSW_GROUPS)
    assert d == D_MODEL and seq % tm == 0 and seq % span == 0 and seq % MLA_TQ == 0
    assert seq % (SB_TQ * SB_TILES_PER_STEP) == 0 and seq % SSM_CHUNK == 0 and t % (ROPE_PACK * tm) == 0

    pos_packed = jnp.repeat(positions.reshape(t // ROPE_PACK, ROPE_PACK).astype(F32), LANES // ROPE_PACK, axis=1)
    cos_a, sin_a, cos_b, sin_b = _rope_tables(pos_packed, *_rope_lane_constants(), rows=tm)

    nd = DSW_WIDTH
    gw = DSW_GROUP_WIDTH
    w_qkv = jnp.concatenate([l0_w_in[:, part * nd + gi * gw:part * nd + (gi + 1) * gw]
                             for gi in range(len(DSW_GROUPS)) for part in range(3)], axis=1).astype(BF16)
    w_cq = l0_w_in[:, 3 * nd:3 * nd + MLA_Q_RANK]
    w_ckv = l0_w_in[:, 3 * nd + MLA_Q_RANK:3 * nd + MLA_Q_RANK + MLA_KV_RANK]
    w_kpe = l0_w_in[:, 3 * nd + MLA_Q_RANK + MLA_KV_RANK:]
    w_kpe_slot = jnp.pad(w_kpe, ((0, 0), (MLA_NOPE, MLA_SLOT - MLA_NOPE - MLA_ROPE)))
    w_lat = jnp.concatenate([w_cq, w_ckv, w_kpe_slot], axis=1).astype(BF16)
    wq_slot, wkv_slot = _mla_slot_weights(l0_mla_w_uq, l0_mla_w_ukv)
    g0, g1, g2, q_mla, kv_mla = _l0_in_proj(xf, _row(l0_norm_mix), w_qkv, w_lat, _row(l0_mla_q_norm), wq_slot,
                                            _row(l0_mla_kv_norm), wkv_slot, cos_a, sin_a, cos_b, sin_b,
                                            bsz, seq, tm)
    y_a = _dilated_attention(g0, g1, g2, bsz, seq)
    y_b = _mla_attention(q_mla, kv_mla, bsz, seq, tq=MLA_TQ, tk=MLA_TK, heads=MLA_HEADS_PER_STEP)

    w_out0 = l0_w_out.astype(BF16)
    x2 = _mix_ffn(xf, y_a, y_b, w_out0[:DSW_GROUP_WIDTH], w_out0[DSW_GROUP_WIDTH:], _row(l0_norm_ffn),
                  l0_ffn_w_gate.astype(BF16), l0_ffn_w_up.astype(BF16), l0_ffn_conv_w, _row(l0_ffn_conv_b),
                  l0_ffn_w_down.astype(BF16), seq, tm=tm)

    o_dt = 2 * SSM_INNER + SSM_BC_WIDTH
    w_ssm = _pad_cols(l1_w_in[:, :o_dt + SSM_HEADS], SSM_PROJ_WIDTH).astype(BF16)
    w_sb = l1_w_in[:, o_dt + SSM_HEADS:].astype(BF16)
    proj, qkv_sb = _l1_in_proj(x2, _row(l1_norm_mix), w_ssm, w_sb, tm)

    cw = l1_ssm_conv_w
    cb = _row(l1_ssm_conv_b)
    y_c = _ssd(proj, bsz, seq, cw[:, :SSM_INNER], cb[:, :SSM_INNER], cw[:, SSM_INNER:], cb[:, SSM_INNER:],
               _pad_cols(_row(l1_ssm_dt_bias), LANES), _pad_cols(_row(l1_ssm_a_log), LANES),
               _row(jnp.repeat(l1_ssm_d, SSM_HEADDIM)), _row(l1_ssm_norm))
    y_d = _sb_attention(qkv_sb, bsz, seq, tq=SB_TQ, tiles=SB_TILES_PER_STEP)

    w_out1 = l1_w_out.astype(BF16)
    out = _mix_ffn(x2, y_c, y_d, w_out1[:SSM_INNER], w_out1[SSM_INNER:], _row(l1_norm_ffn),
                   l1_ffn_w_gate.astype(BF16), l1_ffn_w_up.astype(BF16), l1_ffn_conv_w, _row(l1_ffn_conv_b),
                   l1_ffn_w_down.astype(BF16), seq, tm=tm, final_gain=_row(final_norm))
    return out.reshape(bsz, seq, d)
```

```python
import functools
import math

import jax
import jax.numpy as jnp
from jax import lax
from jax.experimental import pallas as pl
from jax.experimental.pallas import tpu as pltpu

F32 = jnp.float32
BF16 = jnp.bfloat16

LANES = 128
SUBLANES = 8
MXU_WIDTH = 256
VMEM_LIMIT_BYTES = 56 * 1024 * 1024

D_MODEL = 1024
HEAD_DIM = 64
ROPE_THETA = 10000.0
ROPE_PACK = 4
NORM_EPS = 1e-6

DSW_GROUPS = ((128, 1), (512, 4), (2048, 16))
DSW_HEADS_PER_GROUP = 4
DSW_HEADS = DSW_HEADS_PER_GROUP * len(DSW_GROUPS)
DSW_BLK = 128
DSW_WIDTH = DSW_HEADS * HEAD_DIM
DSW_GROUP_WIDTH = DSW_HEADS_PER_GROUP * HEAD_DIM
DSW_UNITS_PER_BODY = 4

MLA_HEADS = 12
MLA_Q_RANK = 256
MLA_KV_RANK = 128
MLA_NOPE = 64
MLA_ROPE = 32
MLA_V = 64
MLA_SLOT = 128
MLA_WIDTH = MLA_HEADS * MLA_SLOT
MLA_LAT_WIDTH =MLA_Q_RANK + MLA_KV_RANK + MLA_SLOT
MLA_Q_SCALE = (MLA_NOPE + MLA_ROPE) ** -0.5 * math.log2(math.e)

SSM_INNER = 1024
SSM_HEADDIM = 64
SSM_HEADS = 16
SSM_STATE = 128
SSM_GROUPS = 2
SSM_CONV = 4
SSM_CHUNK = 128
SSM_GROUP_WIDTH = SSM_INNER // SSM_GROUPS
SSM_BC_WIDTH = 2 * SSM_GROUPS * SSM_STATE
SSM_DT_PAD = LANES
SSM_PROJ_WIDTH = 2 * SSM_INNER + SSM_BC_WIDTH + SSM_DT_PAD

SB_HEADS = 8
SB_WIDTH = SB_HEADS * HEAD_DIM
SB_LOG_UNDERFLOW = 104.0

FFN_DIM = 2816
FFN_CONV = 3

NEG_BIG = -1e30

ROW_TILE = 512
MLA_TQ, MLA_TK, MLA_HEADS_PER_STEP = 1024, 512, 4
SB_TQ, SB_TILES_PER_STEP = 256, 4


def _params(*sem):
    return pltpu.CompilerParams(dimension_semantics=sem, vmem_limit_bytes=VMEM_LIMIT_BYTES)


def _rms(x, g):
    return x * lax.rsqrt(jnp.mean(x * x, axis=-1, keepdims=True) + NORM_EPS) * g


def _silu(x):
    return x * (1.0 / (1.0 + jnp.exp(-x)))


def _log1p(e):
    u = 1.0 + e
    return jnp.where(u == 1.0, e, jnp.log(u) * (e / (u - 1.0)))


def _softplus(x):
    return jnp.maximum(x, 0.0) + _log1p(jnp.exp(-jnp.abs(x)))


def _swap_halves(x, half):
    lane = lax.broadcasted_iota(jnp.int32, x.shape, 1)
    up = pltpu.roll(x, LANES - half, 1)
    down = pltpu.roll(x, half, 1)
    return jnp.where((lane & half) == 0, up, down)


def _rope_tile(x, cos, sin, half):
    return x * cos + _swap_halves(x, half) * sin


def _split2(x):
    hi = x.astype(BF16)
    return hi, (x - hi.astype(F32)).astype(BF16)


def _split3(x):
    hi = x.astype(BF16)
    r1 = x - hi.astype(F32)
    mid = r1.astype(BF16)
    lo = (r1 - mid.astype(F32)).astype(BF16)
    return hi, mid, lo


def _dot(a, b):
    return jnp.dot(a, b, preferred_element_type=F32)


def _dot_nt(a, b):
    return lax.dot_general(a, b, (((1,), (1,)), ((), ())), preferred_element_type=F32)


def _keep_head(x2, lane, second):
    mine = (lane >= HEAD_DIM) if second else (lane < HEAD_DIM)
    return jnp.where(mine, x2.astype(F32), 0.0).astype(BF16)


def _resident(shape):
    return pl.BlockSpec(shape, lambda *_: (0,) * len(shape), pipeline_mode=pl.Buffered(1))


def _rope_table_kernel(pos_ref, freq_ref, sela_ref, selb_ref, signa_ref, signb_ref, restb_ref,
                       cosa_ref, sina_ref, cosb_ref, sinb_ref):
    ang = pos_ref[...] * freq_ref[...]
    cos3 = jnp.concatenate(_split3(jnp.cos(ang)), axis=1)
    sin3 = jnp.concatenate(_split3(jnp.sin(ang)), axis=1)
    rows = ang.shape[0]

    def spread(parts, sel):
        return _dot(parts, jnp.concatenate([sel] * 3, axis=0))

    for g in range(ROPE_PACK):
        tokens = pl.ds(g, rows, stride=ROPE_PACK)
        cosa_ref[tokens, :] = spread(cos3, sela_ref[g])
        sina_ref[tokens, :] = spread(sin3, sela_ref[g]) * signa_ref[...]
        cosb_ref[tokens, :] = spread(cos3, selb_ref[g]) + restb_ref[...]
        sinb_ref[tokens, :] = spread(sin3, selb_ref[g]) * signb_ref[...]


def _rope_tables(pos_packed, freq, sel_a, sel_b, sign_a, sign_b, rest_b, rows):
    packed = pos_packed.shape[0]
    t = packed * ROPE_PACK
    const = pl.BlockSpec((1, LANES), lambda i: (0, 0))
    sel = pl.BlockSpec((ROPE_PACK, LANES, LANES), lambda i: (0, 0, 0))
    out = pl.BlockSpec((rows * ROPE_PACK, LANES), lambda i: (i, 0))
    return pl.pallas_call(
        _rope_table_kernel,
        grid=(packed // rows,),
        in_specs=[pl.BlockSpec((rows, LANES), lambda i: (i, 0)), const, sel, sel, const, const, const],
        out_specs=[out] * 4,
        out_shape=[jax.ShapeDtypeStruct((t, LANES), F32)] * 4,
        compiler_params=_params("parallel"),
        name="rope_tables",
    )(pos_packed, freq, sel_a, sel_b, sign_a, sign_b, rest_b)


def _l0_in_kernel(x_ref, g_ref, wqkv_ref, wlat_ref, gq_ref, wq_ref, gkv_ref, wkv_ref,
                  cosa_ref, sina_ref, cosb_ref, sinb_ref, g0_ref, g1_ref, g2_ref, qm_ref, kvm_ref, slabs):
    tm = x_ref.shape[0]
    hn = _rms(x_ref[...], g_ref[...]).astype(BF16)
    cos_a, sin_a = cosa_ref[...], sina_ref[...]
    cos_b, sin_b = cosb_ref[...], sinb_ref[...]
    half_a, half_b = HEAD_DIM // 2, MLA_ROPE // 2
    step = MXU_WIDTH
    halves = step // LANES
    group_refs = (g0_ref, g1_ref, g2_ref)
    slab = 0
    for gi, (_, dil) in enumerate(DSW_GROUPS):
        for part in range(3):
            c0 = (3 * gi + part) * DSW_GROUP_WIDTH
            acc = _dot(hn, wqkv_ref[:, c0:c0 + step])
            for h in range(halves):
                tile = acc[:, h * LANES:(h + 1) * LANES]
                if part < 2:
                    tile = _rope_tile(tile, cos_a, sin_a, half_a)
                if part == 0:
                    tile = tile * HEAD_DIM ** -0.5
                col = part * DSW_GROUP_WIDTH + h * LANES
                if dil == 1:
                    g0_ref[:, col:col + LANES] = tile.astype(BF16)
                else:
                    slabs[slab] = tile
                    for r in range(dil):
                        rows = slabs[slab, pl.ds(r, tm // dil, stride=dil), :]
                        group_refs[gi][r, :, col:col + LANES] = rows.astype(BF16)
                    slab += 1

    lat = _dot(hn, wlat_ref[...])
    cq = _rms(lat[:, :MLA_Q_RANK], gq_ref[...]).astype(BF16)
    ckv = _rms(lat[:, MLA_Q_RANK:MLA_Q_RANK + MLA_KV_RANK], gkv_ref[...]).astype(BF16)
    kpe = _rope_tile(lat[:, MLA_Q_RANK + MLA_KV_RANK:], cos_b, sin_b, half_b)
    for c0 in range(0, MLA_WIDTH, step):
        acc = _dot(cq, wq_ref[:, c0:c0 + step])
        for h in range(halves):
            sl = slice(h * LANES, (h + 1) * LANES)
            tile = _rope_tile(acc[:, sl], cos_b, sin_b, half_b) * MLA_Q_SCALE
            qm_ref[:, c0 + h * LANES:c0 + (h + 1) * LANES] = tile.astype(BF16)
    lane = lax.broadcasted_iota(jnp.int32, kpe.shape, 1)
    one_hot = jnp.where(lane == MLA_V, 1.0, 0.0)
    for c0 in range(0, 2 * MLA_WIDTH, step):
        acc = _dot(ckv, wkv_ref[:, c0:c0 + step])
        extra = kpe if c0 < MLA_WIDTH else one_hot
        for h in range(halves):
            sl = slice(h * LANES, (h + 1) * LANES)
            kvm_ref[:, c0 + h * LANES:c0 + (h + 1) * LANES] = (acc[:, sl] + extra).astype(BF16)


def _l0_in_proj(x, g, w_qkv, w_lat, gq, wq, gkv, wkv, cos_a, sin_a, cos_b, sin_b, bsz, seq, tm):
    t, d = x.shape
    tps = seq // tm
    gw = 3 * DSW_GROUP_WIDTH
    row = lambda w: pl.BlockSpec((tm, w), lambda i: (i, 0))

    def residue_major(dil):
        return pl.BlockSpec((None, dil, tm // dil, gw), lambda i: (i // tps, 0, i % tps, 0))

    dils = [dil for _, dil in DSW_GROUPS]
    n_slabs = sum(gw // LANES for dil in dils if dil > 1)
    return pl.pallas_call(
        _l0_in_kernel,
        grid=(t // tm,),
        in_specs=[row(d), _resident((1, d)), _resident(w_qkv.shape), _resident(w_lat.shape),
                  _resident(gq.shape), _resident(wq.shape), _resident(gkv.shape), _resident(wkv.shape),
                  row(LANES), row(LANES), row(LANES), row(LANES)],
        out_specs=[row(gw), residue_major(dils[1]), residue_major(dils[2]), row(MLA_WIDTH), row(2 * MLA_WIDTH)],
        out_shape=[jax.ShapeDtypeStruct((t, gw), BF16),
                   jax.ShapeDtypeStruct((bsz, dils[1], seq // dils[1], gw), BF16),
                   jax.ShapeDtypeStruct((bsz, dils[2], seq // dils[2], gw), BF16),
                   jax.ShapeDtypeStruct((t, MLA_WIDTH), BF16),
                   jax.ShapeDtypeStruct((t, 2 * MLA_WIDTH), BF16)],
        scratch_shapes=[pltpu.VMEM((n_slabs, tm, LANES), F32)],
        compiler_params=_params("parallel"),
        name="l0_in_proj",
    )(x, g, w_qkv, w_lat, gq, wq, gkv, wkv, cos_a, sin_a, cos_b, sin_b)


def _l1_in_kernel(x_ref, g_ref, wssm_ref, wsb_ref, proj_ref, qkv_ref):
    hn = _rms(x_ref[...], g_ref[...]).astype(BF16)
    step = MXU_WIDTH
    for c0 in range(0, SSM_PROJ_WIDTH, step):
        c1 = min(c0 + step, SSM_PROJ_WIDTH)
        proj_ref[:, c0:c1] = _dot(hn, wssm_ref[:, c0:c1])
    for c0 in range(0, 3 * SB_WIDTH, step):
        acc = _dot(hn, wsb_ref[:, c0:c0 + step])
        if c0 < SB_WIDTH:
            acc = acc * HEAD_DIM ** -0.5
        qkv_ref[:, c0:c0 + step] = acc.astype(BF16)


def _l1_in_proj(x, g, w_ssm, w_sb, tm):
    t, d = x.shape
    row = lambda w: pl.BlockSpec((tm, w), lambda i: (i, 0))
    return pl.pallas_call(
        _l1_in_kernel,
        grid=(t // tm,),
        in_specs=[row(d), _resident((1, d)), _resident(w_ssm.shape), _resident(w_sb.shape)],
        out_specs=[row(SSM_PROJ_WIDTH), row(3 * SB_WIDTH)],
        out_shape=[jax.ShapeDtypeStruct((t, SSM_PROJ_WIDTH), F32),
                   jax.ShapeDtypeStruct((t, 3 * SB_WIDTH), BF16)],
        compiler_params=_params("parallel"),
        name="l1_in_proj",
    )(x, g, w_ssm, w_sb)


def _dilated_kernel(c0_ref, p0_ref, c1_ref, p1_ref, c2_ref, p2_ref, y_ref, o_acc, m_acc, l_acc):
    blk = DSW_BLK
    gw = DSW_GROUP_WIDTH
    pairs = gw // LANES
    qcols, kcols, vcols = slice(0, gw), slice(gw, 2 * gw), slice(2 * gw, 3 * gw)
    first_key = jnp.where(pl.program_id(1) > 0, 0, blk)
    qi = lax.broadcasted_iota(jnp.int32, (blk, 2 * blk), 0)
    kj = lax.broadcasted_iota(jnp.int32, (blk, 2 * blk), 1)
    window = (kj >= qi) & (kj <= qi + blk)
    lane = lax.broadcasted_iota(jnp.int32, (blk, LANES), 1)
    first = lane < HEAD_DIM

    def unit(q, k2, v2, lowest_key):
        valid = window & (kj >= lowest_key)
        res = []
        for pair in range(pairs):
            sl = slice(pair * LANES, (pair + 1) * LANES)
            kk, vv = k2[:, sl], v2[:, sl]
            outs, maxes, sums = [], [], []
            for hh in range(2):
                s = jnp.where(valid, _dot_nt(_keep_head(q[:, sl], lane, hh), kk), NEG_BIG)
                m = jnp.max(s, axis=-1, keepdims=True)
                p = jnp.exp(s - m)
                outs.append(_dot(p.astype(BF16), vv))
                maxes.append(jnp.broadcast_to(m, (blk, LANES)))
                sums.append(jnp.broadcast_to(jnp.sum(p, axis=-1, keepdims=True), (blk, LANES)))
            res.append(tuple(jnp.where(first, a, b) for a, b in (outs, maxes, sums)))
        return res

    def put(rows, res):
        for pair, (o, m, l) in enumerate(res):
            o_acc[pair, rows, :] = o
            m_acc[pair, rows, :] = m
            l_acc[pair, rows, :] = l

    def merge(rows, res):
        for pair, (o, m, l) in enumerate(res):
            m_old = m_acc[pair, rows, :]
            m_new = jnp.maximum(m_old, m)
            keep, add = jnp.exp(m_old - m_new), jnp.exp(m - m_new)
            o_acc[pair, rows, :] = o_acc[pair, rows, :] * keep + o * add
            l_acc[pair, rows, :] = l_acc[pair, rows, :] * keep + l * add
            m_acc[pair, rows, :] = m_new

    def stacked(prev, cur):
        return jnp.concatenate([prev, cur], axis=0)

    head = slice(0, blk)

    def head_unit(cur, prev):
        return unit(cur(head, qcols), stacked(prev(kcols), cur(head, kcols)),
                    stacked(prev(vcols), cur(head, vcols)), first_key)

    def later_unit(cur, r0):
        keys = pl.ds(r0 - blk, 2 * blk)
        return unit(cur(pl.ds(r0, blk), qcols), cur(keys, kcols), cur(keys, vcols), 0)

    units_per_body = DSW_UNITS_PER_BODY

    cur0 = lambda rows, cols: c0_ref[rows, cols]
    put(head, head_unit(cur0, lambda cols: p0_ref[:, cols]))
    later_blocks = c0_ref.shape[0] // blk - 1
    group0 = 5
    assert later_blocks % group0 == 0

    def dense_blocks(u, carry):
        for k in range(group0):
            r0 = pl.multiple_of((1 + u * group0 + k) * blk, blk)
            put(pl.ds(r0, blk), later_unit(cur0, r0))
        return carry

    lax.fori_loop(0, later_blocks // group0, dense_blocks, 0)

    for cur_ref, prev_ref in ((c1_ref, p1_ref), (c2_ref, p2_ref)):
        dil, blocks = cur_ref.shape[0], cur_ref.shape[1] // blk
        residues_per_body = max(1, units_per_body // blocks)
        assert dil % residues_per_body == 0

        def residues(u, carry, cur_ref=cur_ref, prev_ref=prev_ref, dil=dil, blocks=blocks,
                     residues_per_body=residues_per_body):
            for k in range(residues_per_body):
                r = u * residues_per_body + k
                cur = lambda rows, cols, r=r: cur_ref[r, rows, cols]
                merge(pl.ds(r, blk, stride=dil), head_unit(cur, lambda cols, r=r: prev_ref[r, :, cols]))
                for n in range(1, blocks):
                    merge(pl.ds(n * blk * dil + r, blk, stride=dil), later_unit(cur, n * blk))
            return carry

        lax.fori_loop(0, dil // residues_per_body, residues, 0)

    for pair in range(pairs):
        y_ref[:, pair * LANES:(pair + 1) * LANES] = (o_acc[pair] * (1.0 / l_acc[pair])).astype(y_ref.dtype)


def _dilated_attention(g0, g1, g2, bsz, seq):
    blk = DSW_BLK
    span = blk * max(dil for _, dil in DSW_GROUPS)
    nspan = seq // span
    gw = 3 * DSW_GROUP_WIDTH
    d1, d2 = g1.shape[1], g2.shape[1]
    blocks_per_span0 = span // blk

    def residue_major(dil, rows, prev):
        per_span = span // dil // rows
        if prev:
            return pl.BlockSpec((None, dil, rows, gw), lambda b, s: (b, 0, jnp.maximum(s * per_span - 1, 0), 0))
        return pl.BlockSpec((None, dil, rows, gw), lambda b, s: (b, 0, s, 0))

    return pl.pallas_call(
        _dilated_kernel,
        grid=(bsz, nspan),
        in_specs=[
            pl.BlockSpec((span, gw), lambda b, s: (b * nspan + s, 0)),
            pl.BlockSpec((blk, gw), lambda b, s: (jnp.maximum((b * nspan + s) * blocks_per_span0 - 1, 0), 0)),
            residue_major(d1, span // d1, False), residue_major(d1, blk, True),
            residue_major(d2, span // d2, False), residue_major(d2, blk, True),
        ],
        out_specs=pl.BlockSpec((span, DSW_GROUP_WIDTH), lambda b, s: (b * nspan + s, 0)),
        out_shape=jax.ShapeDtypeStruct((bsz * seq, DSW_GROUP_WIDTH), BF16),
        scratch_shapes=[pltpu.VMEM((DSW_GROUP_WIDTH // LANES, span, LANES), F32)] * 3,
        compiler_params=_params("parallel", "arbitrary"),
        name="dilated_attention",
    )(g0, g0, g1, g1, g2, g2)


def _mla_attn_kernel(q_ref, k_ref, v_ref, o_ref, *, tq, tk, heads, nq):
    assert tq % tk == 0 and heads % 2 == 0
    lane = lax.broadcasted_iota(jnp.int32, (tq, LANES), 1)
    slots = [slice(hh * MLA_SLOT, (hh + 1) * MLA_SLOT) for hh in range(heads)]
    diag_blocks = tq // tk

    def update(q, m, acc, start, slot, visible=None):
        s = _dot_nt(q, k_ref[start:start + tk, slot])
        if visible is not None:
            s = jnp.where(visible, s, NEG_BIG)
        m_new = jnp.maximum(m, jnp.max(s, axis=-1, keepdims=True))
        p = jnp.exp2(s - m_new)
        return m_new, jnp.exp2(m - m_new) * acc + _dot(p.astype(BF16), v_ref[start:start + tk, slot])

    def tile(c):
        carry = [(jnp.full((tq, 1), NEG_BIG, F32), jnp.zeros((tq, LANES), F32)) for _ in range(heads)]
        for j in range(c * diag_blocks):
            carry = [update(q_ref[:, slots[hh]], *carry[hh], j * tk, slots[hh]) for hh in range(heads)]
        for b in range(diag_blocks):
            r0 = b * tk
            row = lax.broadcasted_iota(jnp.int32, (tq - r0, tk), 0)
            col = lax.broadcasted_iota(jnp.int32, (tq - r0, tk), 1)
            new = []
            for hh in range(heads):
                m, acc = carry[hh]
                m_low, acc_low = update(q_ref[r0:, slots[hh]], m[r0:], acc[r0:], c * tq + r0, slots[hh], col <= row)
                if r0:
                    m_low = jnp.concatenate([m[:r0], m_low], axis=0)
                    acc_low = jnp.concatenate([acc[:r0], acc_low], axis=0)
                new.append((m_low, acc_low))
            carry = new
        outs = [acc * (1.0 / acc[:, MLA_V:MLA_V + 1]) for _, acc in carry]
        for pr in range(heads // 2):
            packed = jnp.where(lane < MLA_V, outs[2 * pr], pltpu.roll(outs[2 * pr + 1], MLA_V, 1))
            o_ref[:, pr * LANES:(pr + 1) * LANES] = packed.astype(o_ref.dtype)

    for c in range(nq):
        pl.when(pl.program_id(2) == c)(functools.partial(tile, c))


def _mla_attention(q, kv, bsz, seq, tq, tk, heads):
    nq = seq // tq
    groups = MLA_HEADS // heads
    kern = functools.partial(_mla_attn_kernel, tq=tq, tk=tk, heads=heads, nq=nq)
    return pl.pallas_call(
        kern,
        grid=(bsz, groups, nq),
        in_specs=[
            pl.BlockSpec((tq, heads * MLA_SLOT), lambda b, p, i: (b * nq + i, p)),
            pl.BlockSpec((seq, heads * MLA_SLOT), lambda b, p, i: (b, p)),
            pl.BlockSpec((seq, heads * MLA_SLOT), lambda b, p, i: (b, groups + p)),
        ],
        out_specs=pl.BlockSpec((tq, heads * MLA_V), lambda b, p, i: (b * nq + i, p)),
        out_shape=jax.ShapeDtypeStruct((bsz * seq, MLA_HEADS * MLA_V), BF16),
        compiler_params=_params("parallel", "parallel", "arbitrary"),
        name="mla_attention",
    )(q, kv, kv)


def _sb_attn_kernel(q_ref, k_ref, v_ref, o_ref, *, tq, tiles):
    first = pl.program_id(2) * tiles
    lane = lax.broadcasted_iota(jnp.int32, (tq, LANES), 1)
    row = lax.broadcasted_iota(jnp.int32, (tq, tq), 0)
    col = lax.broadcasted_iota(jnp.int32, (tq, tq), 1)
    later = jnp.where(row > col, 1.0, 0.0).astype(BF16)
    later2 = jnp.concatenate([later, later], axis=0)
    strict = jnp.concatenate([col < row] * 2, axis=0)
    qs = [jnp.concatenate([_keep_head(q_ref[t * tq:(t + 1) * tq, :], lane, hh) for hh in range(2)], axis=0)
          for t in range(tiles)]

    def block(q2, blk, run, acc, masked, present=None):
        start = pl.multiple_of(blk * tq, tq)
        z = _dot_nt(q2, k_ref[pl.ds(start, tq), :])
        log_beta = jnp.minimum(z, 0.0) - jnp.log(1.0 + jnp.exp(-jnp.abs(z)))
        log_stay = log_beta - z
        if masked:
            log_stay = jnp.where(strict, log_stay, 0.0)
        if present is not None:
            log_stay = log_stay * present
        after = _dot(jnp.concatenate(_split2(log_stay), axis=1), later2) + run
        w = jnp.exp(log_beta + after)
        if masked:
            w = jnp.where(strict, w, 0.0)
        if present is not None:
            w = w * present
        acc = acc + _dot(w.astype(BF16), v_ref[pl.ds(start, tq), :])
        return run + jnp.sum(log_stay, axis=-1, keepdims=True), acc

    run0 = jnp.zeros((2 * tq, 1), F32)
    acc0 = jnp.zeros((2 * tq, LANES), F32)
    has_left = jnp.where(first > 0, 1.0, 0.0)
    states = []
    for t in range(tiles):
        run, acc = block(qs[t], first + t, run0, acc0, True)
        if t == 0:
            states.append(block(qs[t], jnp.maximum(first - 1, 0), run, acc, False, has_left))
        else:
            states.append(block(qs[t], first + t - 1, run, acc, False))

    def may_matter(run):
        return jnp.max(run) >= -SB_LOG_UNDERFLOW

    alive = [may_matter(run) for run, _ in states]
    for t in range(tiles):

        def cond(c):
            left, alive_t, _, _ = c
            return (left > 0) & alive_t

        def body(c, t=t):
            left, _, run, acc = c
            run, acc = block(qs[t], left - 1, run, acc, False)
            return left - 1, may_matter(run), run, acc

        acc = lax.while_loop(cond, body, (jnp.maximum(first + t - 1, 0), alive[t]) + states[t])[3]
        o_ref[t * tq:(t + 1) * tq, :] = jnp.where(lane < HEAD_DIM, acc[:tq], acc[tq:]).astype(o_ref.dtype)


def _sb_attention(qkv, bsz, seq, tq, tiles):
    nq = seq // (tq * tiles)
    pairs = SB_WIDTH // LANES
    kern = functools.partial(_sb_attn_kernel, tq=tq, tiles=tiles)
    return pl.pallas_call(
        kern,
        grid=(bsz, pairs, nq),
        in_specs=[
            pl.BlockSpec((tq * tiles, LANES), lambda b, p, i: (b * nq + i, p)),
            pl.BlockSpec((seq, LANES), lambda b, p, i: (b, pairs + p)),
            pl.BlockSpec((seq, LANES), lambda b, p, i: (b, 2 * pairs + p)),
        ],
        out_specs=pl.BlockSpec((tq * tiles, LANES), lambda b, p, i: (b * nq + i, p)),
        out_shape=jax.ShapeDtypeStruct((bsz * seq, SB_WIDTH), BF16),
        compiler_params=_params("parallel", "parallel", "arbitrary"),
        name="stickbreak_attention",
    )(qkv, qkv, qkv)


def _ssd_kernel(z_ref, xs_ref, bc_ref, dt_ref, cwx_ref, cbx_ref, cwb_ref, cbb_ref, dtb_ref, alog_ref, dskip_ref,
                gn_ref, y_ref, xtail, btail, state):
    c = pl.program_id(1)
    cl = SSM_CHUNK
    halo = SUBLANES
    assert SSM_CONV == 4

    @pl.when(c == 0)
    def _():
        xtail[...] = jnp.zeros_like(xtail)
        btail[...] = jnp.zeros_like(btail)
        state[...] = jnp.zeros_like(state)

    def shift_rows(x, tail, k):
        rolled = pltpu.roll(x, k, 0)
        wrapped = pltpu.roll(tail, k, 0)
        row = lax.broadcasted_iota(jnp.int32, tail.shape, 0)
        return jnp.concatenate([jnp.where(row < k, wrapped, rolled[:halo]), rolled[halo:]], axis=0)

    def conv_silu(tails, raw_ref, w_ref, b_ref):
        x = raw_ref[...]
        x1 = shift_rows(x, tails[0], 1)
        u = w_ref[1:2, :] * x + w_ref[0:1, :] * x1
        y = b_ref[...] + w_ref[3:4, :] * x + w_ref[2:3, :] * x1 + shift_rows(u, tails[1], 2)
        tails[0] = x[cl - halo:, :]
        tails[1] = u[cl - halo:, :]
        return _silu(y)

    xs = conv_silu(xtail, xs_ref, cwx_ref, cbx_ref)
    bc = conv_silu(btail, bc_ref, cwb_ref, cbb_ref)

    dt = _softplus(dt_ref[...] + dtb_ref[...])
    da = dt * (-jnp.exp(alog_ref[...]))
    row = lax.broadcasted_iota(jnp.int32, (cl, cl), 0)
    col = lax.broadcasted_iota(jnp.int32, (cl, cl), 1)
    causal = col <= row
    tri = jnp.where(causal, 1.0, 0.0).astype(BF16)
    cs = _dot(jnp.concatenate([tri] * 3, axis=1), jnp.concatenate(_split3(da), axis=0))
    cs_t = cs.T
    ecs = jnp.exp(cs)
    dec_end = jnp.exp(cs[cl - 1:cl, :] - cs)

    head_of_lane = jnp.right_shift(lax.broadcasted_iota(jnp.int32, (LANES, SSM_INNER), 1), 6)
    expand = jnp.where(lax.broadcasted_iota(jnp.int32, (LANES, SSM_INNER), 0) == head_of_lane, 1.0, 0.0).astype(BF16)

    expand3 = jnp.concatenate([expand] * 3, axis=0)

    def per_head_lanes(v):
        return _dot(jnp.concatenate(_split3(v), axis=1), expand3)

    dt_e = per_head_lanes(dt)
    ecs_e = per_head_lanes(ecs)
    dec_end_e = per_head_lanes(dec_end)

    xdt = xs * dt_e
    xdt_b = xdt.astype(BF16)
    xw_b = (xdt * dec_end_e).astype(BF16)
    lane = lax.broadcasted_iota(jnp.int32, (cl, LANES), 1)
    first = lane < SSM_HEADDIM
    gs = SSM_STATE
    heads_per_group = SSM_HEADS // SSM_GROUPS
    for g in range(SSM_GROUPS):
        bg = bc[:, g * gs:(g + 1) * gs]
        cg_b = bc[:, (SSM_GROUPS + g) * gs:(SSM_GROUPS + g + 1) * gs].astype(BF16)
        cb = _dot_nt(cg_b, bg.astype(BF16))
        cols = slice(g * SSM_GROUP_WIDTH, (g + 1) * SSM_GROUP_WIDTH)
        prev = state[:, cols]
        y_off = _dot(cg_b, prev.astype(BF16)) * ecs_e[:, cols]
        state[:, cols] = prev * ecs_e[cl - 1:cl, cols] + _dot(bg.T.astype(BF16), xw_b[:, cols])
        for pr in range(heads_per_group // 2):
            pcols = slice(g * SSM_GROUP_WIDTH + pr * LANES, g * SSM_GROUP_WIDTH + (pr + 1) * LANES)
            x_pair = xdt_b[:, pcols]
            ys = []
            for hh in range(2):
                h = g * heads_per_group + 2 * pr + hh
                seg = jnp.where(causal, cs[:, h:h + 1] - cs_t[h:h + 1, :], NEG_BIG)
                ys.append(_dot((cb * jnp.exp(seg)).astype(BF16), x_pair))
            y_diag = jnp.where(first, ys[0], ys[1])
            y_pair = y_diag + y_off[:, pr * LANES:(pr + 1) * LANES] + xs[:, pcols] * dskip_ref[:, pcols]
            y_ref[:, pcols] = y_pair * _silu(z_ref[:, pcols])
    for g in range(SSM_GROUPS):
        cols = slice(g * SSM_GROUP_WIDTH, (g + 1) * SSM_GROUP_WIDTH)
        y_ref[:, cols] = _rms(y_ref[:, cols], gn_ref[:, cols])


def _ssd(proj, bsz, seq, cwx, cbx, cwb, cbb, dt_bias, a_log, d_skip, gnorm):
    cl = SSM_CHUNK
    nc = seq // cl

    def rows(width, colblk):
        return pl.BlockSpec((cl, width), lambda b, c: (b * nc + c, colblk))

    def const(r, width):
        return pl.BlockSpec((r, width), lambda b, c: (0, 0))

    return pl.pallas_call(
        _ssd_kernel,
        grid=(bsz, nc),
        in_specs=[
            rows(SSM_INNER, 0),
            rows(SSM_INNER, 1),
            rows(SSM_BC_WIDTH, 2 * SSM_INNER // SSM_BC_WIDTH),
            rows(LANES, (2 * SSM_INNER + SSM_BC_WIDTH) // LANES),
            const(SSM_CONV, SSM_INNER), const(1, SSM_INNER), const(SSM_CONV, SSM_BC_WIDTH), const(1, SSM_BC_WIDTH),
            const(1, LANES), const(1, LANES), const(1, SSM_INNER), const(1, SSM_INNER),
        ],
        out_specs=pl.BlockSpec((cl, SSM_INNER), lambda b, c: (b * nc + c, 0)),
        out_shape=jax.ShapeDtypeStruct((bsz * seq, SSM_INNER), F32),
        scratch_shapes=[
            pltpu.VMEM((2, SUBLANES, SSM_INNER), F32),
            pltpu.VMEM((2, SUBLANES, SSM_BC_WIDTH), F32),
            pltpu.VMEM((SSM_STATE, SSM_INNER), F32),
        ],
        compiler_params=_params("parallel", "arbitrary"),
        name="ssd_scan",
    )(proj, proj, proj, proj, cwx, cbx, cwb, cbb, dt_bias, a_log, d_skip, gnorm)


def _mix_ffn_kernel(*refs, tm, tiles_per_seq, final_norm):
    x_ref, a_ref, b_ref, wa_ref, wb_ref, g_ref, wg_ref, wu_ref, cw_ref, cb_ref, wd_ref = refs[:11]
    pos = 11
    fn_ref = None
    if final_norm:
        fn_ref = refs[pos]
        pos += 1
    o_ref, gbuf, act_ref = refs[pos:pos + 3]
    halo = SUBLANES

    o_ref[...] = x_ref[...] + _dot(a_ref[...].astype(BF16), wa_ref[...]) + _dot(b_ref[...].astype(BF16), wb_ref[...])
    hn = _rms(o_ref[...], g_ref[...]).astype(BF16)

    @pl.when(pl.program_id(0) % tiles_per_seq == 0)
    def _():
        gbuf[0:halo, :] = jnp.zeros((halo, gbuf.shape[1]), F32)

    step = MXU_WIDTH
    for c0 in range(0, FFN_DIM, step):
        cols = slice(c0, c0 + step)
        gate = _dot(hn, wg_ref[:, cols])
        up = _dot(hn, wu_ref[:, cols])
        gbuf[halo:halo + tm, cols] = gate
        conv = cb_ref[:, cols] + cw_ref[FFN_CONV - 1:FFN_CONV, cols] * gate
        for t in range(FFN_CONV - 1):
            back = FFN_CONV - 1 - t
            conv = conv + cw_ref[t:t + 1, cols] * gbuf[halo - back:halo - back + tm, cols]
        gbuf[0:halo, cols] = gbuf[tm:tm + halo, cols]
        act_ref[:, cols] = (_silu(conv) * up).astype(BF16)

    y = o_ref[...] + _dot(act_ref[...], wd_ref[...])
    if final_norm:
        y = _rms(y, fn_ref[...])
    o_ref[...] = y


def _mix_ffn(x, a, b, wa, wb, g, wg, wu, cw, cb, wd, seq, *, tm, final_gain=None):
    t, d = x.shape
    f = wg.shape[1]
    final_norm = final_gain is not None
    row = lambda w: pl.BlockSpec((tm, w), lambda i: (i, 0))
    in_specs = [row(d), row(a.shape[1]), row(b.shape[1]), _resident(wa.shape), _resident(wb.shape),
                _resident(g.shape), _resident(wg.shape), _resident(wu.shape), _resident(cw.shape),
                _resident(cb.shape), _resident(wd.shape)]
    args = [x, a, b, wa, wb, g, wg, wu, cw, cb, wd]
    if final_norm:
        in_specs.append(_resident(final_gain.shape))
        args.append(final_gain)
    kern = functools.partial(_mix_ffn_kernel, tm=tm, tiles_per_seq=seq // tm, final_norm=final_norm)
    return pl.pallas_call(
        kern,
        grid=(t // tm,),
        in_specs=in_specs,
        out_specs=row(d),
        out_shape=jax.ShapeDtypeStruct((t, d), F32),
        scratch_shapes=[pltpu.VMEM((tm + SUBLANES, f), F32), pltpu.VMEM((tm, f), BF16)],
        compiler_params=_params("arbitrary"),
        name="mix_ffn",
    )(*args)


def _row(v):
    return v.reshape(1, -1).astype(F32)


def _pad_cols(w, n):
    return jnp.pad(w, ((0, 0), (0, n - w.shape[1])))


def _rope_lane_constants():
    lane = jnp.arange(LANES)
    half_a = HEAD_DIM // 2
    inv_a = 1.0 / (ROPE_THETA ** (jnp.arange(half_a, dtype=F32) * (2.0 / HEAD_DIM)))
    freq_a = inv_a[lane % half_a]
    sign_a = jnp.where((lane % HEAD_DIM) < half_a, -1.0, 1.0).astype(F32)
    half_b = MLA_ROPE // 2
    assert HEAD_DIM == 2 * MLA_ROPE
    in_rope = (lane >= MLA_NOPE) & (lane < MLA_NOPE + MLA_ROPE)
    sign_b = jnp.where(in_rope, jnp.where((lane - MLA_NOPE) < half_b, -1.0, 1.0), 0.0).astype(F32)
    rest_b = jnp.where(in_rope, 0.0, 1.0).astype(F32)
    src = jnp.arange(LANES)[:, None]
    packed_tok = jnp.arange(ROPE_PACK)[:, None, None]
    sel_a = (src == packed_tok * half_a + lane % half_a).astype(BF16)
    sel_b = (in_rope & (src == packed_tok * half_a + 2 * ((lane - MLA_NOPE) % half_b))).astype(BF16)
    return _row(freq_a), sel_a, sel_b, _row(sign_a), _row(sign_b), _row(rest_b)


def _mla_slot_weights(w_uq, w_ukv):
    rq = w_uq.shape[0]
    wq = w_uq.reshape(rq, MLA_HEADS, MLA_NOPE + MLA_ROPE)
    wq = jnp.pad(wq, ((0, 0), (0, 0), (0, MLA_SLOT - MLA_NOPE - MLA_ROPE))).reshape(rq, MLA_WIDTH)
    rk = w_ukv.shape[0]
    wkv = w_ukv.reshape(rk, MLA_HEADS, MLA_NOPE + MLA_V)
    wk = jnp.pad(wkv[:, :, :MLA_NOPE], ((0, 0), (0, 0), (0, MLA_SLOT - MLA_NOPE))).reshape(rk, MLA_WIDTH)
    wv = jnp.pad(wkv[:, :, MLA_NOPE:], ((0, 0), (0, 0), (0, MLA_SLOT - MLA_V))).reshape(rk, MLA_WIDTH)
    return wq.astype(BF16), jnp.concatenate([wk, wv], axis=1).astype(BF16)


def kernel(x, positions,
           l0_norm_mix, l0_w_in, l0_mla_q_norm, l0_mla_w_uq, l0_mla_kv_norm, l0_mla_w_ukv, l0_w_out,
           l0_norm_ffn, l0_ffn_w_gate, l0_ffn_w_up, l0_ffn_conv_w, l0_ffn_conv_b, l0_ffn_w_down,
           l1_norm_mix, l1_w_in, l1_ssm_conv_w, l1_ssm_conv_b, l1_ssm_dt_bias, l1_ssm_a_log, l1_ssm_d,
           l1_ssm_norm, l1_w_out,
           l1_norm_ffn, l1_ffn_w_gate, l1_ffn_w_up, l1_ffn_conv_w, l1_ffn_conv_b, l1_ffn_w_down,
           final_norm):
    bsz, seq, d = x.shape
    t = bsz * seq
    xf = x.reshape(t, d)
    tm = ROW_TILE
    span = DSW_BLK * max(dil for _, dil in DSW_GROUPS)
    assert d == D_MODEL and seq % tm == 0 and seq % span == 0 and seq % MLA_TQ == 0
    assert seq % (SB_TQ * SB_TILES_PER_STEP) == 0 and seq % SSM_CHUNK == 0 and t % (ROPE_PACK * tm) == 0

    pos_packed = jnp.repeat(positions.reshape(t // ROPE_PACK, ROPE_PACK).astype(F32), LANES // ROPE_PACK, axis=1)
    cos_a, sin_a, cos_b, sin_b = _rope_tables(pos_packed, *_rope_lane_constants(), rows=tm)

    nd = DSW_WIDTH
    gw = DSW_GROUP_WIDTH
    w_qkv = jnp.concatenate([l0_w_in[:, part * nd + gi * gw:part * nd + (gi + 1) * gw]
                             for gi in range(len(DSW_GROUPS)) for part in range(3)], axis=1).astype(BF16)
    w_cq = l0_w_in[:, 3 * nd:3 * nd + MLA_Q_RANK]
    w_ckv = l0_w_in[:, 3 * nd + MLA_Q_RANK:3 * nd + MLA_Q_RANK + MLA_KV_RANK]
    w_kpe = l0_w_in[:, 3 * nd + MLA_Q_RANK + MLA_KV_RANK:]
    w_kpe_slot = jnp.pad(w_kpe, ((0, 0), (MLA_NOPE, MLA_SLOT - MLA_NOPE - MLA_ROPE)))
    w_lat = jnp.concatenate([w_cq, w_ckv, w_kpe_slot], axis=1).astype(BF16)
    wq_slot, wkv_slot = _mla_slot_weights(l0_mla_w_uq, l0_mla_w_ukv)
    g0, g1, g2, q_mla, kv_mla = _l0_in_proj(xf, _row(l0_norm_mix), w_qkv, w_lat, _row(l0_mla_q_norm), wq_slot,
                                            _row(l0_mla_kv_norm), wkv_slot, cos_a, sin_a, cos_b, sin_b,
                                            bsz, seq, tm)
    y_a = _dilated_attention(g0, g1, g2, bsz, seq)
    y_b = _mla_attention(q_mla, kv_mla, bsz, seq, tq=MLA_TQ, tk=MLA_TK, heads=MLA_HEADS_PER_STEP)

    w_out0 = l0_w_out.astype(BF16)
    x2 = _mix_ffn(xf, y_a, y_b, w_out0[:DSW_GROUP_WIDTH], w_out0[DSW_GROUP_WIDTH:], _row(l0_norm_ffn),
                  l0_ffn_w_gate.astype(BF16), l0_ffn_w_up.astype(BF16), l0_ffn_conv_w, _row(l0_ffn_conv_b),
                  l0_ffn_w_down.astype(BF16), seq, tm=tm)

    o_dt = 2 * SSM_INNER + SSM_BC_WIDTH
    w_ssm = _pad_cols(l1_w_in[:, :o_dt + SSM_HEADS], SSM_PROJ_WIDTH).astype(BF16)
    w_sb = l1_w_in[:, o_dt + SSM_HEADS:].astype(BF16)
    proj, qkv_sb = _l1_in_proj(x2, _row(l1_norm_mix), w_ssm, w_sb, tm)

    cw = l1_ssm_conv_w
    cb = _row(l1_ssm_conv_b)
    y_c = _ssd(proj, bsz, seq, cw[:, :SSM_INNER], cb[:, :SSM_INNER], cw[:, SSM_INNER:], cb[:, SSM_INNER:],
               _pad_cols(_row(l1_ssm_dt_bias), LANES), _pad_cols(_row(l1_ssm_a_log), LANES),
               _row(jnp.repeat(l1_ssm_d, SSM_HEADDIM)), _row(l1_ssm_norm))
    y_d = _sb_attention(qkv_sb, bsz, seq, tq=SB_TQ, tiles=SB_TILES_PER_STEP)

    w_out1 = l1_w_out.astype(BF16)
    out = _mix_ffn(x2, y_c, y_d, w_out1[:SSM_INNER], w_out1[SSM_INNER:], _row(l1_norm_ffn),
                   l1_ffn_w_gate.astype(BF16), l1_ffn_w_up.astype(BF16), l1_ffn_conv_w, _row(l1_ffn_conv_b),
                   l1_ffn_w_down.astype(BF16), seq, tm=tm, final_gain=_row(final_norm))
    return out.reshape(bsz, seq, d)
```

```python
import functools
import math

import jax
import jax.numpy as jnp
from jax import lax
from jax.experimental import pallas as pl
from jax.experimental.pallas import tpu as pltpu

F32 = jnp.float32
BF16 = jnp.bfloat16

LANES = 128
SUBLANES = 8
MXU_WIDTH = 256
VMEM_LIMIT_BYTES = 56 * 1024 * 1024

D_MODEL = 1024
HEAD_DIM = 64
ROPE_THETA = 10000.0
ROPE_PACK = 4
NORM_EPS = 1e-6

DSW_GROUPS = ((128, 1), (512, 4), (2048, 16))
DSW_HEADS_PER_GROUP = 4
DSW_HEADS = DSW_HEADS_PER_GROUP * len(DSW_GROUPS)
DSW_BLK = 128
DSW_WIDTH = DSW_HEADS * HEAD_DIM
DSW_GROUP_WIDTH = DSW_HEADS_PER_GROUP * HEAD_DIM
DSW_UNITS_PER_BODY = 4

MLA_HEADS = 12
MLA_Q_RANK = 256
MLA_KV_RANK = 128
MLA_NOPE = 64
MLA_ROPE = 32
MLA_V = 64
MLA_SLOT = 128
MLA_WIDTH = MLA_HEADS * MLA_SLOT
MLA_LAT_WIDTH =MLA_Q_RANK + MLA_KV_RANK + MLA_SLOT
MLA_Q_SCALE = (MLA_NOPE + MLA_ROPE) ** -0.5 * math.log2(math.e)

SSM_INNER = 1024
SSM_HEADDIM = 64
SSM_HEADS = 16
SSM_STATE = 128
SSM_GROUPS = 2
SSM_CONV = 4
SSM_CHUNK = 128
SSM_CHUNKS_PER_STEP = 2
SSM_GROUP_WIDTH = SSM_INNER // SSM_GROUPS
SSM_BC_WIDTH = 2 * SSM_GROUPS * SSM_STATE
SSM_DT_PAD = LANES
SSM_PROJ_WIDTH = 2 * SSM_INNER + SSM_BC_WIDTH + SSM_DT_PAD

SB_HEADS = 8
SB_WIDTH = SB_HEADS * HEAD_DIM
SB_LOG_UNDERFLOW = 104.0

FFN_DIM = 2816
FFN_CONV = 3

NEG_BIG = -1e30

ROW_TILE = 512
MLA_TQ, MLA_TK, MLA_HEADS_PER_STEP = 1024, 512, 4
SB_TQ, SB_TILES_PER_STEP = 256, 4


def _params(*sem):
    return pltpu.CompilerParams(dimension_semantics=sem, vmem_limit_bytes=VMEM_LIMIT_BYTES)


def _rms(x, g):
    return x * lax.rsqrt(jnp.mean(x * x, axis=-1, keepdims=True) + NORM_EPS) * g


def _silu(x):
    return x * (1.0 / (1.0 + jnp.exp(-x)))


def _log1p(e):
    u = 1.0 + e
    return jnp.where(u == 1.0, e, jnp.log(u) * (e / (u - 1.0)))


def _softplus(x):
    return jnp.maximum(x, 0.0) + _log1p(jnp.exp(-jnp.abs(x)))


def _swap_halves(x, half):
    lane = lax.broadcasted_iota(jnp.int32, x.shape, 1)
    up = pltpu.roll(x, LANES - half, 1)
    down = pltpu.roll(x, half, 1)
    return jnp.where((lane & half) == 0, up, down)


def _rope_tile(x, cos, sin, half):
    return x * cos + _swap_halves(x, half) * sin


def _split2(x):
    hi = x.astype(BF16)
    return hi, (x - hi.astype(F32)).astype(BF16)


def _split3(x):
    hi = x.astype(BF16)
    r1 = x - hi.astype(F32)
    mid = r1.astype(BF16)
    lo = (r1 - mid.astype(F32)).astype(BF16)
    return hi, mid, lo


def _dot(a, b):
    return jnp.dot(a, b, preferred_element_type=F32)


def _dot_nt(a, b):
    return lax.dot_general(a, b, (((1,), (1,)), ((), ())), preferred_element_type=F32)


def _keep_head(x2, lane, second):
    mine = (lane >= HEAD_DIM) if second else (lane < HEAD_DIM)
    return jnp.where(mine, x2.astype(F32), 0.0).astype(BF16)


def _resident(shape):
    return pl.BlockSpec(shape, lambda *_: (0,) * len(shape), pipeline_mode=pl.Buffered(1))


def _rope_table_kernel(pos_ref, freq_ref, sela_ref, selb_ref, signa_ref, signb_ref, restb_ref,
                       cosa_ref, sina_ref, cosb_ref, sinb_ref):
    ang = pos_ref[...] * freq_ref[...]
    cos3 = jnp.concatenate(_split3(jnp.cos(ang)), axis=1)
    sin3 = jnp.concatenate(_split3(jnp.sin(ang)), axis=1)
    rows = ang.shape[0]

    def spread(parts, sel):
        return _dot(parts, jnp.concatenate([sel] * 3, axis=0))

    for g in range(ROPE_PACK):
        tokens = pl.ds(g, rows, stride=ROPE_PACK)
        cosa_ref[tokens, :] = spread(cos3, sela_ref[g])
        sina_ref[tokens, :] = spread(sin3, sela_ref[g]) * signa_ref[...]
        cosb_ref[tokens, :] = spread(cos3, selb_ref[g]) + restb_ref[...]
        sinb_ref[tokens, :] = spread(sin3, selb_ref[g]) * signb_ref[...]


def _rope_tables(pos_packed, freq, sel_a, sel_b, sign_a, sign_b, rest_b, rows):
    packed = pos_packed.shape[0]
    t = packed * ROPE_PACK
    const = pl.BlockSpec((1, LANES), lambda i: (0, 0))
    sel = pl.BlockSpec((ROPE_PACK, LANES, LANES), lambda i: (0, 0, 0))
    out = pl.BlockSpec((rows * ROPE_PACK, LANES), lambda i: (i, 0))
    return pl.pallas_call(
        _rope_table_kernel,
        grid=(packed // rows,),
        in_specs=[pl.BlockSpec((rows, LANES), lambda i: (i, 0)), const, sel, sel, const, const, const],
        out_specs=[out] * 4,
        out_shape=[jax.ShapeDtypeStruct((t, LANES), F32)] * 4,
        compiler_params=_params("parallel"),
        name="rope_tables",
    )(pos_packed, freq, sel_a, sel_b, sign_a, sign_b, rest_b)


def _l0_in_kernel(x_ref, g_ref, wqkv_ref, wlat_ref, gq_ref, wq_ref, gkv_ref, wkv_ref,
                  cosa_ref, sina_ref, cosb_ref, sinb_ref, g0_ref, g1_ref, g2_ref, qm_ref, kvm_ref, slabs):
    tm = x_ref.shape[0]
    hn = _rms(x_ref[...], g_ref[...]).astype(BF16)
    cos_a, sin_a = cosa_ref[...], sina_ref[...]
    cos_b, sin_b = cosb_ref[...], sinb_ref[...]
    half_a, half_b = HEAD_DIM // 2, MLA_ROPE // 2
    step = MXU_WIDTH
    halves = step // LANES
    group_refs = (g0_ref, g1_ref, g2_ref)
    slab = 0
    for gi, (_, dil) in enumerate(DSW_GROUPS):
        for part in range(3):
            c0 = (3 * gi + part) * DSW_GROUP_WIDTH
            acc = _dot(hn, wqkv_ref[:, c0:c0 + step])
            for h in range(halves):
                tile = acc[:, h * LANES:(h + 1) * LANES]
                if part < 2:
                    tile = _rope_tile(tile, cos_a, sin_a, half_a)
                if part == 0:
                    tile = tile * HEAD_DIM ** -0.5
                col = part * DSW_GROUP_WIDTH + h * LANES
                if dil == 1:
                    g0_ref[:, col:col + LANES] = tile.astype(BF16)
                else:
                    slabs[slab] = tile
                    for r in range(dil):
                        rows = slabs[slab, pl.ds(r, tm // dil, stride=dil), :]
                        group_refs[gi][r, :, col:col + LANES] = rows.astype(BF16)
                    slab += 1

    lat = _dot(hn, wlat_ref[...])
    cq = _rms(lat[:, :MLA_Q_RANK], gq_ref[...]).astype(BF16)
    ckv = _rms(lat[:, MLA_Q_RANK:MLA_Q_RANK + MLA_KV_RANK], gkv_ref[...]).astype(BF16)
    kpe = _rope_tile(lat[:, MLA_Q_RANK + MLA_KV_RANK:], cos_b, sin_b, half_b)
    for c0 in range(0, MLA_WIDTH, step):
        acc = _dot(cq, wq_ref[:, c0:c0 + step])
        for h in range(halves):
            sl = slice(h * LANES, (h + 1) * LANES)
            tile = _rope_tile(acc[:, sl], cos_b, sin_b, half_b) * MLA_Q_SCALE
            qm_ref[:, c0 + h * LANES:c0 + (h + 1) * LANES] = tile.astype(BF16)
    lane = lax.broadcasted_iota(jnp.int32, kpe.shape, 1)
    one_hot = jnp.where(lane == MLA_V, 1.0, 0.0)
    for c0 in range(0, 2 * MLA_WIDTH, step):
        acc = _dot(ckv, wkv_ref[:, c0:c0 + step])
        extra = kpe if c0 < MLA_WIDTH else one_hot
        for h in range(halves):
            sl = slice(h * LANES, (h + 1) * LANES)
            kvm_ref[:, c0 + h * LANES:c0 + (h + 1) * LANES] = (acc[:, sl] + extra).astype(BF16)


def _l0_in_proj(x, g, w_qkv, w_lat, gq, wq, gkv, wkv, cos_a, sin_a, cos_b, sin_b, bsz, seq, tm):
    t, d = x.shape
    tps = seq // tm
    gw = 3 * DSW_GROUP_WIDTH
    row = lambda w: pl.BlockSpec((tm, w), lambda i: (i, 0))

    def residue_major(dil):
        return pl.BlockSpec((None, dil, tm // dil, gw), lambda i: (i // tps, 0, i % tps, 0))

    dils = [dil for _, dil in DSW_GROUPS]
    n_slabs = sum(gw // LANES for dil in dils if dil > 1)
    return pl.pallas_call(
        _l0_in_kernel,
        grid=(t // tm,),
        in_specs=[row(d), _resident((1, d)), _resident(w_qkv.shape), _resident(w_lat.shape),
                  _resident(gq.shape), _resident(wq.shape), _resident(gkv.shape), _resident(wkv.shape),
                  row(LANES), row(LANES), row(LANES), row(LANES)],
        out_specs=[row(gw), residue_major(dils[1]), residue_major(dils[2]), row(MLA_WIDTH), row(2 * MLA_WIDTH)],
        out_shape=[jax.ShapeDtypeStruct((t, gw), BF16),
                   jax.ShapeDtypeStruct((bsz, dils[1], seq // dils[1], gw), BF16),
                   jax.ShapeDtypeStruct((bsz, dils[2], seq // dils[2], gw), BF16),
                   jax.ShapeDtypeStruct((t, MLA_WIDTH), BF16),
                   jax.ShapeDtypeStruct((t, 2 * MLA_WIDTH), BF16)],
        scratch_shapes=[pltpu.VMEM((n_slabs, tm, LANES), F32)],
        compiler_params=_params("parallel"),
        name="l0_in_proj",
    )(x, g, w_qkv, w_lat, gq, wq, gkv, wkv, cos_a, sin_a, cos_b, sin_b)


def _l1_in_kernel(x_ref, g_ref, wssm_ref, wsb_ref, proj_ref, qkv_ref):
    hn = _rms(x_ref[...], g_ref[...]).astype(BF16)
    step = MXU_WIDTH
    for c0 in range(0, SSM_PROJ_WIDTH, step):
        c1 = min(c0 + step, SSM_PROJ_WIDTH)
        proj_ref[:, c0:c1] = _dot(hn, wssm_ref[:, c0:c1])
    for c0 in range(0, 3 * SB_WIDTH, step):
        acc = _dot(hn, wsb_ref[:, c0:c0 + step])
        if c0 < SB_WIDTH:
            acc = acc * HEAD_DIM ** -0.5
        qkv_ref[:, c0:c0 + step] = acc.astype(BF16)


def _l1_in_proj(x, g, w_ssm, w_sb, tm):
    t, d = x.shape
    row = lambda w: pl.BlockSpec((tm, w), lambda i: (i, 0))
    return pl.pallas_call(
        _l1_in_kernel,
        grid=(t // tm,),
        in_specs=[row(d), _resident((1, d)), _resident(w_ssm.shape), _resident(w_sb.shape)],
        out_specs=[row(SSM_PROJ_WIDTH), row(3 * SB_WIDTH)],
        out_shape=[jax.ShapeDtypeStruct((t, SSM_PROJ_WIDTH), F32),
                   jax.ShapeDtypeStruct((t, 3 * SB_WIDTH), BF16)],
        compiler_params=_params("parallel"),
        name="l1_in_proj",
    )(x, g, w_ssm, w_sb)


def _dilated_kernel(c0_ref, p0_ref, c1_ref, p1_ref, c2_ref, p2_ref, y_ref, o_acc, m_acc, l_acc):
    blk = DSW_BLK
    gw = DSW_GROUP_WIDTH
    pairs = gw // LANES
    qcols, kcols, vcols = slice(0, gw), slice(gw, 2 * gw), slice(2 * gw, 3 * gw)
    first_key = jnp.where(pl.program_id(1) > 0, 0, blk)
    qi = lax.broadcasted_iota(jnp.int32, (blk, 2 * blk), 0)
    kj = lax.broadcasted_iota(jnp.int32, (blk, 2 * blk), 1)
    window = (kj >= qi) & (kj <= qi + blk)
    lane = lax.broadcasted_iota(jnp.int32, (blk, LANES), 1)
    first = lane < HEAD_DIM

    def unit(q, k2, v2, lowest_key):
        valid = window & (kj >= lowest_key)
        res = []
        for pair in range(pairs):
            sl = slice(pair * LANES, (pair + 1) * LANES)
            kk, vv = k2[:, sl], v2[:, sl]
            outs, maxes, sums = [], [], []
            for hh in range(2):
                s = jnp.where(valid, _dot_nt(_keep_head(q[:, sl], lane, hh), kk), NEG_BIG)
                m = jnp.max(s, axis=-1, keepdims=True)
                p = jnp.exp(s - m)
                outs.append(_dot(p.astype(BF16), vv))
                maxes.append(jnp.broadcast_to(m, (blk, LANES)))
                sums.append(jnp.broadcast_to(jnp.sum(p, axis=-1, keepdims=True), (blk, LANES)))
            res.append(tuple(jnp.where(first, a, b) for a, b in (outs, maxes, sums)))
        return res

    def put(rows, res):
        for pair, (o, m, l) in enumerate(res):
            o_acc[pair, rows, :] = o
            m_acc[pair, rows, :] = m
            l_acc[pair, rows, :] = l

    def merge(rows, res):
        for pair, (o, m, l) in enumerate(res):
            m_old = m_acc[pair, rows, :]
            m_new = jnp.maximum(m_old, m)
            keep, add = jnp.exp(m_old - m_new), jnp.exp(m - m_new)
            o_acc[pair, rows, :] = o_acc[pair, rows, :] * keep + o * add
            l_acc[pair, rows, :] = l_acc[pair, rows, :] * keep + l * add
            m_acc[pair, rows, :] = m_new

    def stacked(prev, cur):
        return jnp.concatenate([prev, cur], axis=0)

    head = slice(0, blk)

    def head_unit(cur, prev):
        return unit(cur(head, qcols), stacked(prev(kcols), cur(head, kcols)),
                    stacked(prev(vcols), cur(head, vcols)), first_key)

    def later_unit(cur, r0):
        keys = pl.ds(r0 - blk, 2 * blk)
        return unit(cur(pl.ds(r0, blk), qcols), cur(keys, kcols), cur(keys, vcols), 0)

    units_per_body = DSW_UNITS_PER_BODY

    cur0 = lambda rows, cols: c0_ref[rows, cols]
    put(head, head_unit(cur0, lambda cols: p0_ref[:, cols]))
    later_blocks = c0_ref.shape[0] // blk - 1
    group0 = 5
    assert later_blocks % group0 == 0

    def dense_blocks(u, carry):
        for k in range(group0):
            r0 = pl.multiple_of((1 + u * group0 + k) * blk, blk)
            put(pl.ds(r0, blk), later_unit(cur0, r0))
        return carry

    lax.fori_loop(0, later_blocks // group0, dense_blocks, 0)

    for cur_ref, prev_ref in ((c1_ref, p1_ref), (c2_ref, p2_ref)):
        dil, blocks = cur_ref.shape[0], cur_ref.shape[1] // blk
        residues_per_body = max(1, units_per_body // blocks)
        assert dil % residues_per_body == 0

        def residues(u, carry, cur_ref=cur_ref, prev_ref=prev_ref, dil=dil, blocks=blocks,
                     residues_per_body=residues_per_body):
            for k in range(residues_per_body):
                r = u * residues_per_body + k
                cur = lambda rows, cols, r=r: cur_ref[r, rows, cols]
                merge(pl.ds(r, blk, stride=dil), head_unit(cur, lambda cols, r=r: prev_ref[r, :, cols]))
                for n in range(1, blocks):
                    merge(pl.ds(n * blk * dil + r, blk, stride=dil), later_unit(cur, n * blk))
            return carry

        lax.fori_loop(0, dil // residues_per_body, residues, 0)

    for pair in range(pairs):
        y_ref[:, pair * LANES:(pair + 1) * LANES] = (o_acc[pair] * (1.0 / l_acc[pair])).astype(y_ref.dtype)


def _dilated_attention(g0, g1, g2, bsz, seq):
    blk = DSW_BLK
    span = blk * max(dil for _, dil in DSW_GROUPS)
    nspan = seq // span
    gw = 3 * DSW_GROUP_WIDTH
    d1, d2 = g1.shape[1], g2.shape[1]
    blocks_per_span0 = span // blk

    def residue_major(dil, rows, prev):
        per_span = span // dil // rows
        if prev:
            return pl.BlockSpec((None, dil, rows, gw), lambda b, s: (b, 0, jnp.maximum(s * per_span - 1, 0), 0))
        return pl.BlockSpec((None, dil, rows, gw), lambda b, s: (b, 0, s, 0))

    return pl.pallas_call(
        _dilated_kernel,
        grid=(bsz, nspan),
        in_specs=[
            pl.BlockSpec((span, gw), lambda b, s: (b * nspan + s, 0)),
            pl.BlockSpec((blk, gw), lambda b, s: (jnp.maximum((b * nspan + s) * blocks_per_span0 - 1, 0), 0)),
            residue_major(d1, span // d1, False), residue_major(d1, blk, True),
            residue_major(d2, span // d2, False), residue_major(d2, blk, True),
        ],
        out_specs=pl.BlockSpec((span, DSW_GROUP_WIDTH), lambda b, s: (b * nspan + s, 0)),
        out_shape=jax.ShapeDtypeStruct((bsz * seq, DSW_GROUP_WIDTH), BF16),
        scratch_shapes=[pltpu.VMEM((DSW_GROUP_WIDTH // LANES, span, LANES), F32)] * 3,
        compiler_params=_params("parallel", "arbitrary"),
        name="dilated_attention",
    )(g0, g0, g1, g1, g2, g2)


def _mla_attn_kernel(q_ref, k_ref, v_ref, o_ref, *, tq, tk, heads):
    i = pl.program_id(2)
    assert (tq // tk) % 2 == 0 and heads % 2 == 0
    lane = lax.broadcasted_iota(jnp.int32, (tq, LANES), 1)
    slots = [slice(hh * MLA_SLOT, (hh + 1) * MLA_SLOT) for hh in range(heads)]

    def update(q, m, acc, start, width, slot, visible=None):
        s = _dot_nt(q, k_ref[pl.ds(start, width), slot])
        if visible is not None:
            s = jnp.where(visible, s, NEG_BIG)
        m_new = jnp.maximum(m, jnp.max(s, axis=-1, keepdims=True))
        p = jnp.exp2(s - m_new)
        return m_new, jnp.exp2(m - m_new) * acc + _dot(p.astype(BF16), v_ref[pl.ds(start, width), slot])

    def block(j, carry):
        start = pl.multiple_of(j * tk, tk)
        return tuple(update(q_ref[:, slots[hh]], *carry[hh], start, tk, slots[hh]) for hh in range(heads))

    def block_pair(jp, carry):
        return block(2 * jp + 1, block(2 * jp, carry))

    init = tuple((jnp.full((tq, 1), NEG_BIG, F32), jnp.zeros((tq, LANES), F32)) for _ in range(heads))
    diag_blocks = tq // tk
    carry = lax.fori_loop(0, i * (diag_blocks // 2), block_pair, init)
    for b in range(diag_blocks):
        r0 = b * tk
        row = lax.broadcasted_iota(jnp.int32, (tq - r0, tk), 0)
        col = lax.broadcasted_iota(jnp.int32, (tq - r0, tk), 1)
        start = pl.multiple_of((i * diag_blocks + b) * tk, tk)
        new = []
        for hh in range(heads):
            m, acc = carry[hh]
            m_low, acc_low = update(q_ref[r0:, slots[hh]], m[r0:], acc[r0:], start, tk, slots[hh], col <= row)
            if r0:
                m_low = jnp.concatenate([m[:r0], m_low], axis=0)
                acc_low = jnp.concatenate([acc[:r0], acc_low], axis=0)
            new.append((m_low, acc_low))
        carry = tuple(new)
    outs = [acc * (1.0 / acc[:, MLA_V:MLA_V + 1]) for _, acc in carry]
    for pr in range(heads // 2):
        packed = jnp.where(lane < MLA_V, outs[2 * pr], pltpu.roll(outs[2 * pr + 1], MLA_V, 1))
        o_ref[:, pr * LANES:(pr + 1) * LANES] = packed.astype(o_ref.dtype)


def _mla_attention(q, kv, bsz, seq, tq, tk, heads):
    nq = seq // tq
    groups = MLA_HEADS // heads
    kern = functools.partial(_mla_attn_kernel, tq=tq, tk=tk, heads=heads)
    return pl.pallas_call(
        kern,
        grid=(bsz, groups, nq),
        in_specs=[
            pl.BlockSpec((tq, heads * MLA_SLOT), lambda b, p, i: (b * nq + i, p)),
            pl.BlockSpec((seq, heads * MLA_SLOT), lambda b, p, i: (b, p)),
            pl.BlockSpec((seq, heads * MLA_SLOT), lambda b, p, i: (b, groups + p)),
        ],
        out_specs=pl.BlockSpec((tq, heads * MLA_V), lambda b, p, i: (b * nq + i, p)),
        out_shape=jax.ShapeDtypeStruct((bsz * seq, MLA_HEADS * MLA_V), BF16),
        compiler_params=_params("parallel", "parallel", "arbitrary"),
        name="mla_attention",
    )(q, kv, kv)


def _sb_attn_kernel(q_ref, k_ref, v_ref, o_ref, *, tq, tiles):
    first = pl.program_id(2) * tiles
    lane = lax.broadcasted_iota(jnp.int32, (tq, LANES), 1)
    row = lax.broadcasted_iota(jnp.int32, (tq, tq), 0)
    col = lax.broadcasted_iota(jnp.int32, (tq, tq), 1)
    later = jnp.where(row > col, 1.0, 0.0).astype(BF16)
    later2 = jnp.concatenate([later, later], axis=0)
    strict = jnp.concatenate([col < row] * 2, axis=0)
    qs = [jnp.concatenate([_keep_head(q_ref[t * tq:(t + 1) * tq, :], lane, hh) for hh in range(2)], axis=0)
          for t in range(tiles)]

    def block(q2, blk, run, acc, masked, present=None):
        start = pl.multiple_of(blk * tq, tq)
        z = _dot_nt(q2, k_ref[pl.ds(start, tq), :])
        log_beta = jnp.minimum(z, 0.0) - jnp.log(1.0 + jnp.exp(-jnp.abs(z)))
        log_stay = log_beta - z
        if masked:
            log_stay = jnp.where(strict, log_stay, 0.0)
        if present is not None:
            log_stay = log_stay * present
        after = _dot(jnp.concatenate(_split2(log_stay), axis=1), later2) + run
        w = jnp.exp(log_beta + after)
        if masked:
            w = jnp.where(strict, w, 0.0)
        if present is not None:
            w = w * present
        acc = acc + _dot(w.astype(BF16), v_ref[pl.ds(start, tq), :])
        return run + jnp.sum(log_stay, axis=-1, keepdims=True), acc

    run0 = jnp.zeros((2 * tq, 1), F32)
    acc0 = jnp.zeros((2 * tq, LANES), F32)
    has_left = jnp.where(first > 0, 1.0, 0.0)
    states = []
    for t in range(tiles):
        run, acc = block(qs[t], first + t, run0, acc0, True)
        if t == 0:
            states.append(block(qs[t], jnp.maximum(first - 1, 0), run, acc, False, has_left))
        else:
            states.append(block(qs[t], first + t - 1, run, acc, False))

    def may_matter(run):
        return jnp.max(run) >= -SB_LOG_UNDERFLOW

    alive = [may_matter(run) for run, _ in states]
    for t in range(tiles):

        def cond(c):
            left, alive_t, _, _ = c
            return (left > 0) & alive_t

        def body(c, t=t):
            left, _, run, acc = c
            run, acc = block(qs[t], left - 1, run, acc, False)
            return left - 1, may_matter(run), run, acc

        acc = lax.while_loop(cond, body, (jnp.maximum(first + t - 1, 0), alive[t]) + states[t])[3]
        o_ref[t * tq:(t + 1) * tq, :] = jnp.where(lane < HEAD_DIM, acc[:tq], acc[tq:]).astype(o_ref.dtype)


def _sb_attention(qkv, bsz, seq, tq, tiles):
    nq = seq // (tq * tiles)
    pairs = SB_WIDTH // LANES
    kern = functools.partial(_sb_attn_kernel, tq=tq, tiles=tiles)
    return pl.pallas_call(
        kern,
        grid=(bsz, pairs, nq),
        in_specs=[
            pl.BlockSpec((tq * tiles, LANES), lambda b, p, i: (b * nq + i, p)),
            pl.BlockSpec((seq, LANES), lambda b, p, i: (b, pairs + p)),
            pl.BlockSpec((seq, LANES), lambda b, p, i: (b, 2 * pairs + p)),
        ],
        out_specs=pl.BlockSpec((tq * tiles, LANES), lambda b, p, i: (b * nq + i, p)),
        out_shape=jax.ShapeDtypeStruct((bsz * seq, SB_WIDTH), BF16),
        compiler_params=_params("parallel", "parallel", "arbitrary"),
        name="stickbreak_attention",
    )(qkv, qkv, qkv)


def _ssd_kernel(z_ref, xs_ref, bc_ref, dt_ref, cwx_ref, cbx_ref, cwb_ref, cbb_ref, dtb_ref, alog_ref, dskip_ref,
                gn_ref, y_ref, xtail, btail, state):
    c = pl.program_id(1)
    cl = SSM_CHUNK
    halo = SUBLANES
    assert SSM_CONV == 4

    @pl.when(c == 0)
    def _():
        xtail[...] = jnp.zeros_like(xtail)
        btail[...] = jnp.zeros_like(btail)
        state[...] = jnp.zeros_like(state)

    def shift_rows(x, tail, k):
        rolled = pltpu.roll(x, k, 0)
        wrapped = pltpu.roll(tail, k, 0)
        row = lax.broadcasted_iota(jnp.int32, tail.shape, 0)
        return jnp.concatenate([jnp.where(row < k, wrapped, rolled[:halo]), rolled[halo:]], axis=0)

    def conv_silu(tails, raw_ref, rows, w_ref, b_ref):
        x = raw_ref[rows, :]
        x1 = shift_rows(x, tails[0], 1)
        u = w_ref[1:2, :] * x + w_ref[0:1, :] * x1
        y = b_ref[...] + w_ref[3:4, :] * x + w_ref[2:3, :] * x1 + shift_rows(u, tails[1], 2)
        tails[0] = x[cl - halo:, :]
        tails[1] = u[cl - halo:, :]
        return _silu(y)

    row = lax.broadcasted_iota(jnp.int32, (cl, cl), 0)
    col = lax.broadcasted_iota(jnp.int32, (cl, cl), 1)
    causal = col <= row
    tri3 = jnp.concatenate([jnp.where(causal, 1.0, 0.0).astype(BF16)] * 3, axis=1)
    head_of_lane = jnp.right_shift(lax.broadcasted_iota(jnp.int32, (LANES, SSM_INNER), 1), 6)
    expand = jnp.where(lax.broadcasted_iota(jnp.int32, (LANES, SSM_INNER), 0) == head_of_lane, 1.0, 0.0).astype(BF16)
    expand3 = jnp.concatenate([expand] * 3, axis=0)
    lane = lax.broadcasted_iota(jnp.int32, (cl, LANES), 1)
    first = lane < SSM_HEADDIM
    gs = SSM_STATE
    heads_per_group = SSM_HEADS // SSM_GROUPS

    def per_head_lanes(v):
        return _dot(jnp.concatenate(_split3(v), axis=1), expand3)

    def chunk(rows):
        xs = conv_silu(xtail, xs_ref, rows, cwx_ref, cbx_ref)
        bc = conv_silu(btail, bc_ref, rows, cwb_ref, cbb_ref)

        dt = _softplus(dt_ref[rows, :] + dtb_ref[...])
        da = dt * (-jnp.exp(alog_ref[...]))
        cs = _dot(tri3, jnp.concatenate(_split3(da), axis=0))
        cs_t = cs.T
        ecs = jnp.exp(cs)
        dec_end = jnp.exp(cs[cl - 1:cl, :] - cs)
        dt_e = per_head_lanes(dt)
        ecs_e = per_head_lanes(ecs)
        dec_end_e = per_head_lanes(dec_end)

        xdt = xs * dt_e
        xdt_b = xdt.astype(BF16)
        xw_b = (xdt * dec_end_e).astype(BF16)
        for g in range(SSM_GROUPS):
            bg = bc[:, g * gs:(g + 1) * gs]
            cg_b = bc[:, (SSM_GROUPS + g) * gs:(SSM_GROUPS + g + 1) * gs].astype(BF16)
            cb = _dot_nt(cg_b, bg.astype(BF16))
            cols = slice(g * SSM_GROUP_WIDTH, (g + 1) * SSM_GROUP_WIDTH)
            prev = state[:, cols]
            y_off = _dot(cg_b, prev.astype(BF16)) * ecs_e[:, cols]
            state[:, cols] = prev * ecs_e[cl - 1:cl, cols] + _dot(bg.T.astype(BF16), xw_b[:, cols])
            for pr in range(heads_per_group // 2):
                pcols = slice(g * SSM_GROUP_WIDTH + pr * LANES, g * SSM_GROUP_WIDTH + (pr + 1) * LANES)
                x_pair = xdt_b[:, pcols]
                ys = []
                for hh in range(2):
                    h = g * heads_per_group + 2 * pr + hh
                    seg = jnp.where(causal, cs[:, h:h + 1] - cs_t[h:h + 1, :], NEG_BIG)
                    ys.append(_dot((cb * jnp.exp(seg)).astype(BF16), x_pair))
                y_diag = jnp.where(first, ys[0], ys[1])
                y_pair = y_diag + y_off[:, pr * LANES:(pr + 1) * LANES] + xs[:, pcols] * dskip_ref[:, pcols]
                y_ref[rows, pcols] = y_pair * _silu(z_ref[rows, pcols])
        for g in range(SSM_GROUPS):
            cols = slice(g * SSM_GROUP_WIDTH, (g + 1) * SSM_GROUP_WIDTH)
            y_ref[rows, cols] = _rms(y_ref[rows, cols], gn_ref[:, cols])

    for k in range(SSM_CHUNKS_PER_STEP):
        chunk(slice(k * cl, (k + 1) * cl))


def _ssd(proj, bsz, seq, cwx, cbx, cwb, cbb, dt_bias, a_log, d_skip, gnorm):
    cl = SSM_CHUNK * SSM_CHUNKS_PER_STEP
    nc = seq // cl

    def rows(width, colblk):
        return pl.BlockSpec((cl, width), lambda b, c: (b * nc + c, colblk))

    def const(r, width):
        return pl.BlockSpec((r, width), lambda b, c: (0, 0))

    return pl.pallas_call(
        _ssd_kernel,
        grid=(bsz, nc),
        in_specs=[
            rows(SSM_INNER, 0),
            rows(SSM_INNER, 1),
            rows(SSM_BC_WIDTH, 2 * SSM_INNER // SSM_BC_WIDTH),
            rows(LANES, (2 * SSM_INNER + SSM_BC_WIDTH) // LANES),
            const(SSM_CONV, SSM_INNER), const(1, SSM_INNER), const(SSM_CONV, SSM_BC_WIDTH), const(1, SSM_BC_WIDTH),
            const(1, LANES), const(1, LANES), const(1, SSM_INNER), const(1, SSM_INNER),
        ],
        out_specs=pl.BlockSpec((cl, SSM_INNER), lambda b, c: (b * nc + c, 0)),
        out_shape=jax.ShapeDtypeStruct((bsz * seq, SSM_INNER), F32),
        scratch_shapes=[
            pltpu.VMEM((2, SUBLANES, SSM_INNER), F32),
            pltpu.VMEM((2, SUBLANES, SSM_BC_WIDTH), F32),
            pltpu.VMEM((SSM_STATE, SSM_INNER), F32),
        ],
        compiler_params=_params("parallel", "arbitrary"),
        name="ssd_scan",
    )(proj, proj, proj, proj, cwx, cbx, cwb, cbb, dt_bias, a_log, d_skip, gnorm)


def _mix_ffn_kernel(*refs, tm, tiles_per_seq, final_norm):
    x_ref, a_ref, b_ref, wa_ref, wb_ref, g_ref, wg_ref, wu_ref, cw_ref, cb_ref, wd_ref = refs[:11]
    pos = 11
    fn_ref = None
    if final_norm:
        fn_ref = refs[pos]
        pos += 1
    o_ref, gbuf, act_ref = refs[pos:pos + 3]
    halo = SUBLANES

    o_ref[...] = x_ref[...] + _dot(a_ref[...].astype(BF16), wa_ref[...]) + _dot(b_ref[...].astype(BF16), wb_ref[...])
    hn = _rms(o_ref[...], g_ref[...]).astype(BF16)

    @pl.when(pl.program_id(0) % tiles_per_seq == 0)
    def _():
        gbuf[0:halo, :] = jnp.zeros((halo, gbuf.shape[1]), F32)

    step = MXU_WIDTH
    for c0 in range(0, FFN_DIM, step):
        cols = slice(c0, c0 + step)
        gate = _dot(hn, wg_ref[:, cols])
        up = _dot(hn, wu_ref[:, cols])
        gbuf[halo:halo + tm, cols] = gate
        conv = cb_ref[:, cols] + cw_ref[FFN_CONV - 1:FFN_CONV, cols] * gate
        for t in range(FFN_CONV - 1):
            back = FFN_CONV - 1 - t
            conv = conv + cw_ref[t:t + 1, cols] * gbuf[halo - back:halo - back + tm, cols]
        gbuf[0:halo, cols] = gbuf[tm:tm + halo, cols]
        act_ref[:, cols] = (_silu(conv) * up).astype(BF16)

    y = o_ref[...] + _dot(act_ref[...], wd_ref[...])
    if final_norm:
        y = _rms(y, fn_ref[...])
    o_ref[...] = y


def _mix_ffn(x, a, b, wa, wb, g, wg, wu, cw, cb, wd, seq, *, tm, final_gain=None):
    t, d = x.shape
    f = wg.shape[1]
    final_norm = final_gain is not None
    row = lambda w: pl.BlockSpec((tm, w), lambda i: (i, 0))
    in_specs = [row(d), row(a.shape[1]), row(b.shape[1]), _resident(wa.shape), _resident(wb.shape),
                _resident(g.shape), _resident(wg.shape), _resident(wu.shape), _resident(cw.shape),
                _resident(cb.shape), _resident(wd.shape)]
    args = [x, a, b, wa, wb, g, wg, wu, cw, cb, wd]
    if final_norm:
        in_specs.append(_resident(final_gain.shape))
        args.append(final_gain)
    kern = functools.partial(_mix_ffn_kernel, tm=tm, tiles_per_seq=seq // tm, final_norm=final_norm)
    return pl.pallas_call(
        kern,
        grid=(t // tm,),
        in_specs=in_specs,
        out_specs=row(d),
        out_shape=jax.ShapeDtypeStruct((t, d), F32),
        scratch_shapes=[pltpu.VMEM((tm + SUBLANES, f), F32), pltpu.VMEM((tm, f), BF16)],
        compiler_params=_params("arbitrary"),
        name="mix_ffn",
    )(*args)


def _row(v):
    return v.reshape(1, -1).astype(F32)


def _pad_cols(w, n):
    return jnp.pad(w, ((0, 0), (0, n - w.shape[1])))


def _rope_lane_constants():
    lane = jnp.arange(LANES)
    half_a = HEAD_DIM // 2
    inv_a = 1.0 / (ROPE_THETA ** (jnp.arange(half_a, dtype=F32) * (2.0 / HEAD_DIM)))
    freq_a = inv_a[lane % half_a]
    sign_a = jnp.where((lane % HEAD_DIM) < half_a, -1.0, 1.0).astype(F32)
    half_b = MLA_ROPE // 2
    assert HEAD_DIM == 2 * MLA_ROPE
    in_rope = (lane >= MLA_NOPE) & (lane < MLA_NOPE + MLA_ROPE)
    sign_b = jnp.where(in_rope, jnp.where((lane - MLA_NOPE) < half_b, -1.0, 1.0), 0.0).astype(F32)
    rest_b = jnp.where(in_rope, 0.0, 1.0).astype(F32)
    src = jnp.arange(LANES)[:, None]
    packed_tok = jnp.arange(ROPE_PACK)[:, None, None]
    sel_a = (src == packed_tok * half_a + lane % half_a).astype(BF16)
    sel_b = (in_rope & (src == packed_tok * half_a + 2 * ((lane - MLA_NOPE) % half_b))).astype(BF16)
    return _row(freq_a), sel_a, sel_b, _row(sign_a), _row(sign_b), _row(rest_b)


def _mla_slot_weights(w_uq, w_ukv):
    rq = w_uq.shape[0]
    wq = w_uq.reshape(rq, MLA_HEADS, MLA_NOPE + MLA_ROPE)
    wq = jnp.pad(wq, ((0, 0), (0, 0), (0, MLA_SLOT - MLA_NOPE - MLA_ROPE))).reshape(rq, MLA_WIDTH)
    rk = w_ukv.shape[0]
    wkv = w_ukv.reshape(rk, MLA_HEADS, MLA_NOPE + MLA_V)
    wk = jnp.pad(wkv[:, :, :MLA_NOPE], ((0, 0), (0, 0), (0, MLA_SLOT - MLA_NOPE))).reshape(rk, MLA_WIDTH)
    wv = jnp.pad(wkv[:, :, MLA_NOPE:], ((0, 0), (0, 0), (0, MLA_SLOT - MLA_V))).reshape(rk, MLA_WIDTH)
    return wq.astype(BF16), jnp.concatenate([wk, wv], axis=1).astype(BF16)


def kernel(x, positions,
           l0_norm_mix, l0_w_in, l0_mla_q_norm, l0_mla_w_uq, l0_mla_kv_norm, l0_mla_w_ukv, l0_w_out,
           l0_norm_ffn, l0_ffn_w_gate, l0_ffn_w_up, l0_ffn_conv_w, l0_ffn_conv_b, l0_ffn_w_down,
           l1_norm_mix, l1_w_in, l1_ssm_conv_w, l1_ssm_conv_b, l1_ssm_dt_bias, l1_ssm_a_log, l1_ssm_d,
           l1_ssm_norm, l1_w_out,
           l1_norm_ffn, l1_ffn_w_gate, l1_ffn_w_up, l1_ffn_conv_w, l1_ffn_conv_b, l1_ffn_w_down,
           final_norm):
    bsz, seq, d = x.shape
    t = bsz * seq
    xf = x.reshape(t, d)
    tm = ROW_TILE
    span = DSW_BLK * max(dil for _, dil in DSW_GROUPS)
    assert d == D_MODEL and seq % tm == 0 and seq % span == 0 and seq % MLA_TQ == 0
    assert seq % (SB_TQ * SB_TILES_PER_STEP) == 0 and seq % (SSM_CHUNK * SSM_CHUNKS_PER_STEP) == 0 and t % (ROPE_PACK * tm) == 0

    pos_packed = jnp.repeat(positions.reshape(t // ROPE_PACK, ROPE_PACK).astype(F32), LANES // ROPE_PACK, axis=1)
    cos_a, sin_a, cos_b, sin_b = _rope_tables(pos_packed, *_rope_lane_constants(), rows=tm)

    nd = DSW_WIDTH
    gw = DSW_GROUP_WIDTH
    w_qkv = jnp.concatenate([l0_w_in[:, part * nd + gi * gw:part * nd + (gi + 1) * gw]
                             for gi in range(len(DSW_GROUPS)) for part in range(3)], axis=1).astype(BF16)
    w_cq = l0_w_in[:, 3 * nd:3 * nd + MLA_Q_RANK]
    w_ckv = l0_w_in[:, 3 * nd + MLA_Q_RANK:3 * nd + MLA_Q_RANK + MLA_KV_RANK]
    w_kpe = l0_w_in[:, 3 * nd + MLA_Q_RANK + MLA_KV_RANK:]
    w_kpe_slot = jnp.pad(w_kpe, ((0, 0), (MLA_NOPE, MLA_SLOT - MLA_NOPE - MLA_ROPE)))
    w_lat = jnp.concatenate([w_cq, w_ckv, w_kpe_slot], axis=1).astype(BF16)
    wq_slot, wkv_slot = _mla_slot_weights(l0_mla_w_uq, l0_mla_w_ukv)
    g0, g1, g2, q_mla, kv_mla = _l0_in_proj(xf, _row(l0_norm_mix), w_qkv, w_lat, _row(l0_mla_q_norm), wq_slot,
                                            _row(l0_mla_kv_norm), wkv_slot, cos_a, sin_a, cos_b, sin_b,
                                            bsz, seq, tm)
    y_a = _dilated_attention(g0, g1, g2, bsz, seq)
    y_b = _mla_attention(q_mla, kv_mla, bsz, seq, tq=MLA_TQ, tk=MLA_TK, heads=MLA_HEADS_PER_STEP)

    w_out0 = l0_w_out.astype(BF16)
    x2 = _mix_ffn(xf, y_a, y_b, w_out0[:DSW_GROUP_WIDTH], w_out0[DSW_GROUP_WIDTH:], _row(l0_norm_ffn),
                  l0_ffn_w_gate.astype(BF16), l0_ffn_w_up.astype(BF16), l0_ffn_conv_w, _row(l0_ffn_conv_b),
                  l0_ffn_w_down.astype(BF16), seq, tm=tm)

    o_dt = 2 * SSM_INNER + SSM_BC_WIDTH
    w_ssm = _pad_cols(l1_w_in[:, :o_dt + SSM_HEADS], SSM_PROJ_WIDTH).astype(BF16)
    w_sb = l1_w_in[:, o_dt + SSM_HEADS:].astype(BF16)
    proj, qkv_sb = _l1_in_proj(x2, _row(l1_norm_mix), w_ssm, w_sb, tm)

    cw = l1_ssm_conv_w
    cb = _row(l1_ssm_conv_b)
    y_c = _ssd(proj, bsz, seq, cw[:, :SSM_INNER], cb[:, :SSM_INNER], cw[:, SSM_INNER:], cb[:, SSM_INNER:],
               _pad_cols(_row(l1_ssm_dt_bias), LANES), _pad_cols(_row(l1_ssm_a_log), LANES),
               _row(jnp.repeat(l1_ssm_d, SSM_HEADDIM)), _row(l1_ssm_norm))
    y_d = _sb_attention(qkv_sb, bsz, seq, tq=SB_TQ, tiles=SB_TILES_PER_STEP)

    w_out1 = l1_w_out.astype(BF16)
    out = _mix_ffn(x2, y_c, y_d, w_out1[:SSM_INNER], w_out1[SSM_INNER:], _row(l1_norm_ffn),
                   l1_ffn_w_gate.astype(BF16), l1_ffn_w_up.astype(BF16), l1_ffn_conv_w, _row(l1_ffn_conv_b),
                   l1_ffn_w_down.astype(BF16), seq, tm=tm, final_gain=_row(final_norm))
    return out.reshape(bsz, seq, d)
```
